```python
import math
import jax, jax.numpy as jnp
from jax import lax
import numpy as np

D_MODEL = 1024
BATCH = 4
SEQ = 8192
DEPTH = 2

N_META = 16
EPS = 1e-6
LRU_WIDTH = D_MODEL
LRU_BLOCKS = 4
LRU_BLOCK = LRU_WIDTH // LRU_BLOCKS
CONV_WIDTH = 4
LRU_C = 8.0
SSM_WIDTH = D_MODEL // 2
SSM_GROUP = 16
SSM_GROUPS = SSM_WIDTH // SSM_GROUP
SSM_STATE = 64
EVEN_IN = 2 * LRU_WIDTH + SSM_WIDTH
EVEN_MIX = LRU_WIDTH + SSM_WIDTH
D_FF = 3 * D_MODEL
GLA_HEADS = 4
GLA_DK = D_MODEL // 2 // GLA_HEADS
GLA_DV = D_MODEL // GLA_HEADS
GLA_KEY = GLA_HEADS * GLA_DK
GLA_VAL = GLA_HEADS * GLA_DV
GLA_RANK = 16
GLA_TAU = 16.0
GLA_CHUNK = 64
ODD_IN = 2 * GLA_KEY + 2 * GLA_VAL + GLA_RANK
N_EXPERTS = 8
TOP_K = 2
D_FF_EXPERT = 7 * D_MODEL // 2
N_EVEN = (DEPTH + 1) // 2
N_ODD = DEPTH // 2

kernel_name = "hybrid_rglru_s5_gla_moe_meta"

F32 = jnp.float32


def rms_norm(x, g):
    xf = x.astype(F32)
    y = xf * lax.rsqrt(jnp.mean(xf * xf, axis=-1, keepdims=True) + EPS)
    return (y * g.astype(F32)).astype(x.dtype)


def swiglu(h, w_gate, w_up, w_down):
    return (jax.nn.silu(h @ w_gate) * (h @ w_up)) @ w_down


def causal_conv(x, w, b):
    K = w.shape[0]
    L = x.shape[1]
    xp = jnp.pad(x, ((0, 0), (K - 1, 0), (0, 0)))
    return b + sum(w[k] * xp[:, k:k + L] for k in range(K))


def _real_combine(e1, e2):
    a1, b1 = e1
    a2, b2 = e2
    return a1 * a2, a2 * b1 + b2


def _complex_combine(e1, e2):
    ar1, ai1, br1, bi1 = e1
    ar2, ai2, br2, bi2 = e2
    return (ar1 * ar2 - ai1 * ai2,
            ar1 * ai2 + ai1 * ar2,
            ar2 * br1 - ai2 * bi1 + br2,
            ar2 * bi1 + ai2 * br1 + bi2)


def rg_lru(x, gate_a_w, gate_a_b, gate_x_w, gate_x_b, lam):
    B_, L, _ = x.shape
    xb = x.reshape(B_, L, LRU_BLOCKS, LRU_BLOCK)
    r = jax.nn.sigmoid(jnp.einsum('blhi,hij->blhj', xb, gate_a_w).reshape(B_, L, LRU_WIDTH) + gate_a_b)
    i = jax.nn.sigmoid(jnp.einsum('blhi,hij->blhj', xb, gate_x_w).reshape(B_, L, LRU_WIDTH) + gate_x_b)
    log_a = -LRU_C * r.astype(F32) * jax.nn.softplus(-lam.astype(F32))
    a = jnp.exp(log_a)
    b = jnp.sqrt(-jnp.expm1(2.0 * log_a)) * (i * x).astype(F32)
    _, h = lax.associative_scan(_real_combine, (a, b), axis=1)
    return h.astype(x.dtype)


def s5_ssm(u, lam_re, lam_im, log_dt, b_re, b_im, c_re, c_im, d, glu_w, glu_b):
    B_, L, _ = u.shape
    lr, li = lam_re.astype(F32), lam_im.astype(F32)
    dt = jnp.exp(log_dt.astype(F32))[:, None]
    mag = jnp.exp(dt * lr)
    ar, ai = mag * jnp.cos(dt * li), mag * jnp.sin(dt * li)
    den = lr * lr + li * li
    nr = ar - 1.0
    zr = (nr * lr + ai * li) / den
    zi = (ai * lr - nr * li) / den
    br_ = zr[..., None] * b_re.astype(F32) - zi[..., None] * b_im.astype(F32)
    bi_ = zr[..., None] * b_im.astype(F32) + zi[..., None] * b_re.astype(F32)
    ug = u.reshape(B_, L, SSM_GROUPS, SSM_GROUP).astype(F32)
    xr = jnp.einsum('blgh,gnh->blgn', ug, br_)
    xi = jnp.einsum('blgh,gnh->blgn', ug, bi_)
    a_shape = (1, L, SSM_GROUPS, SSM_STATE)
    ar_t = jnp.broadcast_to(ar[None, None], a_shape)
    ai_t = jnp.broadcast_to(ai[None, None], a_shape)
    _, _, hr, hi = lax.associative_scan(_complex_combine, (ar_t, ai_t, xr, xi), axis=1)
    y = (jnp.einsum('blgn,ghn->blgh', hr, c_re.astype(F32))
         - jnp.einsum('blgn,ghn->blgh', hi, c_im.astype(F32)))
    y = y.reshape(B_, L, SSM_WIDTH) + d.astype(F32) * u.astype(F32)
    g = jax.nn.gelu(y).astype(u.dtype)
    return g * jax.nn.sigmoid(g @ glu_w + glu_b)


def even_mixer(h, w_in, conv_w, conv_b, gate_a_w, gate_a_b, gate_x_w, gate_x_b, lru_lambda,
               lam_re, lam_im, log_dt, b_re, b_im, c_re, c_im, d, glu_w, glu_b, w_out):
    z = h @ w_in
    gate_branch = z[..., :LRU_WIDTH]
    rec_branch = z[..., LRU_WIDTH:2 * LRU_WIDTH]
    ssm_in = z[..., 2 * LRU_WIDTH:]
    rec = rg_lru(causal_conv(rec_branch, conv_w, conv_b), gate_a_w, gate_a_b, gate_x_w, gate_x_b, lru_lambda)
    a_out = jax.nn.gelu(gate_branch) * rec
    b_out = s5_ssm(ssm_in, lam_re, lam_im, log_dt, b_re, b_im, c_re, c_im, d, glu_w, glu_b)
    return (jnp.concatenate([a_out, b_out.astype(h.dtype)], axis=-1) @ w_out).astype(h.dtype)


def gla_chunk(S, q, k, v, lg):
    size = q.shape[2]
    b = jnp.cumsum(lg, axis=2)
    causal = jnp.tril(jnp.ones((size, size), dtype=bool))[:, :, None]
    diff = b[:, :, :, None, :] - b[:, :, None, :, :]
    decay = jnp.exp(jnp.where(causal, diff, -jnp.inf))
    att = jnp.einsum('bhtk,bhsk,bhtsk->bhts', q, k, decay)
    o = jnp.einsum('bhts,bhsv->bhtv', att, v) + jnp.einsum('bhtk,bhkv->bhtv', q * jnp.exp(b), S)
    b_last = b[:, :, -1:, :]
    S_new = (jnp.exp(b_last[:, :, 0, :])[..., None] * S
             + jnp.einsum('bhsk,bhsv->bhkv', k * jnp.exp(b_last - b), v))
    return S_new, o


def gla_mixer(h, w_in, gate_w2, gate_b, head_norm, w_out):
    B_, L, _ = h.shape
    z = h @ w_in
    q = z[..., :GLA_KEY] * (GLA_DK ** -0.5)
    k = z[..., GLA_KEY:2 * GLA_KEY]
    v = z[..., 2 * GLA_KEY:2 * GLA_KEY + GLA_VAL]
    g = z[..., 2 * GLA_KEY + GLA_VAL:2 * GLA_KEY + 2 * GLA_VAL]
    glr = z[..., 2 * GLA_KEY + 2 * GLA_VAL:]
    lg = jax.nn.log_sigmoid((glr @ gate_w2 + gate_b).astype(F32)) / GLA_TAU

    def heads(t, dh):
        return t.reshape(B_, L, GLA_HEADS, dh).transpose(0, 2, 1, 3).astype(F32)

    q, k, lg, v = heads(q, GLA_DK), heads(k, GLA_DK), heads(lg, GLA_DK), heads(v, GLA_DV)
    S0 = jnp.zeros((B_, GLA_HEADS, GLA_DK, GLA_DV), F32)
    S1, o_meta = gla_chunk(S0, q[:, :, :N_META], k[:, :, :N_META], v[:, :, :N_META], lg[:, :, :N_META])
    n_chunks = (L - N_META) // GLA_CHUNK

    def to_chunks(t):
        return t[:, :, N_META:].reshape(B_, GLA_HEADS, n_chunks, GLA_CHUNK, t.shape[-1]).transpose(2, 0, 1, 3, 4)

    _, o_real = lax.scan(lambda S, xs: gla_chunk(S, *xs), S1,
                         (to_chunks(q), to_chunks(k), to_chunks(v), to_chunks(lg)))
    o_real = o_real.transpose(1, 2, 0, 3, 4).reshape(B_, GLA_HEADS, L - N_META, GLA_DV)
    o = jnp.concatenate([o_meta, o_real], axis=2).transpose(0, 2, 1, 3)
    o = o * lax.rsqrt(jnp.mean(o * o, axis=-1, keepdims=True) + EPS)
    o = o.reshape(B_, L, GLA_VAL) * head_norm.astype(F32)
    o = o.astype(h.dtype) * jax.nn.silu(g)
    return o @ w_out


def moe_swiglu(h, router_w, w_gate, w_up, w_down):
    logits = (h @ router_w).astype(F32)
    top_v, top_i = lax.top_k(logits, TOP_K)
    top_w = jax.nn.softmax(top_v, axis=-1)
    gates = jnp.sum(jax.nn.one_hot(top_i, N_EXPERTS, dtype=F32) * top_w[..., None], axis=-2)
    out = jnp.zeros_like(h)
    for e in range(N_EXPERTS):
        out = out + gates[..., e:e + 1].astype(h.dtype) * swiglu(h, w_gate[e], w_up[e], w_down[e])
    return out


def _normal(key, shape, scale):
    return scale * jax.random.normal(key, shape, F32)


def setup_inputs(seed: int = 0) -> dict:
    key = jax.random.key(seed)
    ks = iter(jax.random.split(key, 48))
    D, E, O = D_MODEL, N_EVEN, N_ODD
    G, N, H = SSM_GROUPS, SSM_STATE, SSM_GROUP
    x = _normal(next(ks), (BATCH, SEQ, D), 1.0)
    meta_tokens = _normal(next(ks), (N_META, D), 1.0)
    ev_norm_mix = 1.0 + _normal(next(ks), (E, D), 0.02)
    ev_w_in = _normal(next(ks), (E, D, EVEN_IN), D ** -0.5)
    ev_conv_w = _normal(next(ks), (E, CONV_WIDTH, LRU_WIDTH), CONV_WIDTH ** -0.5)
    ev_conv_b = _normal(next(ks), (E, LRU_WIDTH), 0.01)
    ev_gate_a_w = _normal(next(ks), (E, LRU_BLOCKS, LRU_BLOCK, LRU_BLOCK), LRU_BLOCK ** -0.5)
    ev_gate_a_b = _normal(next(ks), (E, LRU_WIDTH), 0.01)
    ev_gate_x_w = _normal(next(ks), (E, LRU_BLOCKS, LRU_BLOCK, LRU_BLOCK), LRU_BLOCK ** -0.5)
    ev_gate_x_b = _normal(next(ks), (E, LRU_WIDTH), 0.01)
    a_c = jax.random.uniform(next(ks), (E, LRU_WIDTH), F32, minval=0.9, maxval=0.999)
    s = a_c ** (1.0 / LRU_C)
    ev_lru_lambda = jnp.log(s) - jnp.log1p(-s)
    ev_ssm_lambda_re = -0.5 + _normal(next(ks), (E, G, N), 0.01)
    ev_ssm_lambda_im = jnp.pi * jnp.arange(N, dtype=F32) + _normal(next(ks), (E, G, N), 0.01)
    ev_ssm_log_dt = jax.random.uniform(next(ks), (E, G), F32, minval=math.log(1e-3), maxval=math.log(1e-1))
    ev_ssm_b_re = _normal(next(ks), (E, G, N, H), (2 * H) ** -0.5)
    ev_ssm_b_im = _normal(next(ks), (E, G, N, H), (2 * H) ** -0.5)
    ev_ssm_c_re = _normal(next(ks), (E, G, H, N), N ** -0.5)
    ev_ssm_c_im = _normal(next(ks), (E, G, H, N), N ** -0.5)
    ev_ssm_d = _normal(next(ks), (E, SSM_WIDTH), 1.0)
    ev_ssm_glu_w = _normal(next(ks), (E, SSM_WIDTH, SSM_WIDTH), SSM_WIDTH ** -0.5)
    ev_ssm_glu_b = _normal(next(ks), (E, SSM_WIDTH), 0.01)
    ev_w_out = _normal(next(ks), (E, EVEN_MIX, D), EVEN_MIX ** -0.5)
    ev_norm_ffn = 1.0 + _normal(next(ks), (E, D), 0.02)
    ev_ffn_w_gate = _normal(next(ks), (E, D, D_FF), D ** -0.5)
    ev_ffn_w_up = _normal(next(ks), (E, D, D_FF), D ** -0.5)
    ev_ffn_w_down = _normal(next(ks), (E, D_FF, D), D_FF ** -0.5)
    od_norm_mix = 1.0 + _normal(next(ks), (O, D), 0.02)
    od_w_in = _normal(next(ks), (O, D, ODD_IN), D ** -0.5)
    od_gla_gate_w2 = _normal(next(ks), (O, GLA_RANK, GLA_KEY), GLA_RANK ** -0.5)
    od_gla_gate_b = _normal(next(ks), (O, GLA_KEY), 0.1)
    od_gla_norm = 1.0 + _normal(next(ks), (O, GLA_VAL), 0.02)
    od_w_out = _normal(next(ks), (O, GLA_VAL, D), GLA_VAL ** -0.5)
    od_norm_ffn = 1.0 + _normal(next(ks), (O, D), 0.02)
    od_router_w = _normal(next(ks), (O, D, N_EXPERTS), D ** -0.5)
    od_moe_w_gate = _normal(next(ks), (O, N_EXPERTS, D, D_FF_EXPERT), D ** -0.5)
    od_moe_w_up = _normal(next(ks), (O, N_EXPERTS, D, D_FF_EXPERT), D ** -0.5)
    od_moe_w_down = _normal(next(ks), (O, N_EXPERTS, D_FF_EXPERT, D), D_FF_EXPERT ** -0.5)
    final_norm = 1.0 + _normal(next(ks), (D,), 0.02)
    return {
        "x": x, "meta_tokens": meta_tokens,
        "ev_norm_mix": ev_norm_mix, "ev_w_in": ev_w_in, "ev_conv_w": ev_conv_w, "ev_conv_b": ev_conv_b,
        "ev_gate_a_w": ev_gate_a_w, "ev_gate_a_b": ev_gate_a_b, "ev_gate_x_w": ev_gate_x_w,
        "ev_gate_x_b": ev_gate_x_b, "ev_lru_lambda": ev_lru_lambda,
        "ev_ssm_lambda_re": ev_ssm_lambda_re, "ev_ssm_lambda_im": ev_ssm_lambda_im,
        "ev_ssm_log_dt": ev_ssm_log_dt, "ev_ssm_b_re": ev_ssm_b_re, "ev_ssm_b_im": ev_ssm_b_im,
        "ev_ssm_c_re": ev_ssm_c_re, "ev_ssm_c_im": ev_ssm_c_im, "ev_ssm_d": ev_ssm_d,
        "ev_ssm_glu_w": ev_ssm_glu_w, "ev_ssm_glu_b": ev_ssm_glu_b, "ev_w_out": ev_w_out,
        "ev_norm_ffn": ev_norm_ffn, "ev_ffn_w_gate": ev_ffn_w_gate, "ev_ffn_w_up": ev_ffn_w_up,
        "ev_ffn_w_down": ev_ffn_w_down,
        "od_norm_mix": od_norm_mix, "od_w_in": od_w_in, "od_gla_gate_w2": od_gla_gate_w2,
        "od_gla_gate_b": od_gla_gate_b, "od_gla_norm": od_gla_norm, "od_w_out": od_w_out,
        "od_norm_ffn": od_norm_ffn, "od_router_w": od_router_w, "od_moe_w_gate": od_moe_w_gate,
        "od_moe_w_up": od_moe_w_up, "od_moe_w_down": od_moe_w_down,
        "final_norm": final_norm,
    }


def reference(x, meta_tokens,
              ev_norm_mix, ev_w_in, ev_conv_w, ev_conv_b, ev_gate_a_w, ev_gate_a_b, ev_gate_x_w,
              ev_gate_x_b, ev_lru_lambda, ev_ssm_lambda_re, ev_ssm_lambda_im, ev_ssm_log_dt,
              ev_ssm_b_re, ev_ssm_b_im, ev_ssm_c_re, ev_ssm_c_im, ev_ssm_d, ev_ssm_glu_w, ev_ssm_glu_b,
              ev_w_out, ev_norm_ffn, ev_ffn_w_gate, ev_ffn_w_up, ev_ffn_w_down,
              od_norm_mix, od_w_in, od_gla_gate_w2, od_gla_gate_b, od_gla_norm, od_w_out,
              od_norm_ffn, od_router_w, od_moe_w_gate, od_moe_w_up, od_moe_w_down,
              final_norm):
    B_ = x.shape[0]
    meta = jnp.broadcast_to(meta_tokens.astype(x.dtype)[None], (B_, N_META, D_MODEL))
    h = jnp.concatenate([meta, x], axis=1)
    for layer in range(DEPTH):
        j = layer // 2
        if layer % 2 == 0:
            h = h + even_mixer(rms_norm(h, ev_norm_mix[j]), ev_w_in[j], ev_conv_w[j], ev_conv_b[j],
                               ev_gate_a_w[j], ev_gate_a_b[j], ev_gate_x_w[j], ev_gate_x_b[j],
                               ev_lru_lambda[j], ev_ssm_lambda_re[j], ev_ssm_lambda_im[j],
                               ev_ssm_log_dt[j], ev_ssm_b_re[j], ev_ssm_b_im[j], ev_ssm_c_re[j],
                               ev_ssm_c_im[j], ev_ssm_d[j], ev_ssm_glu_w[j], ev_ssm_glu_b[j], ev_w_out[j])
            h = h + swiglu(rms_norm(h, ev_norm_ffn[j]), ev_ffn_w_gate[j], ev_ffn_w_up[j], ev_ffn_w_down[j])
        else:
            h = h + gla_mixer(rms_norm(h, od_norm_mix[j]), od_w_in[j], od_gla_gate_w2[j],
                              od_gla_gate_b[j], od_gla_norm[j], od_w_out[j])
            h = h + moe_swiglu(rms_norm(h, od_norm_ffn[j]), od_router_w[j], od_moe_w_gate[j],
                               od_moe_w_up[j], od_moe_w_down[j])
    return rms_norm(h, final_norm)[:, N_META:]
```

```python
import functools
import math

import jax
import jax.numpy as jnp
from jax import lax
from jax.experimental import pallas as pl
from jax.experimental.pallas import tpu as pltpu

F32 = jnp.float32
BF16 = jnp.bfloat16

D_MODEL = 1024
N_META = 16
EPS = 1e-6
LRU_BLOCKS = 4
LRU_BLOCK = 256
LRU_C = 8.0
SSM_WIDTH = 512
SSM_GROUP = 16
SSM_GROUPS = 32
SSM_STATE = 64
SSM_LANES = SSM_GROUPS * SSM_STATE
D_FF = 3072
GLA_HEADS = 4
GLA_DK = 128
GLA_DV = 256
GLA_KEY = 512
GLA_VAL = 1024
GLA_RANK = 16
GLA_TAU = 16.0
N_EXPERTS = 8
D_FF_EXPERT = 3584

SUBLANES = 8
LANES = 128
VMEM_LIMIT = 52 * 1024 * 1024


def _cparams(sem):
    return pltpu.CompilerParams(dimension_semantics=sem, vmem_limit_bytes=VMEM_LIMIT)


def _dot(a, b):
    return jnp.dot(a, b, preferred_element_type=F32)


def _rms(x, gain):
    ms = jnp.mean(x * x, axis=-1, keepdims=True)
    return x * lax.rsqrt(ms + EPS) * gain


def _gelu_tanh(x):
    c = math.sqrt(2.0 / math.pi)
    return 0.5 * x * (1.0 + jnp.tanh(c * (x + 0.044715 * (x * x * x))))


def _silu(x):
    return x * jax.nn.sigmoid(x)


def _split_bf16(x):
    hi = x.astype(BF16)
    lo = (x - hi.astype(F32)).astype(BF16)
    return hi, lo


def _norm_proj_kernel(x_ref, g_ref, w_ref, *out_refs):
    xn = _rms(x_ref[...], g_ref[...]).astype(BF16)
    off = 0
    for o_ref in out_refs:
        n = o_ref.shape[-1]
        o_ref[...] = _dot(xn, w_ref[:, off:off + n]).astype(o_ref.dtype)
        off += n


def _norm_proj(x, gain, w, splits, tm):
    rows = x.shape[0]
    n = w.shape[1]
    assert sum(splits) == n and rows % tm == 0
    return pl.pallas_call(
        _norm_proj_kernel,
        grid=(rows // tm,),
        in_specs=[
            pl.BlockSpec((tm, D_MODEL), lambda i: (i, 0)),
            pl.BlockSpec((1, D_MODEL), lambda i: (0, 0)),
            pl.BlockSpec((D_MODEL, n), lambda i: (0, 0)),
        ],
        out_specs=[pl.BlockSpec((tm, s), lambda i: (i, 0)) for s in splits],
        out_shape=[jax.ShapeDtypeStruct((rows, s), BF16) for s in splits],
        compiler_params=_cparams(("parallel",)),
        name="norm_proj",
    )(x, gain, w)


def _gla_proj_kernel(x_ref, g_ref, w_ref, w2_ref, b2_ref, q_ref, k_ref, v_ref, og_ref, lg_ref):
    xn = _rms(x_ref[...], g_ref[...]).astype(BF16)
    q_ref[...] = _dot(xn, w_ref[:, 0:GLA_KEY]).astype(BF16)
    k_ref[...] = _dot(xn, w_ref[:, GLA_KEY:2 * GLA_KEY]).astype(BF16)
    v_ref[...] = _dot(xn, w_ref[:, 2 * GLA_KEY:2 * GLA_KEY + GLA_VAL]).astype(BF16)
    o0 = 2 * GLA_KEY + GLA_VAL
    og_ref[...] = _dot(xn, w_ref[:, o0:o0 + GLA_VAL]).astype(BF16)
    glr = _dot(xn, w_ref[:, o0 + GLA_VAL:o0 + GLA_VAL + LANES])
    glr_hi, glr_lo = _split_bf16(glr)
    w2_hi = w2_ref[0]
    w2_lo = w2_ref[1]
    pre = _dot(glr_hi, w2_hi) + _dot(glr_lo, w2_hi) + _dot(glr_hi, w2_lo) + b2_ref[...]
    lg_ref[...] = (jnp.minimum(pre, 0.0) - jnp.log1p(jnp.exp(-jnp.abs(pre)))) * (1.0 / GLA_TAU)


def _gla_proj(x, gain, w, w2, b2, tm):
    rows = x.shape[0]
    n = w.shape[1]
    row_spec = lambda c: pl.BlockSpec((tm, c), lambda i: (i, 0))
    return pl.pallas_call(
        _gla_proj_kernel,
        grid=(rows // tm,),
        in_specs=[
            row_spec(D_MODEL),
            pl.BlockSpec((1, D_MODEL), lambda i: (0, 0)),
            pl.BlockSpec((D_MODEL, n), lambda i: (0, 0)),
            pl.BlockSpec((2, LANES, GLA_KEY), lambda i: (0, 0, 0)),
            pl.BlockSpec((1, GLA_KEY), lambda i: (0, 0)),
        ],
        out_specs=[row_spec(GLA_KEY), row_spec(GLA_KEY), row_spec(GLA_VAL), row_spec(GLA_VAL),
                   row_spec(GLA_KEY)],
        out_shape=[jax.ShapeDtypeStruct((rows, GLA_KEY), BF16),
                   jax.ShapeDtypeStruct((rows, GLA_KEY), BF16),
                   jax.ShapeDtypeStruct((rows, GLA_VAL), BF16),
                   jax.ShapeDtypeStruct((rows, GLA_VAL), BF16),
                   jax.ShapeDtypeStruct((rows, GLA_KEY), F32)],
        compiler_params=_cparams(("parallel",)),
        name="gla_proj",
    )(x, gain, w, w2, b2)


def _lru_kernel(gate_ref, rec_ref, cw_ref, cb_ref, wa_ref, ba_ref, wx_ref, bx_ref, sp_ref,
                conv0_ref, h0_ref, out_ref, conv_t_ref, h_t_ref, xbuf, abuf, bbuf, hcar, *, tl):
    @pl.when(pl.program_id(1) == 0)
    def _():
        xbuf[0:SUBLANES, :] = conv0_ref[...]
        hcar[...] = h0_ref[...]

    xbuf[SUBLANES:SUBLANES + tl, :] = rec_ref[...].astype(F32)
    xc = cb_ref[...]
    for tap in range(4):
        xc = xc + cw_ref[tap:tap + 1, :] * xbuf[SUBLANES - 3 + tap:SUBLANES - 3 + tap + tl, :]
    xbuf[0:SUBLANES, :] = xbuf[tl:tl + SUBLANES, :]

    for blk in range(LRU_BLOCKS):
        sl = slice(blk * LRU_BLOCK, (blk + 1) * LRU_BLOCK)
        xs = xc[:, sl]
        xb = xs.astype(BF16)
        r = jax.nn.sigmoid(_dot(xb, wa_ref[blk]) + ba_ref[:, sl])
        i = jax.nn.sigmoid(_dot(xb, wx_ref[blk]) + bx_ref[:, sl])
        a = jnp.exp(-LRU_C * r * sp_ref[:, sl])
        abuf[:, sl] = a
        bbuf[:, sl] = jnp.sqrt(1.0 - a * a) * (i * xs)

    row = lax.broadcasted_iota(jnp.int32, (SUBLANES, D_MODEL), 0)

    def body(gidx, carry):
        r0 = pl.multiple_of(gidx * SUBLANES, SUBLANES)
        a = abuf[pl.ds(r0, SUBLANES), :]
        b = bbuf[pl.ds(r0, SUBLANES), :]
        for s in (1, 2, 4):
            m = row >= s
            a_sh = pltpu.roll(a, s, 0)
            b_sh = pltpu.roll(b, s, 0)
            b = jnp.where(m, a * b_sh + b, b)
            a = jnp.where(m, a * a_sh, a)
        h = a * carry + b
        bbuf[pl.ds(r0, SUBLANES), :] = h
        return jnp.broadcast_to(h[SUBLANES - 1:SUBLANES, :], (SUBLANES, D_MODEL))

    carry = lax.fori_loop(0, tl // SUBLANES, body, hcar[...])
    hcar[...] = carry
    out_ref[...] = (_gelu_tanh(gate_ref[...].astype(F32)) * bbuf[...]).astype(out_ref.dtype)
    conv_t_ref[...] = xbuf[0:SUBLANES, :]
    h_t_ref[...] = carry


def _lru(gate, rec, p, conv0, h0, nb, tl):
    rows = gate.shape[0]
    nt = rows // (nb * tl)
    tile = pl.BlockSpec((tl, D_MODEL), lambda b, t: (b * nt + t, 0))
    full2 = lambda shp: pl.BlockSpec(shp, lambda b, t: (0, 0))
    full3 = lambda shp: pl.BlockSpec(shp, lambda b, t: (0, 0, 0))
    return pl.pallas_call(
        functools.partial(_lru_kernel, tl=tl),
        grid=(nb, nt),
        in_specs=[tile, tile,
                  full2((4, D_MODEL)), full2((1, D_MODEL)),
                  full3((LRU_BLOCKS, LRU_BLOCK, LRU_BLOCK)), full2((1, D_MODEL)),
                  full3((LRU_BLOCKS, LRU_BLOCK, LRU_BLOCK)), full2((1, D_MODEL)),
                  full2((1, D_MODEL)),
                  full2((SUBLANES, D_MODEL)), full2((SUBLANES, D_MODEL))],
        out_specs=[tile, full2((SUBLANES, D_MODEL)), full2((SUBLANES, D_MODEL))],
        out_shape=[jax.ShapeDtypeStruct((rows, D_MODEL), BF16),
                   jax.ShapeDtypeStruct((SUBLANES, D_MODEL), F32),
                   jax.ShapeDtypeStruct((SUBLANES, D_MODEL), F32)],
        scratch_shapes=[pltpu.VMEM((tl + SUBLANES, D_MODEL), F32),
                        pltpu.VMEM((tl, D_MODEL), F32),
                        pltpu.VMEM((tl, D_MODEL), F32),
                        pltpu.VMEM((SUBLANES, D_MODEL), F32)],
        compiler_params=_cparams(("arbitrary", "arbitrary")),
        name="rg_lru",
    )(gate, rec, p["conv_w"], p["conv_b"], p["gate_a_w"], p["gate_a_b"], p["gate_x_w"],
      p["gate_x_b"], p["softplus"], conv0, h0)


SSM_CHUNK = 256


def _ssm_kernel(u_ref, bblk_ref, cblk_ref, d_ref, gw_ref, gb_ref, pw_ref, h0_ref,
                out_ref, h_t_ref, xs, hcar, *, tl):
    @pl.when(pl.program_id(1) == 0)
    def _():
        hcar[...] = h0_ref[...]

    u = u_ref[...]
    xs[...] = _dot(u, bblk_ref[...])

    for c in range(SSM_LANES // SSM_CHUNK):
        re = slice(c * SSM_CHUNK, (c + 1) * SSM_CHUNK)
        im = slice(SSM_LANES + c * SSM_CHUNK, SSM_LANES + (c + 1) * SSM_CHUNK)
        tabs = [(pw_ref[k * SUBLANES:(k + 1) * SUBLANES, re], pw_ref[k * SUBLANES:(k + 1) * SUBLANES, im])
                for k in range(4)]

        def body(gidx, carry, re=re, im=im, tabs=tabs):
            cr, ci = carry
            r0 = pl.multiple_of(gidx * SUBLANES, SUBLANES)
            hr = xs[pl.ds(r0, SUBLANES), re]
            hi = xs[pl.ds(r0, SUBLANES), im]
            for k, s in enumerate((1, 2, 4)):
                pr, pi = tabs[k]
                sr = pltpu.roll(hr, s, 0)
                si = pltpu.roll(hi, s, 0)
                hr, hi = hr + (pr * sr - pi * si), hi + (pr * si + pi * sr)
            pr, pi = tabs[3]
            hr, hi = hr + (pr * cr - pi * ci), hi + (pr * ci + pi * cr)
            xs[pl.ds(r0, SUBLANES), re] = hr
            xs[pl.ds(r0, SUBLANES), im] = hi
            last = slice(SUBLANES - 1, SUBLANES)
            return (jnp.broadcast_to(hr[last, :], (SUBLANES, SSM_CHUNK)),
                    jnp.broadcast_to(hi[last, :], (SUBLANES, SSM_CHUNK)))

        cr, ci = lax.fori_loop(0, tl // SUBLANES, body, (hcar[:, re], hcar[:, im]))
        hcar[:, re] = cr
        hcar[:, im] = ci

    y = _dot(xs[...].astype(BF16), cblk_ref[...]) + d_ref[...] * u.astype(F32)
    g = _gelu_tanh(y)
    out_ref[...] = (g * jax.nn.sigmoid(_dot(g.astype(BF16), gw_ref[...]) + gb_ref[...])).astype(out_ref.dtype)
    h_t_ref[...] = hcar[...]


def _ssm(u, p, h0, nb, tl):
    rows = u.shape[0]
    nt = rows // (nb * tl)
    tile = pl.BlockSpec((tl, SSM_WIDTH), lambda b, t: (b * nt + t, 0))
    full2 = lambda shp: pl.BlockSpec(shp, lambda b, t: (0, 0))
    return pl.pallas_call(
        functools.partial(_ssm_kernel, tl=tl),
        grid=(nb, nt),
        in_specs=[tile,
                  full2((SSM_WIDTH, 2 * SSM_LANES)), full2((2 * SSM_LANES, SSM_WIDTH)),
                  full2((1, SSM_WIDTH)), full2((SSM_WIDTH, SSM_WIDTH)), full2((1, SSM_WIDTH)),
                  full2((4 * SUBLANES, 2 * SSM_LANES)), full2((SUBLANES, 2 * SSM_LANES))],
        out_specs=[tile, full2((SUBLANES, 2 * SSM_LANES))],
        out_shape=[jax.ShapeDtypeStruct((rows, SSM_WIDTH), BF16),
                   jax.ShapeDtypeStruct((SUBLANES, 2 * SSM_LANES), F32)],
        scratch_shapes=[pltpu.VMEM((tl, 2 * SSM_LANES), F32),
                        pltpu.VMEM((SUBLANES, 2 * SSM_LANES), F32)],
        compiler_params=_cparams(("arbitrary", "arbitrary")),
        name="s5_ssm",
    )(u, p["bblk"], p["cblk"], p["d"], p["glu_w"], p["glu_b"], p["powers"], h0)


def _out_proj_kernel(*refs, n_in):
    h_ref = refs[2 * n_in]
    o_ref = refs[2 * n_in + 1]
    acc = h_ref[...]
    for j in range(n_in):
        acc = acc + _dot(refs[j][...], refs[n_in + j][...])
    o_ref[...] = acc


def _out_proj(xs, ws, h, tm):
    rows = h.shape[0]
    n_in = len(xs)
    in_specs = [pl.BlockSpec((tm, x.shape[1]), lambda i: (i, 0)) for x in xs]
    in_specs += [pl.BlockSpec(w.shape, lambda i: (0, 0)) for w in ws]
    in_specs += [pl.BlockSpec((tm, D_MODEL), lambda i: (i, 0))]
    return pl.pallas_call(
        functools.partial(_out_proj_kernel, n_in=n_in),
        grid=(rows // tm,),
        in_specs=in_specs,
        out_specs=pl.BlockSpec((tm, D_MODEL), lambda i: (i, 0)),
        out_shape=jax.ShapeDtypeStruct((rows, D_MODEL), F32),
        compiler_params=_cparams(("parallel",)),
        name="out_proj",
    )(*xs, *ws, h)


def _ffn_kernel(h_ref, g_ref, wg_ref, wu_ref, wd_ref, o_ref, xn_ref, acc_ref):
    f = pl.program_id(1)

    @pl.when(f == 0)
    def _():
        xn_ref[...] = _rms(h_ref[...], g_ref[...]).astype(BF16)
        acc_ref[...] = h_ref[...]

    xn = xn_ref[...]
    a = (_silu(_dot(xn, wg_ref[...])) * _dot(xn, wu_ref[...])).astype(BF16)
    acc_ref[...] += _dot(a, wd_ref[...])

    @pl.when(f == pl.num_programs(1) - 1)
    def _():
        o_ref[...] = acc_ref[...]


def _ffn(h, gain, wg, wu, wd, tm, fc):
    rows = h.shape[0]
    dff = wg.shape[1]
    return pl.pallas_call(
        _ffn_kernel,
        grid=(rows // tm, dff // fc),
        in_specs=[pl.BlockSpec((tm, D_MODEL), lambda i, f: (i, 0)),
                  pl.BlockSpec((1, D_MODEL), lambda i, f: (0, 0)),
                  pl.BlockSpec((D_MODEL, fc), lambda i, f: (0, f)),
                  pl.BlockSpec((D_MODEL, fc), lambda i, f: (0, f)),
                  pl.BlockSpec((fc, D_MODEL), lambda i, f: (f, 0))],
        out_specs=pl.BlockSpec((tm, D_MODEL), lambda i, f: (i, 0)),
        out_shape=jax.ShapeDtypeStruct((rows, D_MODEL), F32),
        scratch_shapes=[pltpu.VMEM((tm, D_MODEL), BF16), pltpu.VMEM((tm, D_MODEL), F32)],
        compiler_params=_cparams(("parallel", "arbitrary")),
        name="dense_ffn",
    )(h, gain, wg, wu, wd)


def _gla_kernel(q_ref, k_ref, v_ref, og_ref, lg_ref, hn_ref, s0_ref, o_ref, s_t_ref, st, *, tl, ck):
    @pl.when(pl.program_id(1) == 0)
    def _():
        st[...] = s0_ref[...]

    ri = lax.broadcasted_iota(jnp.int32, (ck, ck), 0)
    ci = lax.broadcasted_iota(jnp.int32, (ck, ck), 1)
    causal = ri >= ci
    tri = causal.astype(F32).astype(BF16)
    scale = GLA_DK ** -0.5

    for c in range(tl // ck):
        rows = slice(c * ck, (c + 1) * ck)
        lg_hi, lg_lo = _split_bf16(lg_ref[rows, :])
        b_all = _dot(tri, lg_hi) + _dot(tri, lg_lo)
        for h in range(GLA_HEADS):
            ks = slice(h * GLA_DK, (h + 1) * GLA_DK)
            vs = slice(h * GLA_DV, (h + 1) * GLA_DV)
            b = b_all[:, ks]
            b_last = b[ck - 1:ck, :]
            q = q_ref[rows, ks].astype(F32)
            k = k_ref[rows, ks].astype(F32)
            v = v_ref[rows, vs]
            qd = (q * (scale * jnp.exp(b))).astype(BF16)
            kd = (k * jnp.exp(-b)).astype(BF16)
            kl = (k * jnp.exp(b_last - b)).astype(BF16)
            att = lax.dot_general(qd, kd, (((1,), (1,)), ((), ())), preferred_element_type=F32)
            att = jnp.where(causal, att, 0.0).astype(BF16)
            s_h = st[h]
            o = _dot(att, v) + lax.dot_general(qd, s_h.astype(BF16), (((1,), (1,)), ((), ())),
                                               preferred_element_type=F32)
            st[h] = s_h * jnp.exp(b_last) + lax.dot_general(v, kl, (((0,), (0,)), ((), ())),
                                                            preferred_element_type=F32)
            o = o * lax.rsqrt(jnp.mean(o * o, axis=-1, keepdims=True) + EPS) * hn_ref[:, vs]
            o_ref[rows, vs] = (o * _silu(og_ref[rows, vs].astype(F32))).astype(o_ref.dtype)
    s_t_ref[...] = st[...]


def _gla(q, k, v, og, lg, head_norm, s0, nb, tl, ck):
    rows = q.shape[0]
    nt = rows // (nb * tl)
    tile = lambda c: pl.BlockSpec((tl, c), lambda b, t: (b * nt + t, 0))
    state = pl.BlockSpec((GLA_HEADS, GLA_DV, GLA_DK), lambda b, t: (0, 0, 0))
    return pl.pallas_call(
        functools.partial(_gla_kernel, tl=tl, ck=ck),
        grid=(nb, nt),
        in_specs=[tile(GLA_KEY), tile(GLA_KEY), tile(GLA_VAL), tile(GLA_VAL), tile(GLA_KEY),
                  pl.BlockSpec((1, GLA_VAL), lambda b, t: (0, 0)), state],
        out_specs=[tile(GLA_VAL), state],
        out_shape=[jax.ShapeDtypeStruct((rows, GLA_VAL), BF16),
                   jax.ShapeDtypeStruct((GLA_HEADS, GLA_DV, GLA_DK), F32)],
        scratch_shapes=[pltpu.VMEM((GLA_HEADS, GLA_DV, GLA_DK), F32)],
        compiler_params=_cparams(("arbitrary", "arbitrary")),
        name="gla",
    )(q, k, v, og, lg, head_norm, s0)


def _router_kernel(h_ref, g_ref, wr_ref, xn_ref, gates_ref):
    xn = _rms(h_ref[...], g_ref[...])
    xn_ref[...] = xn.astype(BF16)
    x_hi, x_lo = _split_bf16(xn)
    w_hi = wr_ref[0]
    w_lo = wr_ref[1]
    logits = _dot(x_hi, w_hi) + _dot(x_lo, w_hi) + _dot(x_hi, w_lo)
    lane = lax.broadcasted_iota(jnp.int32, logits.shape, 1).astype(F32)
    neg = jnp.float32(-jnp.inf)
    logits = jnp.where(lane < N_EXPERTS, logits, neg)
    v1 = jnp.max(logits, axis=-1, keepdims=True)
    i1 = jnp.min(jnp.where(logits == v1, lane, float(LANES)), axis=-1, keepdims=True)
    rest = jnp.where(lane == i1, neg, logits)
    v2 = jnp.max(rest, axis=-1, keepdims=True)
    i2 = jnp.min(jnp.where(rest == v2, lane, float(LANES)), axis=-1, keepdims=True)
    e2 = jnp.exp(v2 - v1)
    w1 = 1.0 / (1.0 + e2)
    w2 = e2 / (1.0 + e2)
    gates_ref[...] = jnp.where(lane == i1, w1, 0.0) + jnp.where(lane == i2, w2, 0.0)


def _router(h, gain, wr, tm):
    rows = h.shape[0]
    return pl.pallas_call(
        _router_kernel,
        grid=(rows // tm,),
        in_specs=[pl.BlockSpec((tm, D_MODEL), lambda i: (i, 0)),
                  pl.BlockSpec((1, D_MODEL), lambda i: (0, 0)),
                  pl.BlockSpec((2, D_MODEL, LANES), lambda i: (0, 0, 0))],
        out_specs=[pl.BlockSpec((tm, D_MODEL), lambda i: (i, 0)),
                   pl.BlockSpec((tm, LANES), lambda i: (i, 0))],
        out_shape=[jax.ShapeDtypeStruct((rows, D_MODEL), BF16),
                   jax.ShapeDtypeStruct((rows, LANES), F32)],
        compiler_params=_cparams(("parallel",)),
        name="router",
    )(h, gain, wr)


def _moe_kernel(xn_ref, gates_ref, wg_ref, wu_ref, wd_ref, h_ref, fn_ref, o_ref, acc_ref):
    e = pl.program_id(1)
    f = pl.program_id(2)

    @pl.when((e == 0) & (f == 0))
    def _():
        acc_ref[...] = h_ref[...]

    xn = xn_ref[...]
    gates = gates_ref[...]
    lane = lax.broadcasted_iota(jnp.int32, gates.shape, 1)
    ge = jnp.sum(jnp.where(lane == e, gates, 0.0), axis=-1, keepdims=True)
    a = _silu(_dot(xn, wg_ref[0])) * _dot(xn, wu_ref[0])
    acc_ref[...] += _dot((a * ge).astype(BF16), wd_ref[0])

    @pl.when((e == pl.num_programs(1) - 1) & (f == pl.num_programs(2) - 1))
    def _():
        o_ref[...] = _rms(acc_ref[...], fn_ref[...])


def _moe(xn, gates, wg, wu, wd, h, final_norm, tm, fc):
    rows = xn.shape[0]
    ne, _, dff = wg.shape
    return pl.pallas_call(
        _moe_kernel,
        grid=(rows // tm, ne, dff // fc),
        in_specs=[pl.BlockSpec((tm, D_MODEL), lambda i, e, f: (i, 0)),
                  pl.BlockSpec((tm, LANES), lambda i, e, f: (i, 0)),
                  pl.BlockSpec((1, D_MODEL, fc), lambda i, e, f: (e, 0, f)),
                  pl.BlockSpec((1, D_MODEL, fc), lambda i, e, f: (e, 0, f)),
                  pl.BlockSpec((1, fc, D_MODEL), lambda i, e, f: (e, f, 0)),
                  pl.BlockSpec((tm, D_MODEL), lambda i, e, f: (i, 0)),
                  pl.BlockSpec((1, D_MODEL), lambda i, e, f: (0, 0))],
        out_specs=pl.BlockSpec((tm, D_MODEL), lambda i, e, f: (i, 0)),
        out_shape=jax.ShapeDtypeStruct((rows, D_MODEL), F32),
        scratch_shapes=[pltpu.VMEM((tm, D_MODEL), F32)],
        compiler_params=_cparams(("parallel", "arbitrary", "arbitrary")),
        name="moe_ffn",
    )(xn, gates, wg, wu, wd, h, final_norm)


def _ssm_params(lam_re, lam_im, log_dt, b_re, b_im, c_re, c_im, d, glu_w, glu_b):
    g, n, hh = SSM_GROUPS, SSM_STATE, SSM_GROUP
    lr, li = lam_re.astype(F32), lam_im.astype(F32)
    dt = jnp.exp(log_dt.astype(F32))[:, None]
    mag = jnp.exp(dt * lr)
    ar, ai = mag * jnp.cos(dt * li), mag * jnp.sin(dt * li)
    den = lr * lr + li * li
    nr = ar - 1.0
    zr = (nr * lr + ai * li) / den
    zi = (ai * lr - nr * li) / den
    br = zr[..., None] * b_re - zi[..., None] * b_im
    bi = zr[..., None] * b_im + zi[..., None] * b_re
    eye = jnp.eye(g, dtype=F32)
    bblk = jnp.concatenate(
        [jnp.einsum("gnh,gk->ghkn", x, eye).reshape(SSM_WIDTH, SSM_LANES) for x in (br, bi)], axis=1)
    cblk = jnp.concatenate(
        [jnp.einsum("ghn,gk->gnkh", x, eye).reshape(SSM_LANES, SSM_WIDTH) for x in (c_re, -c_im)], axis=0)

    def power(m):
        pr, pi = jnp.ones((SUBLANES, g, n), F32), jnp.zeros((SUBLANES, g, n), F32)
        for step in range(1, SUBLANES + 1):
            nr_, ni_ = pr * ar - pi * ai, pr * ai + pi * ar
            take = (m >= step)[:, :, None]
            pr, pi = jnp.where(take, nr_, pr), jnp.where(take, ni_, pi)
        return pr.reshape(SUBLANES, SSM_LANES), pi.reshape(SUBLANES, SSM_LANES)

    sub = jnp.arange(SUBLANES, dtype=jnp.int32)[:, None]
    tabs = []
    for s in (1, 2, 4):
        pr, pi = power(jnp.full((SUBLANES, 1), s, jnp.int32))
        keep = (sub >= s).astype(F32)
        tabs.append(jnp.concatenate([pr * keep, pi * keep], axis=1))
    pr, pi = power(sub + 1)
    tabs.append(jnp.concatenate([pr, pi], axis=1))
    return {
        "bblk": bblk.astype(BF16), "cblk": cblk.astype(BF16),
        "d": d.reshape(1, SSM_WIDTH).astype(F32),
        "glu_w": glu_w.astype(BF16), "glu_b": glu_b.reshape(1, SSM_WIDTH).astype(F32),
        "powers": jnp.concatenate(tabs, axis=0),
    }


def _row(x):
    return x.reshape(1, -1).astype(F32)


def _pad_cols(w, n):
    return jnp.pad(w, ((0, 0), (0, n - w.shape[1])))


def kernel(x, meta_tokens, ev_norm_mix, ev_w_in, ev_conv_w, ev_conv_b, ev_gate_a_w, ev_gate_a_b, ev_gate_x_w, ev_gate_x_b, ev_lru_lambda, ev_ssm_lambda_re, ev_ssm_lambda_im, ev_ssm_log_dt, ev_ssm_b_re, ev_ssm_b_im, ev_ssm_c_re, ev_ssm_c_im, ev_ssm_d, ev_ssm_glu_w, ev_ssm_glu_b, ev_w_out, ev_norm_ffn, ev_ffn_w_gate, ev_ffn_w_up, ev_ffn_w_down, od_norm_mix, od_w_in, od_gla_gate_w2, od_gla_gate_b, od_gla_norm, od_w_out, od_norm_ffn, od_router_w, od_moe_w_gate, od_moe_w_up, od_moe_w_down, final_norm):
    nb, seq, _ = x.shape
    rows = nb * seq
    assert ev_w_in.shape[0] == 1 and od_w_in.shape[0] == 1, "two-layer trunk only"

    w_in0 = ev_w_in[0].astype(BF16)
    lru_p = {
        "conv_w": ev_conv_w[0].astype(F32), "conv_b": _row(ev_conv_b[0]),
        "gate_a_w": ev_gate_a_w[0].astype(BF16), "gate_a_b": _row(ev_gate_a_b[0]),
        "gate_x_w": ev_gate_x_w[0].astype(BF16), "gate_x_b": _row(ev_gate_x_b[0]),
        "softplus": _row(jax.nn.softplus(-ev_lru_lambda[0].astype(F32))),
    }
    ssm_p = _ssm_params(ev_ssm_lambda_re[0], ev_ssm_lambda_im[0], ev_ssm_log_dt[0], ev_ssm_b_re[0],
                        ev_ssm_b_im[0], ev_ssm_c_re[0], ev_ssm_c_im[0], ev_ssm_d[0],
                        ev_ssm_glu_w[0], ev_ssm_glu_b[0])
    w_out0 = ev_w_out[0].astype(BF16)
    w_out0_a, w_out0_b = w_out0[:D_MODEL], w_out0[D_MODEL:]
    ffn_g, ffn_u, ffn_d = (w[0].astype(BF16) for w in (ev_ffn_w_gate, ev_ffn_w_up, ev_ffn_w_down))
    odd_in = 2 * GLA_KEY + 2 * GLA_VAL
    w_in1 = _pad_cols(od_w_in[0], odd_in + LANES).astype(BF16)
    w2 = jnp.pad(od_gla_gate_w2[0].astype(F32), ((0, LANES - GLA_RANK), (0, 0)))
    w2_hi = w2.astype(BF16)
    w2_split = jnp.stack([w2_hi, (w2 - w2_hi.astype(F32)).astype(BF16)])
    wr = _pad_cols(od_router_w[0].astype(F32), LANES)
    wr_hi = wr.astype(BF16)
    wr_split = jnp.stack([wr_hi, (wr - wr_hi.astype(F32)).astype(BF16)])
    w_out1 = od_w_out[0].astype(BF16)
    moe_g, moe_u, moe_d = (w[0].astype(BF16) for w in (od_moe_w_gate, od_moe_w_up, od_moe_w_down))

    tl = min(256, seq)
    tm = min(512, rows)
    ck = min(128, tl)
    even_splits = (D_MODEL, D_MODEL, SSM_WIDTH)

    def even_mixer(h, nbatch, tile_m, tile_l, conv0, h0, s0):
        gate, rec, u = _norm_proj(h, _row(ev_norm_mix[0]), w_in0, even_splits, tile_m)
        a_out, conv_t, h_t = _lru(gate, rec, lru_p, conv0, h0, nbatch, tile_l)
        b_out, s_t = _ssm(u, ssm_p, s0, nbatch, tile_l)
        h = _out_proj([a_out, b_out], [w_out0_a, w_out0_b], h, tile_m)
        h = _ffn(h, _row(ev_norm_ffn[0]), ffn_g, ffn_u, ffn_d, tile_m, 512)
        return h, (conv_t, h_t, s_t)

    def gla_inputs(h, tile_m):
        return _gla_proj(h, _row(od_norm_mix[0]), w_in1, w2_split, _row(od_gla_gate_b[0]), tile_m)

    zeros = lambda *s: jnp.zeros(s, F32)
    hm = meta_tokens.astype(F32)
    hm, (conv_m, h_m, s_m) = even_mixer(hm, 1, N_META, N_META, zeros(SUBLANES, D_MODEL),
                                        zeros(SUBLANES, D_MODEL), zeros(SUBLANES, 2 * SSM_LANES))
    qm, km, vm, ogm, lgm = gla_inputs(hm, N_META)
    _, gla_s = _gla(qm, km, vm, ogm, lgm, _row(od_gla_norm[0]),
                    zeros(GLA_HEADS, GLA_DV, GLA_DK), 1, N_META, N_META)

    h = x.reshape(rows, D_MODEL).astype(F32)
    h, _ = even_mixer(h, nb, tm, tl, conv_m, h_m, s_m)
    q, k, v, og, lg = gla_inputs(h, tm)
    o, _ = _gla(q, k, v, og, lg, _row(od_gla_norm[0]), gla_s, nb, tl, ck)
    h = _out_proj([o], [w_out1], h, tm)
    xn, gates = _router(h, _row(od_norm_ffn[0]), wr_split, tm)
    out = _moe(xn, gates, moe_g, moe_u, moe_d, h, _row(final_norm), min(1024, rows), 512)
    return out.reshape(nb, seq, D_MODEL)
```

```python
import functools
import math

import jax
import jax.numpy as jnp
from jax import lax
from jax.experimental import pallas as pl
from jax.experimental.pallas import tpu as pltpu

F32 = jnp.float32
BF16 = jnp.bfloat16

D_MODEL = 1024
N_META = 16
EPS = 1e-6
LRU_BLOCKS = 4
LRU_BLOCK = 256
LRU_C = 8.0
SSM_WIDTH = 512
SSM_GROUP = 16
SSM_GROUPS = 32
SSM_STATE = 64
SSM_LANES = SSM_GROUPS * SSM_STATE
D_FF = 3072
GLA_HEADS = 4
GLA_DK = 128
GLA_DV = 256
GLA_KEY = 512
GLA_VAL = 1024
GLA_RANK = 16
GLA_TAU = 16.0
N_EXPERTS = 8
D_FF_EXPERT = 3584

SUBLANES = 8
LANES = 128
VMEM_LIMIT = 52 * 1024 * 1024


def _cparams(sem):
    return pltpu.CompilerParams(dimension_semantics=sem, vmem_limit_bytes=VMEM_LIMIT)


def _dot(a, b):
    return jnp.dot(a, b, preferred_element_type=F32)


def _rms(x, gain):
    ms = jnp.mean(x * x, axis=-1, keepdims=True)
    return x * lax.rsqrt(ms + EPS) * gain


def _gelu_tanh(x):
    c = math.sqrt(2.0 / math.pi)
    return 0.5 * x * (1.0 + jnp.tanh(c * (x + 0.044715 * (x * x * x))))


def _silu(x):
    return x * jax.nn.sigmoid(x)


def _split_bf16(x):
    hi = x.astype(BF16)
    lo = (x - hi.astype(F32)).astype(BF16)
    return hi, lo


def _norm_proj_kernel(x_ref, g_ref, w_ref, *out_refs):
    xn = _rms(x_ref[...], g_ref[...]).astype(BF16)
    off = 0
    for o_ref in out_refs:
        n = o_ref.shape[-1]
        o_ref[...] = _dot(xn, w_ref[:, off:off + n]).astype(o_ref.dtype)
        off += n


def _norm_proj(x, gain, w, splits, tm):
    rows = x.shape[0]
    n = w.shape[1]
    assert sum(splits) == n and rows % tm == 0
    return pl.pallas_call(
        _norm_proj_kernel,
        grid=(rows // tm,),
        in_specs=[
            pl.BlockSpec((tm, D_MODEL), lambda i: (i, 0)),
            pl.BlockSpec((1, D_MODEL), lambda i: (0, 0)),
            pl.BlockSpec((D_MODEL, n), lambda i: (0, 0)),
        ],
        out_specs=[pl.BlockSpec((tm, s), lambda i: (i, 0)) for s in splits],
        out_shape=[jax.ShapeDtypeStruct((rows, s), BF16) for s in splits],
        compiler_params=_cparams(("parallel",)),
        name="norm_proj",
    )(x, gain, w)


def _gla_proj_kernel(x_ref, g_ref, w_ref, w2_ref, b2_ref, q_ref, k_ref, v_ref, og_ref, lg_ref):
    xn = _rms(x_ref[...], g_ref[...]).astype(BF16)
    q_ref[...] = _dot(xn, w_ref[:, 0:GLA_KEY]).astype(BF16)
    k_ref[...] = _dot(xn, w_ref[:, GLA_KEY:2 * GLA_KEY]).astype(BF16)
    v_ref[...] = _dot(xn, w_ref[:, 2 * GLA_KEY:2 * GLA_KEY + GLA_VAL]).astype(BF16)
    o0 = 2 * GLA_KEY + GLA_VAL
    og_ref[...] = _dot(xn, w_ref[:, o0:o0 + GLA_VAL]).astype(BF16)
    glr = _dot(xn, w_ref[:, o0 + GLA_VAL:o0 + GLA_VAL + LANES])
    glr_hi, glr_lo = _split_bf16(glr)
    w2_hi = w2_ref[0]
    w2_lo = w2_ref[1]
    pre = _dot(glr_hi, w2_hi) + _dot(glr_lo, w2_hi) + _dot(glr_hi, w2_lo) + b2_ref[...]
    lg_ref[...] = (jnp.minimum(pre, 0.0) - jnp.log1p(jnp.exp(-jnp.abs(pre)))) * (1.0 / GLA_TAU)


def _gla_proj(x, gain, w, w2, b2, tm):
    rows = x.shape[0]
    n = w.shape[1]
    row_spec = lambda c: pl.BlockSpec((tm, c), lambda i: (i, 0))
    return pl.pallas_call(
        _gla_proj_kernel,
        grid=(rows // tm,),
        in_specs=[
            row_spec(D_MODEL),
            pl.BlockSpec((1, D_MODEL), lambda i: (0, 0)),
            pl.BlockSpec((D_MODEL, n), lambda i: (0, 0)),
            pl.BlockSpec((2, LANES, GLA_KEY), lambda i: (0, 0, 0)),
            pl.BlockSpec((1, GLA_KEY), lambda i: (0, 0)),
        ],
        out_specs=[row_spec(GLA_KEY), row_spec(GLA_KEY), row_spec(GLA_VAL), row_spec(GLA_VAL),
                   row_spec(GLA_KEY)],
        out_shape=[jax.ShapeDtypeStruct((rows, GLA_KEY), BF16),
                   jax.ShapeDtypeStruct((rows, GLA_KEY), BF16),
                   jax.ShapeDtypeStruct((rows, GLA_VAL), BF16),
                   jax.ShapeDtypeStruct((rows, GLA_VAL), BF16),
                   jax.ShapeDtypeStruct((rows, GLA_KEY), F32)],
        compiler_params=_cparams(("parallel",)),
        name="gla_proj",
    )(x, gain, w, w2, b2)


def _lru_kernel(gate_ref, rec_ref, cw_ref, cb_ref, wa_ref, ba_ref, wx_ref, bx_ref, sp_ref,
                conv0_ref, h0_ref, out_ref, conv_t_ref, h_t_ref, xbuf, abuf, bbuf, hcar, *, tl):
    @pl.when(pl.program_id(1) == 0)
    def _():
        xbuf[0:SUBLANES, :] = conv0_ref[...]
        hcar[...] = h0_ref[...]

    xbuf[SUBLANES:SUBLANES + tl, :] = rec_ref[...].astype(F32)
    xc = cb_ref[...]
    for tap in range(4):
        xc = xc + cw_ref[tap:tap + 1, :] * xbuf[SUBLANES - 3 + tap:SUBLANES - 3 + tap + tl, :]
    xbuf[0:SUBLANES, :] = xbuf[tl:tl + SUBLANES, :]

    for blk in range(LRU_BLOCKS):
        sl = slice(blk * LRU_BLOCK, (blk + 1) * LRU_BLOCK)
        xs = xc[:, sl]
        xb = xs.astype(BF16)
        r = jax.nn.sigmoid(_dot(xb, wa_ref[blk]) + ba_ref[:, sl])
        i = jax.nn.sigmoid(_dot(xb, wx_ref[blk]) + bx_ref[:, sl])
        a = jnp.exp(-LRU_C * r * sp_ref[:, sl])
        abuf[:, sl] = a
        bbuf[:, sl] = jnp.sqrt(1.0 - a * a) * (i * xs)

    row = lax.broadcasted_iota(jnp.int32, (SUBLANES, D_MODEL), 0)

    def body(gidx, carry):
        r0 = pl.multiple_of(gidx * SUBLANES, SUBLANES)
        a = abuf[pl.ds(r0, SUBLANES), :]
        b = bbuf[pl.ds(r0, SUBLANES), :]
        for s in (1, 2, 4):
            m = row >= s
            a_sh = pltpu.roll(a, s, 0)
            b_sh = pltpu.roll(b, s, 0)
            b = jnp.where(m, a * b_sh + b, b)
            a = jnp.where(m, a * a_sh, a)
        h = a * carry + b
        bbuf[pl.ds(r0, SUBLANES), :] = h
        return jnp.broadcast_to(h[SUBLANES - 1:SUBLANES, :], (SUBLANES, D_MODEL))

    carry = lax.fori_loop(0, tl // SUBLANES, body, hcar[...])
    hcar[...] = carry
    out_ref[...] = (_gelu_tanh(gate_ref[...].astype(F32)) * bbuf[...]).astype(out_ref.dtype)
    conv_t_ref[...] = xbuf[0:SUBLANES, :]
    h_t_ref[...] = carry


def _lru(gate, rec, p, conv0, h0, nb, tl):
    rows = gate.shape[0]
    nt = rows // (nb * tl)
    tile = pl.BlockSpec((tl, D_MODEL), lambda b, t: (b * nt + t, 0))
    full2 = lambda shp: pl.BlockSpec(shp, lambda b, t: (0, 0))
    full3 = lambda shp: pl.BlockSpec(shp, lambda b, t: (0, 0, 0))
    return pl.pallas_call(
        functools.partial(_lru_kernel, tl=tl),
        grid=(nb, nt),
        in_specs=[tile, tile,
                  full2((4, D_MODEL)), full2((1, D_MODEL)),
                  full3((LRU_BLOCKS, LRU_BLOCK, LRU_BLOCK)), full2((1, D_MODEL)),
                  full3((LRU_BLOCKS, LRU_BLOCK, LRU_BLOCK)), full2((1, D_MODEL)),
                  full2((1, D_MODEL)),
                  full2((SUBLANES, D_MODEL)), full2((SUBLANES, D_MODEL))],
        out_specs=[tile, full2((SUBLANES, D_MODEL)), full2((SUBLANES, D_MODEL))],
        out_shape=[jax.ShapeDtypeStruct((rows, D_MODEL), BF16),
                   jax.ShapeDtypeStruct((SUBLANES, D_MODEL), F32),
                   jax.ShapeDtypeStruct((SUBLANES, D_MODEL), F32)],
        scratch_shapes=[pltpu.VMEM((tl + SUBLANES, D_MODEL), F32),
                        pltpu.VMEM((tl, D_MODEL), F32),
                        pltpu.VMEM((tl, D_MODEL), F32),
                        pltpu.VMEM((SUBLANES, D_MODEL), F32)],
        compiler_params=_cparams(("arbitrary", "arbitrary")),
        name="rg_lru",
    )(gate, rec, p["conv_w"], p["conv_b"], p["gate_a_w"], p["gate_a_b"], p["gate_x_w"],
      p["gate_x_b"], p["softplus"], conv0, h0)


SSM_CHUNK = 256


def _ssm_kernel(u_ref, bblk_ref, cblk_ref, d_ref, gw_ref, gb_ref, pw_ref, h0_ref,
                out_ref, h_t_ref, xs, hcar, *, tl):
    @pl.when(pl.program_id(1) == 0)
    def _():
        hcar[...] = h0_ref[...]

    u = u_ref[...]
    xs[...] = _dot(u, bblk_ref[...])

    for c in range(SSM_LANES // SSM_CHUNK):
        re = slice(c * SSM_CHUNK, (c + 1) * SSM_CHUNK)
        im = slice(SSM_LANES + c * SSM_CHUNK, SSM_LANES + (c + 1) * SSM_CHUNK)
        tabs = [(pw_ref[k * SUBLANES:(k + 1) * SUBLANES, re], pw_ref[k * SUBLANES:(k + 1) * SUBLANES, im])
                for k in range(4)]

        def body(gidx, carry, re=re, im=im, tabs=tabs):
            cr, ci = carry
            r0 = pl.multiple_of(gidx * SUBLANES, SUBLANES)
            hr = xs[pl.ds(r0, SUBLANES), re]
            hi = xs[pl.ds(r0, SUBLANES), im]
            for k, s in enumerate((1, 2, 4)):
                pr, pi = tabs[k]
                sr = pltpu.roll(hr, s, 0)
                si = pltpu.roll(hi, s, 0)
                hr, hi = hr + (pr * sr - pi * si), hi + (pr * si + pi * sr)
            pr, pi = tabs[3]
            hr, hi = hr + (pr * cr - pi * ci), hi + (pr * ci + pi * cr)
            xs[pl.ds(r0, SUBLANES), re] = hr
            xs[pl.ds(r0, SUBLANES), im] = hi
            last = slice(SUBLANES - 1, SUBLANES)
            return (jnp.broadcast_to(hr[last, :], (SUBLANES, SSM_CHUNK)),
                    jnp.broadcast_to(hi[last, :], (SUBLANES, SSM_CHUNK)))

        cr, ci = lax.fori_loop(0, tl // SUBLANES, body, (hcar[:, re], hcar[:, im]))
        hcar[:, re] = cr
        hcar[:, im] = ci

    y = _dot(xs[...].astype(BF16), cblk_ref[...]) + d_ref[...] * u.astype(F32)
    g = _gelu_tanh(y)
    out_ref[...] = (g * jax.nn.sigmoid(_dot(g.astype(BF16), gw_ref[...]) + gb_ref[...])).astype(out_ref.dtype)
    h_t_ref[...] = hcar[...]


def _ssm(u, p, h0, nb, tl):
    rows = u.shape[0]
    nt = rows // (nb * tl)
    tile = pl.BlockSpec((tl, SSM_WIDTH), lambda b, t: (b * nt + t, 0))
    full2 = lambda shp: pl.BlockSpec(shp, lambda b, t: (0, 0))
    return pl.pallas_call(
        functools.partial(_ssm_kernel, tl=tl),
        grid=(nb, nt),
        in_specs=[tile,
                  full2((SSM_WIDTH, 2 * SSM_LANES)), full2((2 * SSM_LANES, SSM_WIDTH)),
                  full2((1, SSM_WIDTH)), full2((SSM_WIDTH, SSM_WIDTH)), full2((1, SSM_WIDTH)),
                  full2((4 * SUBLANES, 2 * SSM_LANES)), full2((SUBLANES, 2 * SSM_LANES))],
        out_specs=[tile, full2((SUBLANES, 2 * SSM_LANES))],
        out_shape=[jax.ShapeDtypeStruct((rows, SSM_WIDTH), BF16),
                   jax.ShapeDtypeStruct((SUBLANES, 2 * SSM_LANES), F32)],
        scratch_shapes=[pltpu.VMEM((tl, 2 * SSM_LANES), F32),
                        pltpu.VMEM((SUBLANES, 2 * SSM_LANES), F32)],
        compiler_params=_cparams(("arbitrary", "arbitrary")),
        name="s5_ssm",
    )(u, p["bblk"], p["cblk"], p["d"], p["glu_w"], p["glu_b"], p["powers"], h0)


def _out_proj_kernel(*refs, n_in):
    h_ref = refs[2 * n_in]
    o_ref = refs[2 * n_in + 1]
    acc = h_ref[...]
    for j in range(n_in):
        acc = acc + _dot(refs[j][...], refs[n_in + j][...])
    o_ref[...] = acc


def _out_proj(xs, ws, h, tm):
    rows = h.shape[0]
    n_in = len(xs)
    in_specs = [pl.BlockSpec((tm, x.shape[1]), lambda i: (i, 0)) for x in xs]
    in_specs += [pl.BlockSpec(w.shape, lambda i: (0, 0)) for w in ws]
    in_specs += [pl.BlockSpec((tm, D_MODEL), lambda i: (i, 0))]
    return pl.pallas_call(
        functools.partial(_out_proj_kernel, n_in=n_in),
        grid=(rows // tm,),
        in_specs=in_specs,
        out_specs=pl.BlockSpec((tm, D_MODEL), lambda i: (i, 0)),
        out_shape=jax.ShapeDtypeStruct((rows, D_MODEL), F32),
        compiler_params=_cparams(("parallel",)),
        name="out_proj",
    )(*xs, *ws, h)


def _ffn_kernel(h_ref, g_ref, wg_ref, wu_ref, wd_ref, o_ref, xn_ref, acc_ref):
    f = pl.program_id(1)

    @pl.when(f == 0)
    def _():
        xn_ref[...] = _rms(h_ref[...], g_ref[...]).astype(BF16)
        acc_ref[...] = h_ref[...]

    xn = xn_ref[...]
    a = (_silu(_dot(xn, wg_ref[...])) * _dot(xn, wu_ref[...])).astype(BF16)
    acc_ref[...] += _dot(a, wd_ref[...])

    @pl.when(f == pl.num_programs(1) - 1)
    def _():
        o_ref[...] = acc_ref[...]


def _ffn(h, gain, wg, wu, wd, tm, fc):
    rows = h.shape[0]
    dff = wg.shape[1]
    return pl.pallas_call(
        _ffn_kernel,
        grid=(rows // tm, dff // fc),
        in_specs=[pl.BlockSpec((tm, D_MODEL), lambda i, f: (i, 0)),
                  pl.BlockSpec((1, D_MODEL), lambda i, f: (0, 0)),
                  pl.BlockSpec((D_MODEL, fc), lambda i, f: (0, f)),
                  pl.BlockSpec((D_MODEL, fc), lambda i, f: (0, f)),
                  pl.BlockSpec((fc, D_MODEL), lambda i, f: (f, 0))],
        out_specs=pl.BlockSpec((tm, D_MODEL), lambda i, f: (i, 0)),
        out_shape=jax.ShapeDtypeStruct((rows, D_MODEL), F32),
        scratch_shapes=[pltpu.VMEM((tm, D_MODEL), BF16), pltpu.VMEM((tm, D_MODEL), F32)],
        compiler_params=_cparams(("parallel", "arbitrary")),
        name="dense_ffn",
    )(h, gain, wg, wu, wd)


def _gla_kernel(q_ref, k_ref, v_ref, og_ref, lg_ref, hn_ref, s0_ref, o_ref, s_t_ref, st, *, tl, ck):
    @pl.when(pl.program_id(1) == 0)
    def _():
        st[...] = s0_ref[...]

    ri = lax.broadcasted_iota(jnp.int32, (ck, ck), 0)
    ci = lax.broadcasted_iota(jnp.int32, (ck, ck), 1)
    causal = ri >= ci
    tri = causal.astype(F32).astype(BF16)
    scale = GLA_DK ** -0.5

    for c in range(tl // ck):
        rows = slice(c * ck, (c + 1) * ck)
        lg_hi, lg_lo = _split_bf16(lg_ref[rows, :])
        b_all = _dot(tri, lg_hi) + _dot(tri, lg_lo)
        for h in range(GLA_HEADS):
            ks = slice(h * GLA_DK, (h + 1) * GLA_DK)
            vs = slice(h * GLA_DV, (h + 1) * GLA_DV)
            b = b_all[:, ks]
            b_last = b[ck - 1:ck, :]
            q = q_ref[rows, ks].astype(F32)
            k = k_ref[rows, ks].astype(F32)
            v = v_ref[rows, vs]
            qd = (q * (scale * jnp.exp(b))).astype(BF16)
            kd = (k * jnp.exp(-b)).astype(BF16)
            kl = (k * jnp.exp(b_last - b)).astype(BF16)
            att = lax.dot_general(qd, kd, (((1,), (1,)), ((), ())), preferred_element_type=F32)
            att = jnp.where(causal, att, 0.0).astype(BF16)
            s_h = st[h]
            o = _dot(att, v) + lax.dot_general(qd, s_h.astype(BF16), (((1,), (1,)), ((), ())),
                                               preferred_element_type=F32)
            st[h] = s_h * jnp.exp(b_last) + lax.dot_general(v, kl, (((0,), (0,)), ((), ())),
                                                            preferred_element_type=F32)
            o = o * lax.rsqrt(jnp.mean(o * o, axis=-1, keepdims=True) + EPS) * hn_ref[:, vs]
            o_ref[rows, vs] = (o * _silu(og_ref[rows, vs].astype(F32))).astype(o_ref.dtype)
    s_t_ref[...] = st[...]


def _gla(q, k, v, og, lg, head_norm, s0, nb, tl, ck):
    rows = q.shape[0]
    nt = rows // (nb * tl)
    tile = lambda c: pl.BlockSpec((tl, c), lambda b, t: (b * nt + t, 0))
    state = pl.BlockSpec((GLA_HEADS, GLA_DV, GLA_DK), lambda b, t: (0, 0, 0))
    return pl.pallas_call(
        functools.partial(_gla_kernel, tl=tl, ck=ck),
        grid=(nb, nt),
        in_specs=[tile(GLA_KEY), tile(GLA_KEY), tile(GLA_VAL), tile(GLA_VAL), tile(GLA_KEY),
                  pl.BlockSpec((1, GLA_VAL), lambda b, t: (0, 0)), state],
        out_specs=[tile(GLA_VAL), state],
        out_shape=[jax.ShapeDtypeStruct((rows, GLA_VAL), BF16),
                   jax.ShapeDtypeStruct((GLA_HEADS, GLA_DV, GLA_DK), F32)],
        scratch_shapes=[pltpu.VMEM((GLA_HEADS, GLA_DV, GLA_DK), F32)],
        compiler_params=_cparams(("arbitrary", "arbitrary")),
        name="gla",
    )(q, k, v, og, lg, head_norm, s0)


TOK_TILE = 512
SEG_ALIGN = 16
LOCAL_ROWS = 1152
FFN_TILE = 512
XS_COLS = D_MODEL + LANES
SEG_BITS = (512, 256, 128, 64, 32, 16)
INFO_E1, INFO_E2, INFO_R1, INFO_R2, INFO_W1, INFO_W2 = range(6)
assert LOCAL_ROWS >= 2 * TOK_TILE + N_EXPERTS * (SEG_ALIGN - 1) and LOCAL_ROWS % LANES == 0


def _router_kernel(h_ref, g_ref, wr_ref, tri_ref, xn_ref, info_ref, cnt_ref):
    tm = h_ref.shape[0]
    xn = _rms(h_ref[...], g_ref[...])
    xn_ref[...] = xn.astype(BF16)
    x_hi, x_lo = _split_bf16(xn)
    w_hi = wr_ref[0]
    w_lo = wr_ref[1]
    logits = _dot(x_hi, w_hi) + _dot(x_lo, w_hi) + _dot(x_hi, w_lo)
    lane = lax.broadcasted_iota(jnp.int32, logits.shape, 1).astype(F32)
    neg = jnp.float32(-jnp.inf)
    logits = jnp.where(lane < N_EXPERTS, logits, neg)
    v1 = jnp.max(logits, axis=-1, keepdims=True)
    i1 = jnp.min(jnp.where(logits == v1, lane, float(LANES)), axis=-1, keepdims=True)
    rest = jnp.where(lane == i1, neg, logits)
    v2 = jnp.max(rest, axis=-1, keepdims=True)
    i2 = jnp.min(jnp.where(rest == v2, lane, float(LANES)), axis=-1, keepdims=True)
    e2 = jnp.exp(v2 - v1)
    w1 = 1.0 / (1.0 + e2)
    w2 = e2 / (1.0 + e2)
    hit1 = lane == i1
    hit2 = lane == i2
    sel = jnp.where(hit1, 1.0, 0.0) + jnp.where(hit2, 1.0, 0.0)
    cum = _dot(tri_ref[...], sel.astype(BF16))
    rank1 = jnp.sum(jnp.where(hit1, cum, 0.0), axis=-1, keepdims=True) - 1.0
    rank2 = jnp.sum(jnp.where(hit2, cum, 0.0), axis=-1, keepdims=True) - 1.0
    info = jnp.zeros_like(logits)
    for col, val in ((INFO_E1, i1), (INFO_E2, i2), (INFO_R1, rank1), (INFO_R2, rank2),
                     (INFO_W1, w1), (INFO_W2, w2)):
        info = jnp.where(lane == col, val, info)
    info_ref[...] = info
    cnt_ref[...] = jnp.broadcast_to(cum[tm - 1:tm, :], (SUBLANES, LANES))


def _router(h, gain, wr, tri):
    rows = h.shape[0]
    tm = TOK_TILE
    nt = rows // tm
    return pl.pallas_call(
        _router_kernel,
        grid=(nt,),
        in_specs=[pl.BlockSpec((tm, D_MODEL), lambda i: (i, 0)),
                  pl.BlockSpec((1, D_MODEL), lambda i: (0, 0)),
                  pl.BlockSpec((2, D_MODEL, LANES), lambda i: (0, 0, 0)),
                  pl.BlockSpec((tm, tm), lambda i: (0, 0))],
        out_specs=[pl.BlockSpec((tm, D_MODEL), lambda i: (i, 0)),
                   pl.BlockSpec((tm, LANES), lambda i: (i, 0)),
                   pl.BlockSpec((SUBLANES, LANES), lambda i: (i, 0))],
        out_shape=[jax.ShapeDtypeStruct((rows, D_MODEL), BF16),
                   jax.ShapeDtypeStruct((rows, LANES), F32),
                   jax.ShapeDtypeStruct((nt * SUBLANES, LANES), F32)],
        compiler_params=_cparams(("parallel",)),
        name="router",
    )(h, gain, wr, tri)


def _segment_copies(tile, loff_ref, slen_ref, goff_ref, make_copy, fn):
    for e in range(N_EXPERTS):
        idx = tile * N_EXPERTS + e
        lo = loff_ref[idx]
        n = slen_ref[idx]
        go = goff_ref[idx]
        for bit in SEG_BITS:
            pos = n & (-2 * bit)

            @pl.when((n & bit) != 0)
            def _(lo=lo, go=go, pos=pos, bit=bit):
                fn(make_copy(pl.multiple_of(lo + pos, SEG_ALIGN), pl.multiple_of(go + pos, SEG_ALIGN), bit))


def _local_rows(info_row, loff_ref, tile, which_e, which_r):
    e = info_row(which_e)
    loc = info_row(which_r)
    for ex in range(N_EXPERTS):
        loc = loc + jnp.where(e == float(ex), loff_ref[tile * N_EXPERTS + ex].astype(F32), 0.0)
    return loc


def _dispatch_kernel(loff_ref, slen_ref, goff_ref, xn_ref, info_ref, zero_ref, xs_ref, buf, sem):
    del zero_ref
    i = pl.program_id(0)
    n = pl.num_programs(0)
    slot = i % 2

    def copy_for(s):
        def make(lo, go, rows):
            return pltpu.make_async_copy(buf.at[s, pl.ds(lo, rows), :], xs_ref.at[pl.ds(go, rows), :],
                                         sem.at[s])
        return make

    wait = lambda c: c.wait()
    start = lambda c: c.start()

    @pl.when(i >= 2)
    def _():
        _segment_copies(i - 2, loff_ref, slen_ref, goff_ref, copy_for(slot), wait)

    info_t = info_ref[...].T
    row_of = lambda r: info_t[r:r + 1, :]
    loc1 = _local_rows(row_of, loff_ref, i, INFO_E1, INFO_R1)
    loc2 = _local_rows(row_of, loff_ref, i, INFO_E2, INFO_R2)
    ridx = lax.broadcasted_iota(jnp.int32, (LOCAL_ROWS, TOK_TILE), 0).astype(F32)
    p1 = jnp.where(ridx == loc1, 1.0, 0.0).astype(BF16)
    p2 = jnp.where(ridx == loc2, 1.0, 0.0).astype(BF16)
    info = info_ref[...]
    lane = lax.broadcasted_iota(jnp.int32, info.shape, 1)

    def weight_parts(col):
        w = info[:, col:col + 1]
        hi = w.astype(BF16).astype(F32)
        mid = (w - hi).astype(BF16).astype(F32)
        lo = w - hi - mid
        return jnp.where(lane == 0, hi, jnp.where(lane == 1, mid, jnp.where(lane == 2, lo, 0.0))).astype(BF16)

    buf[slot, :, 0:D_MODEL] = _dot(p1 + p2, xn_ref[...]).astype(BF16)
    buf[slot, :, D_MODEL:XS_COLS] = (_dot(p1, weight_parts(INFO_W1)) + _dot(p2, weight_parts(INFO_W2))).astype(BF16)
    _segment_copies(i, loff_ref, slen_ref, goff_ref, copy_for(slot), start)

    @pl.when(i == n - 1)
    def _():
        _segment_copies(i, loff_ref, slen_ref, goff_ref, copy_for(slot), wait)

        @pl.when(i >= 1)
        def _():
            _segment_copies(i - 1, loff_ref, slen_ref, goff_ref, copy_for(1 - slot), wait)


def _dispatch(loff, slen, goff, xn, info, n_sort):
    rows = xn.shape[0]
    nt = rows // TOK_TILE
    zeros = jnp.zeros((n_sort, XS_COLS), BF16)
    return pl.pallas_call(
        _dispatch_kernel,
        grid_spec=pltpu.PrefetchScalarGridSpec(
            num_scalar_prefetch=3,
            grid=(nt,),
            in_specs=[pl.BlockSpec((TOK_TILE, D_MODEL), lambda i, *_: (i, 0)),
                      pl.BlockSpec((TOK_TILE, LANES), lambda i, *_: (i, 0)),
                      pl.BlockSpec(memory_space=pl.ANY)],
            out_specs=pl.BlockSpec(memory_space=pl.ANY),
            scratch_shapes=[pltpu.VMEM((2, LOCAL_ROWS, XS_COLS), BF16),
                            pltpu.SemaphoreType.DMA((2,))],
        ),
        out_shape=jax.ShapeDtypeStruct((n_sort, XS_COLS), BF16),
        input_output_aliases={5: 0},
        compiler_params=_cparams(("arbitrary",)),
        name="moe_dispatch",
    )(loff, slen, goff, xn, info, zeros)


def _moe_ffn_kernel(te_ref, tot_ref, xs_ref, wg_ref, wu_ref, wd_ref, y_ref, w_scr, acc_ref):
    del te_ref
    j = pl.program_id(0)
    f = pl.program_id(1)

    @pl.when(j < tot_ref[0])
    def _():
        @pl.when(f == 0)
        def _():
            w = jnp.sum(xs_ref[:, D_MODEL:XS_COLS].astype(F32), axis=-1, keepdims=True)
            w_scr[...] = jnp.broadcast_to(w, (FFN_TILE, LANES))
            acc_ref[...] = jnp.zeros_like(acc_ref)

        x = xs_ref[:, 0:D_MODEL]
        a = _silu(_dot(x, wg_ref[0])) * _dot(x, wu_ref[0]) * w_scr[:, 0:1]
        acc_ref[...] += _dot(a.astype(BF16), wd_ref[0])

        @pl.when(f == pl.num_programs(1) - 1)
        def _():
            y_ref[...] = acc_ref[...].astype(y_ref.dtype)

    @pl.when((j >= tot_ref[0]) & (f == pl.num_programs(1) - 1))
    def _():
        y_ref[...] = jnp.zeros_like(y_ref)


def _moe_ffn(tile_e, total, xs, wg, wu, wd, fc):
    n_sort = xs.shape[0]
    nt = n_sort // FFN_TILE
    nf = wg.shape[2] // fc
    tile_of = lambda j, tot: jnp.minimum(j, tot[0] - 1)
    chunk_of = lambda j, f, tot: jnp.where(j < tot[0], f, nf - 1)
    return pl.pallas_call(
        _moe_ffn_kernel,
        grid_spec=pltpu.PrefetchScalarGridSpec(
            num_scalar_prefetch=2,
            grid=(nt, nf),
            in_specs=[pl.BlockSpec((FFN_TILE, XS_COLS), lambda j, f, te, tot: (tile_of(j, tot), 0)),
                      pl.BlockSpec((1, D_MODEL, fc), lambda j, f, te, tot: (te[j], 0, chunk_of(j, f, tot))),
                      pl.BlockSpec((1, D_MODEL, fc), lambda j, f, te, tot: (te[j], 0, chunk_of(j, f, tot))),
                      pl.BlockSpec((1, fc, D_MODEL), lambda j, f, te, tot: (te[j], chunk_of(j, f, tot), 0))],
            out_specs=pl.BlockSpec((FFN_TILE, D_MODEL), lambda j, f, te, tot: (j, 0)),
            scratch_shapes=[pltpu.VMEM((FFN_TILE, LANES), F32),
                            pltpu.VMEM((FFN_TILE, D_MODEL), F32)],
        ),
        out_shape=jax.ShapeDtypeStruct((n_sort, D_MODEL), BF16),
        compiler_params=_cparams(("arbitrary", "arbitrary")),
        name="moe_ffn",
    )(tile_e, total, xs, wg, wu, wd)


def _combine_kernel(loff_ref, slen_ref, goff_ref, info_ref, h_ref, fn_ref, y_ref, o_ref, buf, sem):
    i = pl.program_id(0)
    n = pl.num_programs(0)
    slot = i % 2

    def copy_for(s):
        def make(lo, go, rows):
            return pltpu.make_async_copy(y_ref.at[pl.ds(go, rows), :], buf.at[s, pl.ds(lo, rows), :],
                                         sem.at[s])
        return make

    wait = lambda c: c.wait()
    start = lambda c: c.start()

    @pl.when(i == 0)
    def _():
        buf[...] = jnp.zeros_like(buf)
        _segment_copies(i, loff_ref, slen_ref, goff_ref, copy_for(slot), start)

    @pl.when(i + 1 < n)
    def _():
        _segment_copies(i + 1, loff_ref, slen_ref, goff_ref, copy_for(1 - slot), start)

    _segment_copies(i, loff_ref, slen_ref, goff_ref, copy_for(slot), wait)

    ys = buf[slot]
    info = info_ref[...]
    col_of = lambda c: info[:, c:c + 1]
    loc1 = _local_rows(col_of, loff_ref, i, INFO_E1, INFO_R1)
    loc2 = _local_rows(col_of, loff_ref, i, INFO_E2, INFO_R2)
    cidx = lax.broadcasted_iota(jnp.int32, (TOK_TILE, LOCAL_ROWS), 1).astype(F32)
    pt = (jnp.where(cidx == loc1, 1.0, 0.0) + jnp.where(cidx == loc2, 1.0, 0.0)).astype(BF16)
    o_ref[...] = _rms(h_ref[...] + _dot(pt, ys), fn_ref[...])


def _combine(loff, slen, goff, info, h, final_norm, y):
    rows = h.shape[0]
    nt = rows // TOK_TILE
    return pl.pallas_call(
        _combine_kernel,
        grid_spec=pltpu.PrefetchScalarGridSpec(
            num_scalar_prefetch=3,
            grid=(nt,),
            in_specs=[pl.BlockSpec((TOK_TILE, LANES), lambda i, *_: (i, 0)),
                      pl.BlockSpec((TOK_TILE, D_MODEL), lambda i, *_: (i, 0)),
                      pl.BlockSpec((1, D_MODEL), lambda i, *_: (0, 0)),
                      pl.BlockSpec(memory_space=pl.ANY)],
            out_specs=pl.BlockSpec((TOK_TILE, D_MODEL), lambda i, *_: (i, 0)),
            scratch_shapes=[pltpu.VMEM((2, LOCAL_ROWS, D_MODEL), BF16),
                            pltpu.SemaphoreType.DMA((2,))],
        ),
        out_shape=jax.ShapeDtypeStruct((rows, D_MODEL), F32),
        compiler_params=_cparams(("arbitrary",)),
        name="moe_combine",
    )(loff, slen, goff, info, h, final_norm, y)


def _routing_tables(cnt, n_tiles_max):
    slen = (cnt + SEG_ALIGN - 1) // SEG_ALIGN * SEG_ALIGN
    loff = jnp.cumsum(slen, axis=1) - slen
    rows_e = jnp.sum(slen, axis=0)
    rows_pad = (rows_e + FFN_TILE - 1) // FFN_TILE * FFN_TILE
    base = jnp.cumsum(rows_pad) - rows_pad
    goff = base[None, :] + jnp.cumsum(slen, axis=0) - slen
    tile_end = jnp.cumsum(rows_pad // FFN_TILE)
    total = tile_end[-1]
    j = jnp.arange(n_tiles_max, dtype=jnp.int32)
    tile_e = jnp.sum(jnp.minimum(j, total - 1)[:, None] >= tile_end[None, :], axis=1).astype(jnp.int32)
    flat = lambda a: a.reshape(-1).astype(jnp.int32)
    return flat(loff), flat(slen), flat(goff), tile_e, total.reshape(1).astype(jnp.int32)


def _moe(h, gain, wr, wg, wu, wd, final_norm):
    rows = h.shape[0]
    assert rows % TOK_TILE == 0
    nt = rows // TOK_TILE
    max_rows = 2 * rows + nt * N_EXPERTS * (SEG_ALIGN - 1) + N_EXPERTS * (FFN_TILE - SEG_ALIGN)
    n_tiles_max = -(-max_rows // FFN_TILE)
    tri = jnp.tril(jnp.ones((TOK_TILE, TOK_TILE), BF16))
    xn, info, cnt = _router(h, gain, wr, tri)
    cnt = cnt.reshape(nt, SUBLANES, LANES)[:, 0, :N_EXPERTS].astype(jnp.int32)
    loff, slen, goff, tile_e, total = _routing_tables(cnt, n_tiles_max)
    xs = _dispatch(loff, slen, goff, xn, info, n_tiles_max * FFN_TILE)
    y = _moe_ffn(tile_e, total, xs, wg, wu, wd, 512)
    return _combine(loff, slen, goff, info, h, final_norm, y)


def _ssm_params(lam_re, lam_im, log_dt, b_re, b_im, c_re, c_im, d, glu_w, glu_b):
    g, n, hh = SSM_GROUPS, SSM_STATE, SSM_GROUP
    lr, li = lam_re.astype(F32), lam_im.astype(F32)
    dt = jnp.exp(log_dt.astype(F32))[:, None]
    mag = jnp.exp(dt * lr)
    ar, ai = mag * jnp.cos(dt * li), mag * jnp.sin(dt * li)
    den = lr * lr + li * li
    nr = ar - 1.0
    zr = (nr * lr + ai * li) / den
    zi = (ai * lr - nr * li) / den
    br = zr[..., None] * b_re - zi[..., None] * b_im
    bi = zr[..., None] * b_im + zi[..., None] * b_re
    eye = jnp.eye(g, dtype=F32)
    bblk = jnp.concatenate(
        [jnp.einsum("gnh,gk->ghkn", x, eye).reshape(SSM_WIDTH, SSM_LANES) for x in (br, bi)], axis=1)
    cblk = jnp.concatenate(
        [jnp.einsum("ghn,gk->gnkh", x, eye).reshape(SSM_LANES, SSM_WIDTH) for x in (c_re, -c_im)], axis=0)

    def power(m):
        pr, pi = jnp.ones((SUBLANES, g, n), F32), jnp.zeros((SUBLANES, g, n), F32)
        for step in range(1, SUBLANES + 1):
            nr_, ni_ = pr * ar - pi * ai, pr * ai + pi * ar
            take = (m >= step)[:, :, None]
            pr, pi = jnp.where(take, nr_, pr), jnp.where(take, ni_, pi)
        return pr.reshape(SUBLANES, SSM_LANES), pi.reshape(SUBLANES, SSM_LANES)

    sub = jnp.arange(SUBLANES, dtype=jnp.int32)[:, None]
    tabs = []
    for s in (1, 2, 4):
        pr, pi = power(jnp.full((SUBLANES, 1), s, jnp.int32))
        keep = (sub >= s).astype(F32)
        tabs.append(jnp.concatenate([pr * keep, pi * keep], axis=1))
    pr, pi = power(sub + 1)
    tabs.append(jnp.concatenate([pr, pi], axis=1))
    return {
        "bblk": bblk.astype(BF16), "cblk": cblk.astype(BF16),
        "d": d.reshape(1, SSM_WIDTH).astype(F32),
        "glu_w": glu_w.astype(BF16), "glu_b": glu_b.reshape(1, SSM_WIDTH).astype(F32),
        "powers": jnp.concatenate(tabs, axis=0),
    }


def _row(x):
    return x.reshape(1, -1).astype(F32)


def _pad_cols(w, n):
    return jnp.pad(w, ((0, 0), (0, n - w.shape[1])))


def kernel(x, meta_tokens, ev_norm_mix, ev_w_in, ev_conv_w, ev_conv_b, ev_gate_a_w, ev_gate_a_b, ev_gate_x_w, ev_gate_x_b, ev_lru_lambda, ev_ssm_lambda_re, ev_ssm_lambda_im, ev_ssm_log_dt, ev_ssm_b_re, ev_ssm_b_im, ev_ssm_c_re, ev_ssm_c_im, ev_ssm_d, ev_ssm_glu_w, ev_ssm_glu_b, ev_w_out, ev_norm_ffn, ev_ffn_w_gate, ev_ffn_w_up, ev_ffn_w_down, od_norm_mix, od_w_in, od_gla_gate_w2, od_gla_gate_b, od_gla_norm, od_w_out, od_norm_ffn, od_router_w, od_moe_w_gate, od_moe_w_up, od_moe_w_down, final_norm):
    nb, seq, _ = x.shape
    rows = nb * seq
    assert ev_w_in.shape[0] == 1 and od_w_in.shape[0] == 1, "two-layer trunk only"

    w_in0 = ev_w_in[0].astype(BF16)
    lru_p = {
        "conv_w": ev_conv_w[0].astype(F32), "conv_b": _row(ev_conv_b[0]),
        "gate_a_w": ev_gate_a_w[0].astype(BF16), "gate_a_b": _row(ev_gate_a_b[0]),
        "gate_x_w": ev_gate_x_w[0].astype(BF16), "gate_x_b": _row(ev_gate_x_b[0]),
        "softplus": _row(jax.nn.softplus(-ev_lru_lambda[0].astype(F32))),
    }
    ssm_p = _ssm_params(ev_ssm_lambda_re[0], ev_ssm_lambda_im[0], ev_ssm_log_dt[0], ev_ssm_b_re[0],
                        ev_ssm_b_im[0], ev_ssm_c_re[0], ev_ssm_c_im[0], ev_ssm_d[0],
                        ev_ssm_glu_w[0], ev_ssm_glu_b[0])
    w_out0 = ev_w_out[0].astype(BF16)
    w_out0_a, w_out0_b = w_out0[:D_MODEL], w_out0[D_MODEL:]
    ffn_g, ffn_u, ffn_d = (w[0].astype(BF16) for w in (ev_ffn_w_gate, ev_ffn_w_up, ev_ffn_w_down))
    odd_in = 2 * GLA_KEY + 2 * GLA_VAL
    w_in1 = _pad_cols(od_w_in[0], odd_in + LANES).astype(BF16)
    w2 = jnp.pad(od_gla_gate_w2[0].astype(F32), ((0, LANES - GLA_RANK), (0, 0)))
    w2_hi = w2.astype(BF16)
    w2_split = jnp.stack([w2_hi, (w2 - w2_hi.astype(F32)).astype(BF16)])
    wr = _pad_cols(od_router_w[0].astype(F32), LANES)
    wr_hi = wr.astype(BF16)
    wr_split = jnp.stack([wr_hi, (wr - wr_hi.astype(F32)).astype(BF16)])
    w_out1 = od_w_out[0].astype(BF16)
    moe_g, moe_u, moe_d = (w[0].astype(BF16) for w in (od_moe_w_gate, od_moe_w_up, od_moe_w_down))

    tl = min(256, seq)
    tm = min(512, rows)
    ck = min(128, tl)
    even_splits = (D_MODEL, D_MODEL, SSM_WIDTH)

    def even_mixer(h, nbatch, tile_m, tile_l, conv0, h0, s0):
        gate, rec, u = _norm_proj(h, _row(ev_norm_mix[0]), w_in0, even_splits, tile_m)
        a_out, conv_t, h_t = _lru(gate, rec, lru_p, conv0, h0, nbatch, tile_l)
        b_out, s_t = _ssm(u, ssm_p, s0, nbatch, tile_l)
        h = _out_proj([a_out, b_out], [w_out0_a, w_out0_b], h, tile_m)
        h = _ffn(h, _row(ev_norm_ffn[0]), ffn_g, ffn_u, ffn_d, tile_m, 512)
        return h, (conv_t, h_t, s_t)

    def gla_inputs(h, tile_m):
        return _gla_proj(h, _row(od_norm_mix[0]), w_in1, w2_split, _row(od_gla_gate_b[0]), tile_m)

    zeros = lambda *s: jnp.zeros(s, F32)
    hm = meta_tokens.astype(F32)
    hm, (conv_m, h_m, s_m) = even_mixer(hm, 1, N_META, N_META, zeros(SUBLANES, D_MODEL),
                                        zeros(SUBLANES, D_MODEL), zeros(SUBLANES, 2 * SSM_LANES))
    qm, km, vm, ogm, lgm = gla_inputs(hm, N_META)
    _, gla_s = _gla(qm, km, vm, ogm, lgm, _row(od_gla_norm[0]),
                    zeros(GLA_HEADS, GLA_DV, GLA_DK), 1, N_META, N_META)

    h = x.reshape(rows, D_MODEL).astype(F32)
    h, _ = even_mixer(h, nb, tm, tl, conv_m, h_m, s_m)
    q, k, v, og, lg = gla_inputs(h, tm)
    o, _ = _gla(q, k, v, og, lg, _row(od_gla_norm[0]), gla_s, nb, tl, ck)
    h = _out_proj([o], [w_out1], h, tm)
    out = _moe(h, _row(od_norm_ffn[0]), wr_split, moe_g, moe_u, moe_d, _row(final_norm))
    return out.reshape(nb, seq, D_MODEL)
```

```python
import functools
import math

import jax
import jax.numpy as jnp
from jax import lax
from jax.experimental import pallas as pl
from jax.experimental.pallas import tpu as pltpu

F32 = jnp.float32
BF16 = jnp.bfloat16

D_MODEL = 1024
N_META = 16
EPS = 1e-6
LRU_BLOCKS = 4
LRU_BLOCK = 256
LRU_C = 8.0
SSM_WIDTH = 512
SSM_GROUP = 16
SSM_GROUPS = 32
SSM_STATE = 64
SSM_LANES = SSM_GROUPS * SSM_STATE
D_FF = 3072
GLA_HEADS = 4
GLA_DK = 128
GLA_DV = 256
GLA_KEY = 512
GLA_VAL = 1024
GLA_RANK = 16
GLA_TAU = 16.0
N_EXPERTS = 8
D_FF_EXPERT = 3584

SUBLANES = 8
LANES = 128
VMEM_LIMIT = 52 * 1024 * 1024


def _cparams(sem):
    return pltpu.CompilerParams(dimension_semantics=sem, vmem_limit_bytes=VMEM_LIMIT)


def _dot(a, b):
    return jnp.dot(a, b, preferred_element_type=F32)


def _rms(x, gain):
    ms = jnp.mean(x * x, axis=-1, keepdims=True)
    return x * lax.rsqrt(ms + EPS) * gain


def _gelu_tanh(x):
    c = math.sqrt(2.0 / math.pi)
    return 0.5 * x * (1.0 + jnp.tanh(c * (x + 0.044715 * (x * x * x))))


def _silu(x):
    return x * jax.nn.sigmoid(x)


def _split_bf16(x):
    hi = x.astype(BF16)
    lo = (x - hi.astype(F32)).astype(BF16)
    return hi, lo


def _norm_proj_kernel(x_ref, g_ref, w_ref, *out_refs):
    xn = _rms(x_ref[...], g_ref[...]).astype(BF16)
    off = 0
    for o_ref in out_refs:
        n = o_ref.shape[-1]
        o_ref[...] = _dot(xn, w_ref[:, off:off + n]).astype(o_ref.dtype)
        off += n


def _norm_proj(x, gain, w, splits, tm):
    rows = x.shape[0]
    n = w.shape[1]
    assert sum(splits) == n and rows % tm == 0
    return pl.pallas_call(
        _norm_proj_kernel,
        grid=(rows // tm,),
        in_specs=[
            pl.BlockSpec((tm, D_MODEL), lambda i: (i, 0)),
            pl.BlockSpec((1, D_MODEL), lambda i: (0, 0)),
            _resident((D_MODEL, n)),
        ],
        out_specs=[pl.BlockSpec((tm, s), lambda i: (i, 0)) for s in splits],
        out_shape=[jax.ShapeDtypeStruct((rows, s), BF16) for s in splits],
        compiler_params=_cparams(("parallel",)),
        name="norm_proj",
    )(x, gain, w)


def _gla_proj_kernel(x_ref, g_ref, w_ref, w2_ref, b2_ref, q_ref, k_ref, v_ref, og_ref, lg_ref):
    xn = _rms(x_ref[...], g_ref[...]).astype(BF16)
    q_ref[...] = _dot(xn, w_ref[:, 0:GLA_KEY]).astype(BF16)
    k_ref[...] = _dot(xn, w_ref[:, GLA_KEY:2 * GLA_KEY]).astype(BF16)
    v_ref[...] = _dot(xn, w_ref[:, 2 * GLA_KEY:2 * GLA_KEY + GLA_VAL]).astype(BF16)
    o0 = 2 * GLA_KEY + GLA_VAL
    og_ref[...] = _dot(xn, w_ref[:, o0:o0 + GLA_VAL]).astype(BF16)
    glr = _dot(xn, w_ref[:, o0 + GLA_VAL:o0 + GLA_VAL + LANES])
    glr_hi, glr_lo = _split_bf16(glr)
    w2_hi = w2_ref[0]
    w2_lo = w2_ref[1]
    pre = _dot(glr_hi, w2_hi) + _dot(glr_lo, w2_hi) + _dot(glr_hi, w2_lo) + b2_ref[...]
    lg_ref[...] = (jnp.minimum(pre, 0.0) - jnp.log1p(jnp.exp(-jnp.abs(pre)))) * (1.0 / GLA_TAU)


def _gla_proj(x, gain, w, w2, b2, tm):
    rows = x.shape[0]
    n = w.shape[1]
    row_spec = lambda c: pl.BlockSpec((tm, c), lambda i: (i, 0))
    return pl.pallas_call(
        _gla_proj_kernel,
        grid=(rows // tm,),
        in_specs=[
            row_spec(D_MODEL),
            pl.BlockSpec((1, D_MODEL), lambda i: (0, 0)),
            _resident((D_MODEL, n)),
            pl.BlockSpec((2, LANES, GLA_KEY), lambda i: (0, 0, 0)),
            pl.BlockSpec((1, GLA_KEY), lambda i: (0, 0)),
        ],
        out_specs=[row_spec(GLA_KEY), row_spec(GLA_KEY), row_spec(GLA_VAL), row_spec(GLA_VAL),
                   row_spec(GLA_KEY)],
        out_shape=[jax.ShapeDtypeStruct((rows, GLA_KEY), BF16),
                   jax.ShapeDtypeStruct((rows, GLA_KEY), BF16),
                   jax.ShapeDtypeStruct((rows, GLA_VAL), BF16),
                   jax.ShapeDtypeStruct((rows, GLA_VAL), BF16),
                   jax.ShapeDtypeStruct((rows, GLA_KEY), F32)],
        compiler_params=_cparams(("parallel",)),
        name="gla_proj",
    )(x, gain, w, w2, b2)


def _lru_kernel(gate_ref, rec_ref, cw_ref, cb_ref, wa_ref, ba_ref, wx_ref, bx_ref, sp_ref,
                conv0_ref, h0_ref, out_ref, conv_t_ref, h_t_ref, xbuf, abuf, bbuf, hcar, *, tl):
    @pl.when(pl.program_id(1) == 0)
    def _():
        xbuf[0:SUBLANES, :] = conv0_ref[...]
        hcar[...] = h0_ref[...]

    xbuf[SUBLANES:SUBLANES + tl, :] = rec_ref[...].astype(F32)
    xc = cb_ref[...]
    for tap in range(4):
        xc = xc + cw_ref[tap:tap + 1, :] * xbuf[SUBLANES - 3 + tap:SUBLANES - 3 + tap + tl, :]
    xbuf[0:SUBLANES, :] = xbuf[tl:tl + SUBLANES, :]

    for blk in range(LRU_BLOCKS):
        sl = slice(blk * LRU_BLOCK, (blk + 1) * LRU_BLOCK)
        xs = xc[:, sl]
        xb = xs.astype(BF16)
        r = jax.nn.sigmoid(_dot(xb, wa_ref[blk]) + ba_ref[:, sl])
        i = jax.nn.sigmoid(_dot(xb, wx_ref[blk]) + bx_ref[:, sl])
        a = jnp.exp(-LRU_C * r * sp_ref[:, sl])
        abuf[:, sl] = a
        bbuf[:, sl] = jnp.sqrt(1.0 - a * a) * (i * xs)

    row = lax.broadcasted_iota(jnp.int32, (SUBLANES, D_MODEL), 0)

    def body(gidx, carry):
        r0 = pl.multiple_of(gidx * SUBLANES, SUBLANES)
        a = abuf[pl.ds(r0, SUBLANES), :]
        b = bbuf[pl.ds(r0, SUBLANES), :]
        for s in (1, 2, 4):
            m = row >= s
            a_sh = pltpu.roll(a, s, 0)
            b_sh = pltpu.roll(b, s, 0)
            b = jnp.where(m, a * b_sh + b, b)
            a = jnp.where(m, a * a_sh, a)
        h = a * carry + b
        bbuf[pl.ds(r0, SUBLANES), :] = h
        return jnp.broadcast_to(h[SUBLANES - 1:SUBLANES, :], (SUBLANES, D_MODEL))

    carry = lax.fori_loop(0, tl // SUBLANES, body, hcar[...])
    hcar[...] = carry
    out_ref[...] = (_gelu_tanh(gate_ref[...].astype(F32)) * bbuf[...]).astype(out_ref.dtype)
    conv_t_ref[...] = xbuf[0:SUBLANES, :]
    h_t_ref[...] = carry


def _lru(gate, rec, p, conv0, h0, nb, tl):
    rows = gate.shape[0]
    nt = rows // (nb * tl)
    tile = pl.BlockSpec((tl, D_MODEL), lambda b, t: (b * nt + t, 0))
    full2 = lambda shp: pl.BlockSpec(shp, lambda b, t: (0, 0))
    full3 = lambda shp: pl.BlockSpec(shp, lambda b, t: (0, 0, 0))
    return pl.pallas_call(
        functools.partial(_lru_kernel, tl=tl),
        grid=(nb, nt),
        in_specs=[tile, tile,
                  full2((4, D_MODEL)), full2((1, D_MODEL)),
                  full3((LRU_BLOCKS, LRU_BLOCK, LRU_BLOCK)), full2((1, D_MODEL)),
                  full3((LRU_BLOCKS, LRU_BLOCK, LRU_BLOCK)), full2((1, D_MODEL)),
                  full2((1, D_MODEL)),
                  full2((SUBLANES, D_MODEL)), full2((SUBLANES, D_MODEL))],
        out_specs=[tile, full2((SUBLANES, D_MODEL)), full2((SUBLANES, D_MODEL))],
        out_shape=[jax.ShapeDtypeStruct((rows, D_MODEL), BF16),
                   jax.ShapeDtypeStruct((SUBLANES, D_MODEL), F32),
                   jax.ShapeDtypeStruct((SUBLANES, D_MODEL), F32)],
        scratch_shapes=[pltpu.VMEM((tl + SUBLANES, D_MODEL), F32),
                        pltpu.VMEM((tl, D_MODEL), F32),
                        pltpu.VMEM((tl, D_MODEL), F32),
                        pltpu.VMEM((SUBLANES, D_MODEL), F32)],
        compiler_params=_cparams(("arbitrary", "arbitrary")),
        name="rg_lru",
    )(gate, rec, p["conv_w"], p["conv_b"], p["gate_a_w"], p["gate_a_b"], p["gate_x_w"],
      p["gate_x_b"], p["softplus"], conv0, h0)


SSM_CHUNK = 256


def _ssm_kernel(u_ref, bblk_ref, cblk_ref, d_ref, gw_ref, gb_ref, pw_ref, h0_ref,
                out_ref, h_t_ref, xs, hcar, *, tl):
    @pl.when(pl.program_id(1) == 0)
    def _():
        hcar[...] = h0_ref[...]

    u = u_ref[...]
    xs[...] = _dot(u, bblk_ref[...])

    for c in range(SSM_LANES // SSM_CHUNK):
        re = slice(c * SSM_CHUNK, (c + 1) * SSM_CHUNK)
        im = slice(SSM_LANES + c * SSM_CHUNK, SSM_LANES + (c + 1) * SSM_CHUNK)
        tabs = [(pw_ref[k * SUBLANES:(k + 1) * SUBLANES, re], pw_ref[k * SUBLANES:(k + 1) * SUBLANES, im])
                for k in range(4)]

        def body(gidx, carry, re=re, im=im, tabs=tabs):
            cr, ci = carry
            r0 = pl.multiple_of(gidx * SUBLANES, SUBLANES)
            hr = xs[pl.ds(r0, SUBLANES), re]
            hi = xs[pl.ds(r0, SUBLANES), im]
            for k, s in enumerate((1, 2, 4)):
                pr, pi = tabs[k]
                sr = pltpu.roll(hr, s, 0)
                si = pltpu.roll(hi, s, 0)
                hr, hi = hr + (pr * sr - pi * si), hi + (pr * si + pi * sr)
            pr, pi = tabs[3]
            hr, hi = hr + (pr * cr - pi * ci), hi + (pr * ci + pi * cr)
            xs[pl.ds(r0, SUBLANES), re] = hr
            xs[pl.ds(r0, SUBLANES), im] = hi
            last = slice(SUBLANES - 1, SUBLANES)
            return (jnp.broadcast_to(hr[last, :], (SUBLANES, SSM_CHUNK)),
                    jnp.broadcast_to(hi[last, :], (SUBLANES, SSM_CHUNK)))

        cr, ci = lax.fori_loop(0, tl // SUBLANES, body, (hcar[:, re], hcar[:, im]))
        hcar[:, re] = cr
        hcar[:, im] = ci

    y = _dot(xs[...].astype(BF16), cblk_ref[...]) + d_ref[...] * u.astype(F32)
    g = _gelu_tanh(y)
    out_ref[...] = (g * jax.nn.sigmoid(_dot(g.astype(BF16), gw_ref[...]) + gb_ref[...])).astype(out_ref.dtype)
    h_t_ref[...] = hcar[...]


def _ssm(u, p, h0, nb, tl):
    rows = u.shape[0]
    nt = rows // (nb * tl)
    tile = pl.BlockSpec((tl, SSM_WIDTH), lambda b, t: (b * nt + t, 0))
    full2 = lambda shp: pl.BlockSpec(shp, lambda b, t: (0, 0))
    return pl.pallas_call(
        functools.partial(_ssm_kernel, tl=tl),
        grid=(nb, nt),
        in_specs=[tile,
                  _resident((SSM_WIDTH, 2 * SSM_LANES)), _resident((2 * SSM_LANES, SSM_WIDTH)),
                  full2((1, SSM_WIDTH)), full2((SSM_WIDTH, SSM_WIDTH)), full2((1, SSM_WIDTH)),
                  full2((4 * SUBLANES, 2 * SSM_LANES)), full2((SUBLANES, 2 * SSM_LANES))],
        out_specs=[tile, full2((SUBLANES, 2 * SSM_LANES))],
        out_shape=[jax.ShapeDtypeStruct((rows, SSM_WIDTH), BF16),
                   jax.ShapeDtypeStruct((SUBLANES, 2 * SSM_LANES), F32)],
        scratch_shapes=[pltpu.VMEM((tl, 2 * SSM_LANES), F32),
                        pltpu.VMEM((SUBLANES, 2 * SSM_LANES), F32)],
        compiler_params=_cparams(("arbitrary", "arbitrary")),
        name="s5_ssm",
    )(u, p["bblk"], p["cblk"], p["d"], p["glu_w"], p["glu_b"], p["powers"], h0)


FFN_CHUNK = 512


def _resident(shape):
    return pl.BlockSpec(shape, lambda *_: (0,) * len(shape), pipeline_mode=pl.Buffered(1))


def _mix_ffn_kernel(a_ref, b_ref, wa_ref, wb_ref, h_ref, g_ref, wg_ref, wu_ref, wd_ref, o_ref):
    h = h_ref[...] + _dot(a_ref[...], wa_ref[...]) + _dot(b_ref[...], wb_ref[...])
    o_ref[...] = h
    xn = _rms(h, g_ref[...]).astype(BF16)
    for c in range(wg_ref.shape[1] // FFN_CHUNK):
        sl = slice(c * FFN_CHUNK, (c + 1) * FFN_CHUNK)
        a = (_silu(_dot(xn, wg_ref[:, sl])) * _dot(xn, wu_ref[:, sl])).astype(BF16)
        o_ref[...] += _dot(a, wd_ref[sl, :])


def _mix_ffn(a, b, wa, wb, h, gain, wg, wu, wd, tm):
    rows = h.shape[0]
    row_spec = lambda c: pl.BlockSpec((tm, c), lambda i: (i, 0))
    return pl.pallas_call(
        _mix_ffn_kernel,
        grid=(rows // tm,),
        in_specs=[row_spec(a.shape[1]), row_spec(b.shape[1]), _resident(wa.shape), _resident(wb.shape),
                  row_spec(D_MODEL), _resident((1, D_MODEL)),
                  _resident(wg.shape), _resident(wu.shape), _resident(wd.shape)],
        out_specs=row_spec(D_MODEL),
        out_shape=jax.ShapeDtypeStruct((rows, D_MODEL), F32),
        compiler_params=_cparams(("parallel",)),
        name="mix_ffn",
    )(a, b, wa, wb, h, gain, wg, wu, wd)


def _gla_kernel(q_ref, k_ref, v_ref, og_ref, lg_ref, hn_ref, s0_ref, o_ref, s_t_ref, st, *, tl, ck):
    @pl.when(pl.program_id(1) == 0)
    def _():
        st[...] = s0_ref[...]

    ri = lax.broadcasted_iota(jnp.int32, (ck, ck), 0)
    ci = lax.broadcasted_iota(jnp.int32, (ck, ck), 1)
    causal = ri >= ci
    tri = causal.astype(F32).astype(BF16)
    scale = GLA_DK ** -0.5

    for c in range(tl // ck):
        rows = slice(c * ck, (c + 1) * ck)
        lg_hi, lg_lo = _split_bf16(lg_ref[rows, :])
        b_all = _dot(tri, lg_hi) + _dot(tri, lg_lo)
        for h in range(GLA_HEADS):
            ks = slice(h * GLA_DK, (h + 1) * GLA_DK)
            vs = slice(h * GLA_DV, (h + 1) * GLA_DV)
            b = b_all[:, ks]
            b_last = b[ck - 1:ck, :]
            q = q_ref[rows, ks].astype(F32)
            k = k_ref[rows, ks].astype(F32)
            v = v_ref[rows, vs]
            qd = (q * (scale * jnp.exp(b))).astype(BF16)
            kd = (k * jnp.exp(-b)).astype(BF16)
            kl = (k * jnp.exp(b_last - b)).astype(BF16)
            att = lax.dot_general(qd, kd, (((1,), (1,)), ((), ())), preferred_element_type=F32)
            att = jnp.where(causal, att, 0.0).astype(BF16)
            s_h = st[h]
            o = _dot(att, v) + lax.dot_general(qd, s_h.astype(BF16), (((1,), (1,)), ((), ())),
                                               preferred_element_type=F32)
            st[h] = s_h * jnp.exp(b_last) + lax.dot_general(v, kl, (((0,), (0,)), ((), ())),
                                                            preferred_element_type=F32)
            o = o * lax.rsqrt(jnp.mean(o * o, axis=-1, keepdims=True) + EPS) * hn_ref[:, vs]
            o_ref[rows, vs] = (o * _silu(og_ref[rows, vs].astype(F32))).astype(o_ref.dtype)
    s_t_ref[...] = st[...]


def _gla(q, k, v, og, lg, head_norm, s0, nb, tl, ck):
    rows = q.shape[0]
    nt = rows // (nb * tl)
    tile = lambda c: pl.BlockSpec((tl, c), lambda b, t: (b * nt + t, 0))
    state = pl.BlockSpec((GLA_HEADS, GLA_DV, GLA_DK), lambda b, t: (0, 0, 0))
    return pl.pallas_call(
        functools.partial(_gla_kernel, tl=tl, ck=ck),
        grid=(nb, nt),
        in_specs=[tile(GLA_KEY), tile(GLA_KEY), tile(GLA_VAL), tile(GLA_VAL), tile(GLA_KEY),
                  pl.BlockSpec((1, GLA_VAL), lambda b, t: (0, 0)), state],
        out_specs=[tile(GLA_VAL), state],
        out_shape=[jax.ShapeDtypeStruct((rows, GLA_VAL), BF16),
                   jax.ShapeDtypeStruct((GLA_HEADS, GLA_DV, GLA_DK), F32)],
        scratch_shapes=[pltpu.VMEM((GLA_HEADS, GLA_DV, GLA_DK), F32)],
        compiler_params=_cparams(("arbitrary", "arbitrary")),
        name="gla",
    )(q, k, v, og, lg, head_norm, s0)


TOK_TILE = 512
SEG_ALIGN = 16
LOCAL_ROWS = 1152
FFN_TILE = 512
MOE_CHUNK = 1792
XS_COLS = D_MODEL + LANES
SEG_BITS = (512, 256, 128, 64, 32, 16)
INFO_E1, INFO_E2, INFO_R1, INFO_R2, INFO_W1, INFO_W2 = range(6)
assert LOCAL_ROWS >= 2 * TOK_TILE + N_EXPERTS * (SEG_ALIGN - 1) and LOCAL_ROWS % LANES == 0


def _router_kernel(o_ref, wo_ref, h_ref, g_ref, wr_ref, tri_ref, hout_ref, xn_ref, info_ref, cnt_ref):
    tm = h_ref.shape[0]
    h = h_ref[...] + _dot(o_ref[...], wo_ref[...])
    hout_ref[...] = h
    xn = _rms(h, g_ref[...])
    xn_ref[...] = xn.astype(BF16)
    x_hi, x_lo = _split_bf16(xn)
    w_hi = wr_ref[0]
    w_lo = wr_ref[1]
    logits = _dot(x_hi, w_hi) + _dot(x_lo, w_hi) + _dot(x_hi, w_lo)
    lane = lax.broadcasted_iota(jnp.int32, logits.shape, 1).astype(F32)
    neg = jnp.float32(-jnp.inf)
    logits = jnp.where(lane < N_EXPERTS, logits, neg)
    v1 = jnp.max(logits, axis=-1, keepdims=True)
    i1 = jnp.min(jnp.where(logits == v1, lane, float(LANES)), axis=-1, keepdims=True)
    rest = jnp.where(lane == i1, neg, logits)
    v2 = jnp.max(rest, axis=-1, keepdims=True)
    i2 = jnp.min(jnp.where(rest == v2, lane, float(LANES)), axis=-1, keepdims=True)
    e2 = jnp.exp(v2 - v1)
    w1 = 1.0 / (1.0 + e2)
    w2 = e2 / (1.0 + e2)
    hit1 = lane == i1
    hit2 = lane == i2
    sel = jnp.where(hit1, 1.0, 0.0) + jnp.where(hit2, 1.0, 0.0)
    cum = _dot(tri_ref[...], sel.astype(BF16))
    rank1 = jnp.sum(jnp.where(hit1, cum, 0.0), axis=-1, keepdims=True) - 1.0
    rank2 = jnp.sum(jnp.where(hit2, cum, 0.0), axis=-1, keepdims=True) - 1.0
    info = jnp.zeros_like(logits)
    for col, val in ((INFO_E1, i1), (INFO_E2, i2), (INFO_R1, rank1), (INFO_R2, rank2),
                     (INFO_W1, w1), (INFO_W2, w2)):
        info = jnp.where(lane == col, val, info)
    info_ref[...] = info
    cnt_ref[...] = jnp.broadcast_to(cum[tm - 1:tm, :], (SUBLANES, LANES))


def _router(o, wo, h, gain, wr, tri):
    rows = h.shape[0]
    tm = TOK_TILE
    nt = rows // tm
    row_spec = lambda c: pl.BlockSpec((tm, c), lambda i: (i, 0))
    return pl.pallas_call(
        _router_kernel,
        grid=(nt,),
        in_specs=[row_spec(o.shape[1]), _resident(wo.shape), row_spec(D_MODEL), _resident((1, D_MODEL)),
                  _resident((2, D_MODEL, LANES)), _resident((tm, tm))],
        out_specs=[row_spec(D_MODEL), row_spec(D_MODEL), row_spec(LANES),
                   pl.BlockSpec((SUBLANES, LANES), lambda i: (i, 0))],
        out_shape=[jax.ShapeDtypeStruct((rows, D_MODEL), F32),
                   jax.ShapeDtypeStruct((rows, D_MODEL), BF16),
                   jax.ShapeDtypeStruct((rows, LANES), F32),
                   jax.ShapeDtypeStruct((nt * SUBLANES, LANES), F32)],
        compiler_params=_cparams(("parallel",)),
        name="router",
    )(o, wo, h, gain, wr, tri)


def _segment_copies(tile, loff_ref, slen_ref, goff_ref, make_copy, fn):
    for e in range(N_EXPERTS):
        idx = tile * N_EXPERTS + e
        lo = loff_ref[idx]
        n = slen_ref[idx]
        go = goff_ref[idx]
        for bit in SEG_BITS:
            pos = n & (-2 * bit)

            @pl.when((n & bit) != 0)
            def _(lo=lo, go=go, pos=pos, bit=bit):
                fn(make_copy(pl.multiple_of(lo + pos, SEG_ALIGN), pl.multiple_of(go + pos, SEG_ALIGN), bit))


def _local_rows(info_row, loff_ref, tile, which_e, which_r):
    e = info_row(which_e)
    loc = info_row(which_r)
    for ex in range(N_EXPERTS):
        loc = loc + jnp.where(e == float(ex), loff_ref[tile * N_EXPERTS + ex].astype(F32), 0.0)
    return loc


def _dispatch_kernel(loff_ref, slen_ref, goff_ref, xn_ref, info_ref, zero_ref, xs_ref, buf, sem):
    del zero_ref
    i = pl.program_id(0)
    n = pl.num_programs(0)
    slot = i % 2

    def copy_for(s):
        def make(lo, go, rows):
            return pltpu.make_async_copy(buf.at[s, pl.ds(lo, rows), :], xs_ref.at[pl.ds(go, rows), :],
                                         sem.at[s])
        return make

    wait = lambda c: c.wait()
    start = lambda c: c.start()

    @pl.when(i >= 2)
    def _():
        _segment_copies(i - 2, loff_ref, slen_ref, goff_ref, copy_for(slot), wait)

    info_t = info_ref[...].T
    row_of = lambda r: info_t[r:r + 1, :]
    loc1 = _local_rows(row_of, loff_ref, i, INFO_E1, INFO_R1)
    loc2 = _local_rows(row_of, loff_ref, i, INFO_E2, INFO_R2)
    ridx = lax.broadcasted_iota(jnp.int32, (LOCAL_ROWS, TOK_TILE), 0).astype(F32)
    p1 = jnp.where(ridx == loc1, 1.0, 0.0).astype(BF16)
    p2 = jnp.where(ridx == loc2, 1.0, 0.0).astype(BF16)
    info = info_ref[...]
    lane = lax.broadcasted_iota(jnp.int32, info.shape, 1)

    def weight_parts(col):
        w = info[:, col:col + 1]
        hi = w.astype(BF16).astype(F32)
        mid = (w - hi).astype(BF16).astype(F32)
        lo = w - hi - mid
        return jnp.where(lane == 0, hi, jnp.where(lane == 1, mid, jnp.where(lane == 2, lo, 0.0))).astype(BF16)

    buf[slot, :, 0:D_MODEL] = _dot(p1 + p2, xn_ref[...]).astype(BF16)
    buf[slot, :, D_MODEL:XS_COLS] = (_dot(p1, weight_parts(INFO_W1)) + _dot(p2, weight_parts(INFO_W2))).astype(BF16)
    _segment_copies(i, loff_ref, slen_ref, goff_ref, copy_for(slot), start)

    @pl.when(i == n - 1)
    def _():
        _segment_copies(i, loff_ref, slen_ref, goff_ref, copy_for(slot), wait)

        @pl.when(i >= 1)
        def _():
            _segment_copies(i - 1, loff_ref, slen_ref, goff_ref, copy_for(1 - slot), wait)


def _dispatch(loff, slen, goff, xn, info, n_sort):
    rows = xn.shape[0]
    nt = rows // TOK_TILE
    zeros = jnp.zeros((n_sort, XS_COLS), BF16)
    return pl.pallas_call(
        _dispatch_kernel,
        grid_spec=pltpu.PrefetchScalarGridSpec(
            num_scalar_prefetch=3,
            grid=(nt,),
            in_specs=[pl.BlockSpec((TOK_TILE, D_MODEL), lambda i, *_: (i, 0)),
                      pl.BlockSpec((TOK_TILE, LANES), lambda i, *_: (i, 0)),
                      pl.BlockSpec(memory_space=pl.ANY)],
            out_specs=pl.BlockSpec(memory_space=pl.ANY),
            scratch_shapes=[pltpu.VMEM((2, LOCAL_ROWS, XS_COLS), BF16),
                            pltpu.SemaphoreType.DMA((2,))],
        ),
        out_shape=jax.ShapeDtypeStruct((n_sort, XS_COLS), BF16),
        input_output_aliases={5: 0},
        compiler_params=_cparams(("arbitrary",)),
        name="moe_dispatch",
    )(loff, slen, goff, xn, info, zeros)


def _moe_ffn_kernel(te_ref, tot_ref, xs_ref, wg_ref, wu_ref, wd_ref, y_ref, w_scr, acc_ref):
    del te_ref
    j = pl.program_id(0)
    f = pl.program_id(1)

    @pl.when(j < tot_ref[0])
    def _():
        @pl.when(f == 0)
        def _():
            w = jnp.sum(xs_ref[:, D_MODEL:XS_COLS].astype(F32), axis=-1, keepdims=True)
            w_scr[...] = jnp.broadcast_to(w, (FFN_TILE, LANES))
            acc_ref[...] = jnp.zeros_like(acc_ref)

        x = xs_ref[:, 0:D_MODEL]
        a = _silu(_dot(x, wg_ref[0])) * _dot(x, wu_ref[0]) * w_scr[:, 0:1]
        acc_ref[...] += _dot(a.astype(BF16), wd_ref[0])

        @pl.when(f == pl.num_programs(1) - 1)
        def _():
            y_ref[...] = acc_ref[...].astype(y_ref.dtype)

    @pl.when((j >= tot_ref[0]) & (f == pl.num_programs(1) - 1))
    def _():
        y_ref[...] = jnp.zeros_like(y_ref)


def _moe_ffn(tile_e, total, xs, wg, wu, wd, fc):
    n_sort = xs.shape[0]
    nt = n_sort // FFN_TILE
    nf = wg.shape[2] // fc
    tile_of = lambda j, tot: jnp.minimum(j, tot[0] - 1)
    chunk_of = lambda j, f, tot: jnp.where(j < tot[0], f, nf - 1)
    return pl.pallas_call(
        _moe_ffn_kernel,
        grid_spec=pltpu.PrefetchScalarGridSpec(
            num_scalar_prefetch=2,
            grid=(nt, nf),
            in_specs=[pl.BlockSpec((FFN_TILE, XS_COLS), lambda j, f, te, tot: (tile_of(j, tot), 0)),
                      pl.BlockSpec((1, D_MODEL, fc), lambda j, f, te, tot: (te[j], 0, chunk_of(j, f, tot))),
                      pl.BlockSpec((1, D_MODEL, fc), lambda j, f, te, tot: (te[j], 0, chunk_of(j, f, tot))),
                      pl.BlockSpec((1, fc, D_MODEL), lambda j, f, te, tot: (te[j], chunk_of(j, f, tot), 0))],
            out_specs=pl.BlockSpec((FFN_TILE, D_MODEL), lambda j, f, te, tot: (j, 0)),
            scratch_shapes=[pltpu.VMEM((FFN_TILE, LANES), F32),
                            pltpu.VMEM((FFN_TILE, D_MODEL), F32)],
        ),
        out_shape=jax.ShapeDtypeStruct((n_sort, D_MODEL), BF16),
        compiler_params=_cparams(("arbitrary", "arbitrary")),
        name="moe_ffn",
    )(tile_e, total, xs, wg, wu, wd)


def _combine_kernel(loff_ref, slen_ref, goff_ref, info_ref, h_ref, fn_ref, y_ref, o_ref, buf, sem):
    i = pl.program_id(0)
    n = pl.num_programs(0)
    slot = i % 2

    def copy_for(s):
        def make(lo, go, rows):
            return pltpu.make_async_copy(y_ref.at[pl.ds(go, rows), :], buf.at[s, pl.ds(lo, rows), :],
                                         sem.at[s])
        return make

    wait = lambda c: c.wait()
    start = lambda c: c.start()

    @pl.when(i == 0)
    def _():
        buf[...] = jnp.zeros_like(buf)
        _segment_copies(i, loff_ref, slen_ref, goff_ref, copy_for(slot), start)

    @pl.when(i + 1 < n)
    def _():
        _segment_copies(i + 1, loff_ref, slen_ref, goff_ref, copy_for(1 - slot), start)

    _segment_copies(i, loff_ref, slen_ref, goff_ref, copy_for(slot), wait)

    ys = buf[slot]
    info = info_ref[...]
    col_of = lambda c: info[:, c:c + 1]
    loc1 = _local_rows(col_of, loff_ref, i, INFO_E1, INFO_R1)
    loc2 = _local_rows(col_of, loff_ref, i, INFO_E2, INFO_R2)
    cidx = lax.broadcasted_iota(jnp.int32, (TOK_TILE, LOCAL_ROWS), 1).astype(F32)
    pt = (jnp.where(cidx == loc1, 1.0, 0.0) + jnp.where(cidx == loc2, 1.0, 0.0)).astype(BF16)
    o_ref[...] = _rms(h_ref[...] + _dot(pt, ys), fn_ref[...])


def _combine(loff, slen, goff, info, h, final_norm, y):
    rows = h.shape[0]
    nt = rows // TOK_TILE
    return pl.pallas_call(
        _combine_kernel,
        grid_spec=pltpu.PrefetchScalarGridSpec(
            num_scalar_prefetch=3,
            grid=(nt,),
            in_specs=[pl.BlockSpec((TOK_TILE, LANES), lambda i, *_: (i, 0)),
                      pl.BlockSpec((TOK_TILE, D_MODEL), lambda i, *_: (i, 0)),
                      pl.BlockSpec((1, D_MODEL), lambda i, *_: (0, 0)),
                      pl.BlockSpec(memory_space=pl.ANY)],
            out_specs=pl.BlockSpec((TOK_TILE, D_MODEL), lambda i, *_: (i, 0)),
            scratch_shapes=[pltpu.VMEM((2, LOCAL_ROWS, D_MODEL), BF16),
                            pltpu.SemaphoreType.DMA((2,))],
        ),
        out_shape=jax.ShapeDtypeStruct((rows, D_MODEL), F32),
        compiler_params=_cparams(("arbitrary",)),
        name="moe_combine",
    )(loff, slen, goff, info, h, final_norm, y)


def _routing_tables(cnt, n_tiles_max):
    slen = (cnt + SEG_ALIGN - 1) // SEG_ALIGN * SEG_ALIGN
    loff = jnp.cumsum(slen, axis=1) - slen
    rows_e = jnp.sum(slen, axis=0)
    rows_pad = (rows_e + FFN_TILE - 1) // FFN_TILE * FFN_TILE
    base = jnp.cumsum(rows_pad) - rows_pad
    goff = base[None, :] + jnp.cumsum(slen, axis=0) - slen
    tile_end = jnp.cumsum(rows_pad // FFN_TILE)
    total = tile_end[-1]
    j = jnp.arange(n_tiles_max, dtype=jnp.int32)
    tile_e = jnp.sum(jnp.minimum(j, total - 1)[:, None] >= tile_end[None, :], axis=1).astype(jnp.int32)
    flat = lambda a: a.reshape(-1).astype(jnp.int32)
    return flat(loff), flat(slen), flat(goff), tile_e, total.reshape(1).astype(jnp.int32)


def _moe(o, wo, h, gain, wr, wg, wu, wd, final_norm):
    rows = h.shape[0]
    assert rows % TOK_TILE == 0
    nt = rows // TOK_TILE
    max_rows = 2 * rows + nt * N_EXPERTS * (SEG_ALIGN - 1) + N_EXPERTS * (FFN_TILE - SEG_ALIGN)
    n_tiles_max = -(-max_rows // FFN_TILE)
    tri = jnp.tril(jnp.ones((TOK_TILE, TOK_TILE), BF16))
    h, xn, info, cnt = _router(o, wo, h, gain, wr, tri)
    cnt = cnt.reshape(nt, SUBLANES, LANES)[:, 0, :N_EXPERTS].astype(jnp.int32)
    loff, slen, goff, tile_e, total = _routing_tables(cnt, n_tiles_max)
    xs = _dispatch(loff, slen, goff, xn, info, n_tiles_max * FFN_TILE)
    y = _moe_ffn(tile_e, total, xs, wg, wu, wd, MOE_CHUNK)
    return _combine(loff, slen, goff, info, h, final_norm, y)


def _ssm_params(lam_re, lam_im, log_dt, b_re, b_im, c_re, c_im, d, glu_w, glu_b):
    g, n, hh = SSM_GROUPS, SSM_STATE, SSM_GROUP
    lr, li = lam_re.astype(F32), lam_im.astype(F32)
    dt = jnp.exp(log_dt.astype(F32))[:, None]
    mag = jnp.exp(dt * lr)
    ar, ai = mag * jnp.cos(dt * li), mag * jnp.sin(dt * li)
    den = lr * lr + li * li
    nr = ar - 1.0
    zr = (nr * lr + ai * li) / den
    zi = (ai * lr - nr * li) / den
    br = zr[..., None] * b_re - zi[..., None] * b_im
    bi = zr[..., None] * b_im + zi[..., None] * b_re
    eye = jnp.eye(g, dtype=F32)
    bblk = jnp.concatenate(
        [jnp.einsum("gnh,gk->ghkn", x, eye).reshape(SSM_WIDTH, SSM_LANES) for x in (br, bi)], axis=1)
    cblk = jnp.concatenate(
        [jnp.einsum("ghn,gk->gnkh", x, eye).reshape(SSM_LANES, SSM_WIDTH) for x in (c_re, -c_im)], axis=0)

    def power(m):
        pr, pi = jnp.ones((SUBLANES, g, n), F32), jnp.zeros((SUBLANES, g, n), F32)
        for step in range(1, SUBLANES + 1):
            nr_, ni_ = pr * ar - pi * ai, pr * ai + pi * ar
            take = (m >= step)[:, :, None]
            pr, pi = jnp.where(take, nr_, pr), jnp.where(take, ni_, pi)
        return pr.reshape(SUBLANES, SSM_LANES), pi.reshape(SUBLANES, SSM_LANES)

    sub = jnp.arange(SUBLANES, dtype=jnp.int32)[:, None]
    tabs = []
    for s in (1, 2, 4):
        pr, pi = power(jnp.full((SUBLANES, 1), s, jnp.int32))
        keep = (sub >= s).astype(F32)
        tabs.append(jnp.concatenate([pr * keep, pi * keep], axis=1))
    pr, pi = power(sub + 1)
    tabs.append(jnp.concatenate([pr, pi], axis=1))
    return {
        "bblk": bblk.astype(BF16), "cblk": cblk.astype(BF16),
        "d": d.reshape(1, SSM_WIDTH).astype(F32),
        "glu_w": glu_w.astype(BF16), "glu_b": glu_b.reshape(1, SSM_WIDTH).astype(F32),
        "powers": jnp.concatenate(tabs, axis=0),
    }


def _row(x):
    return x.reshape(1, -1).astype(F32)


def _pad_cols(w, n):
    return jnp.pad(w, ((0, 0), (0, n - w.shape[1])))


def kernel(x, meta_tokens, ev_norm_mix, ev_w_in, ev_conv_w, ev_conv_b, ev_gate_a_w, ev_gate_a_b, ev_gate_x_w, ev_gate_x_b, ev_lru_lambda, ev_ssm_lambda_re, ev_ssm_lambda_im, ev_ssm_log_dt, ev_ssm_b_re, ev_ssm_b_im, ev_ssm_c_re, ev_ssm_c_im, ev_ssm_d, ev_ssm_glu_w, ev_ssm_glu_b, ev_w_out, ev_norm_ffn, ev_ffn_w_gate, ev_ffn_w_up, ev_ffn_w_down, od_norm_mix, od_w_in, od_gla_gate_w2, od_gla_gate_b, od_gla_norm, od_w_out, od_norm_ffn, od_router_w, od_moe_w_gate, od_moe_w_up, od_moe_w_down, final_norm):
    nb, seq, _ = x.shape
    rows = nb * seq
    assert ev_w_in.shape[0] == 1 and od_w_in.shape[0] == 1, "two-layer trunk only"

    w_in0 = ev_w_in[0].astype(BF16)
    lru_p = {
        "conv_w": ev_conv_w[0].astype(F32), "conv_b": _row(ev_conv_b[0]),
        "gate_a_w": ev_gate_a_w[0].astype(BF16), "gate_a_b": _row(ev_gate_a_b[0]),
        "gate_x_w": ev_gate_x_w[0].astype(BF16), "gate_x_b": _row(ev_gate_x_b[0]),
        "softplus": _row(jax.nn.softplus(-ev_lru_lambda[0].astype(F32))),
    }
    ssm_p = _ssm_params(ev_ssm_lambda_re[0], ev_ssm_lambda_im[0], ev_ssm_log_dt[0], ev_ssm_b_re[0],
                        ev_ssm_b_im[0], ev_ssm_c_re[0], ev_ssm_c_im[0], ev_ssm_d[0],
                        ev_ssm_glu_w[0], ev_ssm_glu_b[0])
    w_out0 = ev_w_out[0].astype(BF16)
    w_out0_a, w_out0_b = w_out0[:D_MODEL], w_out0[D_MODEL:]
    ffn_g, ffn_u, ffn_d = (w[0].astype(BF16) for w in (ev_ffn_w_gate, ev_ffn_w_up, ev_ffn_w_down))
    odd_in = 2 * GLA_KEY + 2 * GLA_VAL
    w_in1 = _pad_cols(od_w_in[0], odd_in + LANES).astype(BF16)
    w2 = jnp.pad(od_gla_gate_w2[0].astype(F32), ((0, LANES - GLA_RANK), (0, 0)))
    w2_hi = w2.astype(BF16)
    w2_split = jnp.stack([w2_hi, (w2 - w2_hi.astype(F32)).astype(BF16)])
    wr = _pad_cols(od_router_w[0].astype(F32), LANES)
    wr_hi = wr.astype(BF16)
    wr_split = jnp.stack([wr_hi, (wr - wr_hi.astype(F32)).astype(BF16)])
    w_out1 = od_w_out[0].astype(BF16)
    moe_g, moe_u, moe_d = (w[0].astype(BF16) for w in (od_moe_w_gate, od_moe_w_up, od_moe_w_down))

    tl = min(512, seq)
    tm = min(1024, rows)
    tm_ffn = min(512, rows)
    ck = min(128, tl)
    even_splits = (D_MODEL, D_MODEL, SSM_WIDTH)

    def even_mixer(h, nbatch, tile_m, tile_f, tile_l, conv0, h0, s0):
        gate, rec, u = _norm_proj(h, _row(ev_norm_mix[0]), w_in0, even_splits, tile_m)
        a_out, conv_t, h_t = _lru(gate, rec, lru_p, conv0, h0, nbatch, tile_l)
        b_out, s_t = _ssm(u, ssm_p, s0, nbatch, tile_l)
        h = _mix_ffn(a_out, b_out, w_out0_a, w_out0_b, h, _row(ev_norm_ffn[0]), ffn_g, ffn_u, ffn_d, tile_f)
        return h, (conv_t, h_t, s_t)

    def gla_inputs(h, tile_m):
        return _gla_proj(h, _row(od_norm_mix[0]), w_in1, w2_split, _row(od_gla_gate_b[0]), tile_m)

    zeros = lambda *s: jnp.zeros(s, F32)
    hm = meta_tokens.astype(F32)
    hm, (conv_m, h_m, s_m) = even_mixer(hm, 1, N_META, N_META, N_META, zeros(SUBLANES, D_MODEL),
                                        zeros(SUBLANES, D_MODEL), zeros(SUBLANES, 2 * SSM_LANES))
    qm, km, vm, ogm, lgm = gla_inputs(hm, N_META)
    _, gla_s = _gla(qm, km, vm, ogm, lgm, _row(od_gla_norm[0]),
                    zeros(GLA_HEADS, GLA_DV, GLA_DK), 1, N_META, N_META)

    h = x.reshape(rows, D_MODEL).astype(F32)
    h, _ = even_mixer(h, nb, tm, tm_ffn, tl, conv_m, h_m, s_m)
    q, k, v, og, lg = gla_inputs(h, tm)
    o, _ = _gla(q, k, v, og, lg, _row(od_gla_norm[0]), gla_s, nb, tl, ck)
    out = _moe(o, w_out1, h, _row(od_norm_ffn[0]), wr_split, moe_g, moe_u, moe_d, _row(final_norm))
    return out.reshape(nb, seq, D_MODEL)
```

```python
import functools
import math

import jax
import jax.numpy as jnp
from jax import lax
from jax.experimental import pallas as pl
from jax.experimental.pallas import tpu as pltpu

F32 = jnp.float32
BF16 = jnp.bfloat16

D_MODEL = 1024
N_META = 16
EPS = 1e-6
LRU_BLOCKS = 4
LRU_BLOCK = 256
LRU_C = 8.0
SSM_WIDTH = 512
SSM_GROUP = 16
SSM_GROUPS = 32
SSM_STATE = 64
SSM_LANES = SSM_GROUPS * SSM_STATE
D_FF = 3072
GLA_HEADS = 4
GLA_DK = 128
GLA_DV = 256
GLA_KEY = 512
GLA_VAL = 1024
GLA_RANK = 16
GLA_TAU = 16.0
N_EXPERTS = 8
D_FF_EXPERT = 3584

SUBLANES = 8
LANES = 128
VMEM_LIMIT = 52 * 1024 * 1024


def _cparams(sem):
    return pltpu.CompilerParams(dimension_semantics=sem, vmem_limit_bytes=VMEM_LIMIT)


def _dot(a, b):
    return jnp.dot(a, b, preferred_element_type=F32)


def _rms(x, gain):
    ms = jnp.mean(x * x, axis=-1, keepdims=True)
    return x * lax.rsqrt(ms + EPS) * gain


def _gelu_tanh(x):
    c = math.sqrt(2.0 / math.pi)
    return 0.5 * x * (1.0 + jnp.tanh(c * (x + 0.044715 * (x * x * x))))


def _silu(x):
    return x * jax.nn.sigmoid(x)


def _split_bf16(x):
    hi = x.astype(BF16)
    lo = (x - hi.astype(F32)).astype(BF16)
    return hi, lo


def _norm_proj_kernel(x_ref, g_ref, w_ref, *out_refs):
    xn = _rms(x_ref[...], g_ref[...]).astype(BF16)
    off = 0
    for o_ref in out_refs:
        n = o_ref.shape[-1]
        o_ref[...] = _dot(xn, w_ref[:, off:off + n]).astype(o_ref.dtype)
        off += n


def _norm_proj(x, gain, w, splits, tm):
    rows = x.shape[0]
    n = w.shape[1]
    assert sum(splits) == n and rows % tm == 0
    return pl.pallas_call(
        _norm_proj_kernel,
        grid=(rows // tm,),
        in_specs=[
            pl.BlockSpec((tm, D_MODEL), lambda i: (i, 0)),
            pl.BlockSpec((1, D_MODEL), lambda i: (0, 0)),
            _resident((D_MODEL, n)),
        ],
        out_specs=[pl.BlockSpec((tm, s), lambda i: (i, 0)) for s in splits],
        out_shape=[jax.ShapeDtypeStruct((rows, s), BF16) for s in splits],
        compiler_params=_cparams(("parallel",)),
        name="norm_proj",
    )(x, gain, w)


def _gla_proj_kernel(x_ref, g_ref, w_ref, w2_ref, b2_ref, q_ref, k_ref, v_ref, og_ref, lg_ref):
    xn = _rms(x_ref[...], g_ref[...]).astype(BF16)
    q_ref[...] = _dot(xn, w_ref[:, 0:GLA_KEY]).astype(BF16)
    k_ref[...] = _dot(xn, w_ref[:, GLA_KEY:2 * GLA_KEY]).astype(BF16)
    v_ref[...] = _dot(xn, w_ref[:, 2 * GLA_KEY:2 * GLA_KEY + GLA_VAL]).astype(BF16)
    o0 = 2 * GLA_KEY + GLA_VAL
    og_ref[...] = _dot(xn, w_ref[:, o0:o0 + GLA_VAL]).astype(BF16)
    glr = _dot(xn, w_ref[:, o0 + GLA_VAL:o0 + GLA_VAL + LANES])
    glr_hi, glr_lo = _split_bf16(glr)
    w2_hi = w2_ref[0]
    w2_lo = w2_ref[1]
    pre = _dot(glr_hi, w2_hi) + _dot(glr_lo, w2_hi) + _dot(glr_hi, w2_lo) + b2_ref[...]
    lg_ref[...] = (jnp.minimum(pre, 0.0) - jnp.log1p(jnp.exp(-jnp.abs(pre)))) * (1.0 / GLA_TAU)


def _gla_proj(x, gain, w, w2, b2, tm):
    rows = x.shape[0]
    n = w.shape[1]
    row_spec = lambda c: pl.BlockSpec((tm, c), lambda i: (i, 0))
    return pl.pallas_call(
        _gla_proj_kernel,
        grid=(rows // tm,),
        in_specs=[
            row_spec(D_MODEL),
            pl.BlockSpec((1, D_MODEL), lambda i: (0, 0)),
            _resident((D_MODEL, n)),
            pl.BlockSpec((2, LANES, GLA_KEY), lambda i: (0, 0, 0)),
            pl.BlockSpec((1, GLA_KEY), lambda i: (0, 0)),
        ],
        out_specs=[row_spec(GLA_KEY), row_spec(GLA_KEY), row_spec(GLA_VAL), row_spec(GLA_VAL),
                   row_spec(GLA_KEY)],
        out_shape=[jax.ShapeDtypeStruct((rows, GLA_KEY), BF16),
                   jax.ShapeDtypeStruct((rows, GLA_KEY), BF16),
                   jax.ShapeDtypeStruct((rows, GLA_VAL), BF16),
                   jax.ShapeDtypeStruct((rows, GLA_VAL), BF16),
                   jax.ShapeDtypeStruct((rows, GLA_KEY), F32)],
        compiler_params=_cparams(("parallel",)),
        name="gla_proj",
    )(x, gain, w, w2, b2)


def _lru_kernel(gate_ref, rec_ref, cw_ref, cb_ref, wa_ref, ba_ref, wx_ref, bx_ref, sp_ref,
                conv0_ref, h0_ref, out_ref, conv_t_ref, h_t_ref, xbuf, abuf, bbuf, hcar, *, tl):
    @pl.when(pl.program_id(1) == 0)
    def _():
        xbuf[0:SUBLANES, :] = conv0_ref[...]
        hcar[...] = h0_ref[...]

    xbuf[SUBLANES:SUBLANES + tl, :] = rec_ref[...].astype(F32)
    xc = cb_ref[...]
    for tap in range(4):
        xc = xc + cw_ref[tap:tap + 1, :] * xbuf[SUBLANES - 3 + tap:SUBLANES - 3 + tap + tl, :]
    xbuf[0:SUBLANES, :] = xbuf[tl:tl + SUBLANES, :]

    for blk in range(LRU_BLOCKS):
        sl = slice(blk * LRU_BLOCK, (blk + 1) * LRU_BLOCK)
        xs = xc[:, sl]
        xb = xs.astype(BF16)
        r = jax.nn.sigmoid(_dot(xb, wa_ref[blk]) + ba_ref[:, sl])
        i = jax.nn.sigmoid(_dot(xb, wx_ref[blk]) + bx_ref[:, sl])
        a = jnp.exp(-LRU_C * r * sp_ref[:, sl])
        abuf[:, sl] = a
        bbuf[:, sl] = jnp.sqrt(1.0 - a * a) * (i * xs)

    row = lax.broadcasted_iota(jnp.int32, (SUBLANES, D_MODEL), 0)

    def body(gidx, carry):
        r0 = pl.multiple_of(gidx * SUBLANES, SUBLANES)
        a = abuf[pl.ds(r0, SUBLANES), :]
        b = bbuf[pl.ds(r0, SUBLANES), :]
        for s in (1, 2, 4):
            m = row >= s
            a_sh = pltpu.roll(a, s, 0)
            b_sh = pltpu.roll(b, s, 0)
            b = jnp.where(m, a * b_sh + b, b)
            a = jnp.where(m, a * a_sh, a)
        h = a * carry + b
        bbuf[pl.ds(r0, SUBLANES), :] = h
        return jnp.broadcast_to(h[SUBLANES - 1:SUBLANES, :], (SUBLANES, D_MODEL))

    carry = lax.fori_loop(0, tl // SUBLANES, body, hcar[...])
    hcar[...] = carry
    out_ref[...] = (_gelu_tanh(gate_ref[...].astype(F32)) * bbuf[...]).astype(out_ref.dtype)
    conv_t_ref[...] = xbuf[0:SUBLANES, :]
    h_t_ref[...] = carry


def _lru(gate, rec, p, conv0, h0, nb, tl):
    rows = gate.shape[0]
    nt = rows // (nb * tl)
    tile = pl.BlockSpec((tl, D_MODEL), lambda b, t: (b * nt + t, 0))
    full2 = lambda shp: pl.BlockSpec(shp, lambda b, t: (0, 0))
    full3 = lambda shp: pl.BlockSpec(shp, lambda b, t: (0, 0, 0))
    return pl.pallas_call(
        functools.partial(_lru_kernel, tl=tl),
        grid=(nb, nt),
        in_specs=[tile, tile,
                  full2((4, D_MODEL)), full2((1, D_MODEL)),
                  full3((LRU_BLOCKS, LRU_BLOCK, LRU_BLOCK)), full2((1, D_MODEL)),
                  full3((LRU_BLOCKS, LRU_BLOCK, LRU_BLOCK)), full2((1, D_MODEL)),
                  full2((1, D_MODEL)),
                  full2((SUBLANES, D_MODEL)), full2((SUBLANES, D_MODEL))],
        out_specs=[tile, full2((SUBLANES, D_MODEL)), full2((SUBLANES, D_MODEL))],
        out_shape=[jax.ShapeDtypeStruct((rows, D_MODEL), BF16),
                   jax.ShapeDtypeStruct((SUBLANES, D_MODEL), F32),
                   jax.ShapeDtypeStruct((SUBLANES, D_MODEL), F32)],
        scratch_shapes=[pltpu.VMEM((tl + SUBLANES, D_MODEL), F32),
                        pltpu.VMEM((tl, D_MODEL), F32),
                        pltpu.VMEM((tl, D_MODEL), F32),
                        pltpu.VMEM((SUBLANES, D_MODEL), F32)],
        compiler_params=_cparams(("arbitrary", "arbitrary")),
        name="rg_lru",
    )(gate, rec, p["conv_w"], p["conv_b"], p["gate_a_w"], p["gate_a_b"], p["gate_x_w"],
      p["gate_x_b"], p["softplus"], conv0, h0)


SSM_CHUNK = 256


def _complex_scan(xs, pw_ref, hcar, n_rows, lane_pairs, prev=None):
    row = lax.broadcasted_iota(jnp.int32, (SUBLANES, SSM_CHUNK), 0)
    for re0, im0 in lane_pairs:
        re = slice(re0, re0 + SSM_CHUNK)
        im = slice(im0, im0 + SSM_CHUNK)
        tabs = [(pw_ref[k * SUBLANES:(k + 1) * SUBLANES, re], pw_ref[k * SUBLANES:(k + 1) * SUBLANES, im])
                for k in range(4)]

        def body(gidx, carry, re=re, im=im, tabs=tabs):
            cr, ci = carry
            r0 = pl.multiple_of(gidx * SUBLANES, SUBLANES)
            hr = xs[pl.ds(r0, SUBLANES), re]
            hi = xs[pl.ds(r0, SUBLANES), im]
            for k, s in enumerate((1, 2, 4)):
                pr, pi = tabs[k]
                sr = pltpu.roll(hr, s, 0)
                si = pltpu.roll(hi, s, 0)
                hr, hi = hr + (pr * sr - pi * si), hi + (pr * si + pi * sr)
            pr, pi = tabs[3]
            hr, hi = hr + (pr * cr - pi * ci), hi + (pr * ci + pi * cr)
            xs[pl.ds(r0, SUBLANES), re] = hr
            xs[pl.ds(r0, SUBLANES), im] = hi
            if prev is not None:
                prev[pl.ds(r0, SUBLANES), re] = jnp.where(row == 0, cr, pltpu.roll(hr, 1, 0))
                prev[pl.ds(r0, SUBLANES), im] = jnp.where(row == 0, ci, pltpu.roll(hi, 1, 0))
            last = slice(SUBLANES - 1, SUBLANES)
            return (jnp.broadcast_to(hr[last, :], (SUBLANES, SSM_CHUNK)),
                    jnp.broadcast_to(hi[last, :], (SUBLANES, SSM_CHUNK)))

        cr, ci = lax.fori_loop(0, n_rows // SUBLANES, body, (hcar[:, re], hcar[:, im]))
        hcar[:, re] = cr
        hcar[:, im] = ci


def _ssm_kernel(u_ref, bblk_ref, cblk_ref, d_ref, gw_ref, gb_ref, pw_ref, h0_ref,
                out_ref, h_t_ref, xs, hcar, *, tl):
    @pl.when(pl.program_id(1) == 0)
    def _():
        hcar[...] = h0_ref[...]

    u = u_ref[...]
    xs[...] = _dot(u, bblk_ref[...])
    pairs = [(c * SSM_CHUNK, SSM_LANES + c * SSM_CHUNK) for c in range(SSM_LANES // SSM_CHUNK)]
    _complex_scan(xs, pw_ref, hcar, tl, pairs)
    y = _dot(xs[...].astype(BF16), cblk_ref[...]) + d_ref[...] * u.astype(F32)
    g = _gelu_tanh(y)
    out_ref[...] = (g * jax.nn.sigmoid(_dot(g.astype(BF16), gw_ref[...]) + gb_ref[...])).astype(out_ref.dtype)
    h_t_ref[...] = hcar[...]


def _ssm(u, p, h0, nb, tl):
    rows = u.shape[0]
    nt = rows // (nb * tl)
    tile = pl.BlockSpec((tl, SSM_WIDTH), lambda b, t: (b * nt + t, 0))
    full2 = lambda shp: pl.BlockSpec(shp, lambda b, t: (0, 0))
    return pl.pallas_call(
        functools.partial(_ssm_kernel, tl=tl),
        grid=(nb, nt),
        in_specs=[tile,
                  _resident((SSM_WIDTH, 2 * SSM_LANES)), _resident((2 * SSM_LANES, SSM_WIDTH)),
                  full2((1, SSM_WIDTH)), full2((SSM_WIDTH, SSM_WIDTH)), full2((1, SSM_WIDTH)),
                  full2((4 * SUBLANES, 2 * SSM_LANES)), full2((SUBLANES, 2 * SSM_LANES))],
        out_specs=[tile, full2((SUBLANES, 2 * SSM_LANES))],
        out_shape=[jax.ShapeDtypeStruct((rows, SSM_WIDTH), BF16),
                   jax.ShapeDtypeStruct((SUBLANES, 2 * SSM_LANES), F32)],
        scratch_shapes=[pltpu.VMEM((tl, 2 * SSM_LANES), F32),
                        pltpu.VMEM((SUBLANES, 2 * SSM_LANES), F32)],
        compiler_params=_cparams(("arbitrary", "arbitrary")),
        name="s5_ssm",
    )(u, p["bblk"], p["cblk"], p["d"], p["glu_w"], p["glu_b"], p["powers"], h0)


SSM_Q = 8
SSM_PB = 256
SSM_TILES = SSM_WIDTH // LANES
SSM_VW = SSM_Q * LANES


def _ssm_chunk_kernel(u_ref, perm_ref, permt_ref, w2_ref, tz_ref, w1_ref, pw_ref, d_ref, gw_ref, gb_ref,
                      h0_ref, out_ref, h_t_ref, a_scr, xs, hp, ypm, hcar, *, tl):
    nblk = tl // SSM_PB
    ncb = SSM_PB // SSM_Q
    nc = tl // SSM_Q

    @pl.when(pl.program_id(1) == 0)
    def _():
        hcar[...] = h0_ref[...]

    for b in range(nblk):
        pu = _dot(perm_ref[...], u_ref[b * SSM_PB:(b + 1) * SSM_PB, :]).astype(BF16)
        for j in range(SSM_Q):
            a_scr[j, b * ncb:(b + 1) * ncb, :] = pu[j * ncb:(j + 1) * ncb, :]

    def u_tile(v):
        return jnp.concatenate([a_scr[j, :, v * LANES:(v + 1) * LANES] for j in range(SSM_Q)], axis=1)

    for v in range(SSM_TILES):
        xs[:, v * SSM_VW:(v + 1) * SSM_VW] = _dot(u_tile(v), w2_ref[v])

    half = SSM_VW // 2
    pairs = [(v * SSM_VW + c * SSM_CHUNK, v * SSM_VW + half + c * SSM_CHUNK)
             for v in range(SSM_TILES) for c in range(half // SSM_CHUNK)]
    _complex_scan(xs, pw_ref, hcar, nc, pairs, prev=hp)

    for v in range(SSM_TILES):
        yv = _dot(u_tile(v), tz_ref[v]) + _dot(hp[:, v * SSM_VW:(v + 1) * SSM_VW].astype(BF16), w1_ref[v])
        for b in range(nblk):
            for j in range(SSM_Q):
                ypm[b * SSM_PB + j * ncb:b * SSM_PB + (j + 1) * ncb, v * LANES:(v + 1) * LANES] = (
                    yv[b * ncb:(b + 1) * ncb, j * LANES:(j + 1) * LANES])

    for b in range(nblk):
        rows = slice(b * SSM_PB, (b + 1) * SSM_PB)
        y_hi, y_lo = _split_bf16(ypm[rows, :])
        y = _dot(permt_ref[...], y_hi) + _dot(permt_ref[...], y_lo) + d_ref[...] * u_ref[rows, :].astype(F32)
        g = _gelu_tanh(y)
        out_ref[rows, :] = (g * jax.nn.sigmoid(_dot(g.astype(BF16), gw_ref[...]) + gb_ref[...])).astype(out_ref.dtype)
    h_t_ref[...] = hcar[...]


def _ssm_chunked(u, p, h0, nb, tl):
    rows = u.shape[0]
    nt = rows // (nb * tl)
    assert tl % SSM_PB == 0
    tile = pl.BlockSpec((tl, SSM_WIDTH), lambda b, t: (b * nt + t, 0))
    wspec = _resident((SSM_TILES, SSM_VW, SSM_VW))
    state = (SUBLANES, 2 * SSM_LANES)
    return pl.pallas_call(
        functools.partial(_ssm_chunk_kernel, tl=tl),
        grid=(nb, nt),
        in_specs=[tile, _resident((SSM_PB, SSM_PB)), _resident((SSM_PB, SSM_PB)), wspec, wspec, wspec,
                  _resident((4 * SUBLANES, 2 * SSM_LANES)), _resident((1, SSM_WIDTH)),
                  _resident((SSM_WIDTH, SSM_WIDTH)), _resident((1, SSM_WIDTH)), _resident(state)],
        out_specs=[tile, pl.BlockSpec(state, lambda b, t: (0, 0))],
        out_shape=[jax.ShapeDtypeStruct((rows, SSM_WIDTH), BF16), jax.ShapeDtypeStruct(state, F32)],
        scratch_shapes=[pltpu.VMEM((SSM_Q, tl // SSM_Q, SSM_WIDTH), BF16),
                        pltpu.VMEM((tl // SSM_Q, 2 * SSM_LANES), F32),
                        pltpu.VMEM((tl // SSM_Q, 2 * SSM_LANES), F32),
                        pltpu.VMEM((tl, SSM_WIDTH), F32),
                        pltpu.VMEM(state, F32)],
        compiler_params=_cparams(("arbitrary", "arbitrary")),
        name="s5_ssm_chunked",
    )(u, p["perm"], p["permt"], p["w2"], p["tz"], p["w1"], p["chunk_powers"], p["d"], p["glu_w"],
      p["glu_b"], h0)


FFN_CHUNK = 512


def _resident(shape):
    return pl.BlockSpec(shape, lambda *_: (0,) * len(shape), pipeline_mode=pl.Buffered(1))


def _mix_ffn_kernel(a_ref, b_ref, wa_ref, wb_ref, h_ref, g_ref, wg_ref, wu_ref, wd_ref, o_ref):
    h = h_ref[...] + _dot(a_ref[...], wa_ref[...]) + _dot(b_ref[...], wb_ref[...])
    o_ref[...] = h
    xn = _rms(h, g_ref[...]).astype(BF16)
    for c in range(wg_ref.shape[1] // FFN_CHUNK):
        sl = slice(c * FFN_CHUNK, (c + 1) * FFN_CHUNK)
        a = (_silu(_dot(xn, wg_ref[:, sl])) * _dot(xn, wu_ref[:, sl])).astype(BF16)
        o_ref[...] += _dot(a, wd_ref[sl, :])


def _mix_ffn(a, b, wa, wb, h, gain, wg, wu, wd, tm):
    rows = h.shape[0]
    row_spec = lambda c: pl.BlockSpec((tm, c), lambda i: (i, 0))
    return pl.pallas_call(
        _mix_ffn_kernel,
        grid=(rows // tm,),
        in_specs=[row_spec(a.shape[1]), row_spec(b.shape[1]), _resident(wa.shape), _resident(wb.shape),
                  row_spec(D_MODEL), _resident((1, D_MODEL)),
                  _resident(wg.shape), _resident(wu.shape), _resident(wd.shape)],
        out_specs=row_spec(D_MODEL),
        out_shape=jax.ShapeDtypeStruct((rows, D_MODEL), F32),
        compiler_params=_cparams(("parallel",)),
        name="mix_ffn",
    )(a, b, wa, wb, h, gain, wg, wu, wd)


def _gla_kernel(q_ref, k_ref, v_ref, og_ref, lg_ref, hn_ref, s0_ref, o_ref, s_t_ref, st, *, tl, ck):
    @pl.when(pl.program_id(1) == 0)
    def _():
        st[...] = s0_ref[...]

    ri = lax.broadcasted_iota(jnp.int32, (ck, ck), 0)
    ci = lax.broadcasted_iota(jnp.int32, (ck, ck), 1)
    causal = ri >= ci
    tri = causal.astype(F32).astype(BF16)
    scale = GLA_DK ** -0.5

    for c in range(tl // ck):
        rows = slice(c * ck, (c + 1) * ck)
        lg_hi, lg_lo = _split_bf16(lg_ref[rows, :])
        b_all = _dot(tri, lg_hi) + _dot(tri, lg_lo)
        for h in range(GLA_HEADS):
            ks = slice(h * GLA_DK, (h + 1) * GLA_DK)
            vs = slice(h * GLA_DV, (h + 1) * GLA_DV)
            b = b_all[:, ks]
            b_last = b[ck - 1:ck, :]
            q = q_ref[rows, ks].astype(F32)
            k = k_ref[rows, ks].astype(F32)
            v = v_ref[rows, vs]
            qd = (q * (scale * jnp.exp(b))).astype(BF16)
            kd = (k * jnp.exp(-b)).astype(BF16)
            kl = (k * jnp.exp(b_last - b)).astype(BF16)
            att = lax.dot_general(qd, kd, (((1,), (1,)), ((), ())), preferred_element_type=F32)
            att = jnp.where(causal, att, 0.0).astype(BF16)
            s_h = st[h]
            o = _dot(att, v) + lax.dot_general(qd, s_h.astype(BF16), (((1,), (1,)), ((), ())),
                                               preferred_element_type=F32)
            st[h] = s_h * jnp.exp(b_last) + lax.dot_general(v, kl, (((0,), (0,)), ((), ())),
                                                            preferred_element_type=F32)
            o = o * lax.rsqrt(jnp.mean(o * o, axis=-1, keepdims=True) + EPS) * hn_ref[:, vs]
            o_ref[rows, vs] = (o * _silu(og_ref[rows, vs].astype(F32))).astype(o_ref.dtype)
    s_t_ref[...] = st[...]


def _gla(q, k, v, og, lg, head_norm, s0, nb, tl, ck):
    rows = q.shape[0]
    nt = rows // (nb * tl)
    tile = lambda c: pl.BlockSpec((tl, c), lambda b, t: (b * nt + t, 0))
    state = pl.BlockSpec((GLA_HEADS, GLA_DV, GLA_DK), lambda b, t: (0, 0, 0))
    return pl.pallas_call(
        functools.partial(_gla_kernel, tl=tl, ck=ck),
        grid=(nb, nt),
        in_specs=[tile(GLA_KEY), tile(GLA_KEY), tile(GLA_VAL), tile(GLA_VAL), tile(GLA_KEY),
                  pl.BlockSpec((1, GLA_VAL), lambda b, t: (0, 0)), state],
        out_specs=[tile(GLA_VAL), state],
        out_shape=[jax.ShapeDtypeStruct((rows, GLA_VAL), BF16),
                   jax.ShapeDtypeStruct((GLA_HEADS, GLA_DV, GLA_DK), F32)],
        scratch_shapes=[pltpu.VMEM((GLA_HEADS, GLA_DV, GLA_DK), F32)],
        compiler_params=_cparams(("arbitrary", "arbitrary")),
        name="gla",
    )(q, k, v, og, lg, head_norm, s0)


TOK_TILE = 512
SEG_ALIGN = 16
LOCAL_ROWS = 1152
FFN_TILE = 512
MOE_CHUNK = 1792
XS_COLS = D_MODEL + LANES
SEG_BITS = (512, 256, 128, 64, 32, 16)
INFO_E1, INFO_E2, INFO_R1, INFO_R2, INFO_W1, INFO_W2 = range(6)
assert LOCAL_ROWS >= 2 * TOK_TILE + N_EXPERTS * (SEG_ALIGN - 1) and LOCAL_ROWS % LANES == 0


def _router_kernel(o_ref, wo_ref, h_ref, g_ref, wr_ref, tri_ref, hout_ref, xn_ref, info_ref, cnt_ref):
    tm = h_ref.shape[0]
    h = h_ref[...] + _dot(o_ref[...], wo_ref[...])
    hout_ref[...] = h
    xn = _rms(h, g_ref[...])
    xn_ref[...] = xn.astype(BF16)
    x_hi, x_lo = _split_bf16(xn)
    w_hi = wr_ref[0]
    w_lo = wr_ref[1]
    logits = _dot(x_hi, w_hi) + _dot(x_lo, w_hi) + _dot(x_hi, w_lo)
    lane = lax.broadcasted_iota(jnp.int32, logits.shape, 1).astype(F32)
    neg = jnp.float32(-jnp.inf)
    logits = jnp.where(lane < N_EXPERTS, logits, neg)
    v1 = jnp.max(logits, axis=-1, keepdims=True)
    i1 = jnp.min(jnp.where(logits == v1, lane, float(LANES)), axis=-1, keepdims=True)
    rest = jnp.where(lane == i1, neg, logits)
    v2 = jnp.max(rest, axis=-1, keepdims=True)
    i2 = jnp.min(jnp.where(rest == v2, lane, float(LANES)), axis=-1, keepdims=True)
    e2 = jnp.exp(v2 - v1)
    w1 = 1.0 / (1.0 + e2)
    w2 = e2 / (1.0 + e2)
    hit1 = lane == i1
    hit2 = lane == i2
    sel = jnp.where(hit1, 1.0, 0.0) + jnp.where(hit2, 1.0, 0.0)
    cum = _dot(tri_ref[...], sel.astype(BF16))
    rank1 = jnp.sum(jnp.where(hit1, cum, 0.0), axis=-1, keepdims=True) - 1.0
    rank2 = jnp.sum(jnp.where(hit2, cum, 0.0), axis=-1, keepdims=True) - 1.0
    info = jnp.zeros_like(logits)
    for col, val in ((INFO_E1, i1), (INFO_E2, i2), (INFO_R1, rank1), (INFO_R2, rank2),
                     (INFO_W1, w1), (INFO_W2, w2)):
        info = jnp.where(lane == col, val, info)
    info_ref[...] = info
    cnt_ref[...] = jnp.broadcast_to(cum[tm - 1:tm, :], (SUBLANES, LANES))


def _router(o, wo, h, gain, wr, tri):
    rows = h.shape[0]
    tm = TOK_TILE
    nt = rows // tm
    row_spec = lambda c: pl.BlockSpec((tm, c), lambda i: (i, 0))
    return pl.pallas_call(
        _router_kernel,
        grid=(nt,),
        in_specs=[row_spec(o.shape[1]), _resident(wo.shape), row_spec(D_MODEL), _resident((1, D_MODEL)),
                  _resident((2, D_MODEL, LANES)), _resident((tm, tm))],
        out_specs=[row_spec(D_MODEL), row_spec(D_MODEL), row_spec(LANES),
                   pl.BlockSpec((SUBLANES, LANES), lambda i: (i, 0))],
        out_shape=[jax.ShapeDtypeStruct((rows, D_MODEL), F32),
                   jax.ShapeDtypeStruct((rows, D_MODEL), BF16),
                   jax.ShapeDtypeStruct((rows, LANES), F32),
                   jax.ShapeDtypeStruct((nt * SUBLANES, LANES), F32)],
        compiler_params=_cparams(("parallel",)),
        name="router",
    )(o, wo, h, gain, wr, tri)


def _segment_copies(tile, loff_ref, slen_ref, goff_ref, make_copy, fn):
    for e in range(N_EXPERTS):
        idx = tile * N_EXPERTS + e
        lo = loff_ref[idx]
        n = slen_ref[idx]
        go = goff_ref[idx]
        for bit in SEG_BITS:
            pos = n & (-2 * bit)

            @pl.when((n & bit) != 0)
            def _(lo=lo, go=go, pos=pos, bit=bit):
                fn(make_copy(pl.multiple_of(lo + pos, SEG_ALIGN), pl.multiple_of(go + pos, SEG_ALIGN), bit))


def _local_rows(info_row, loff_ref, tile, which_e, which_r):
    e = info_row(which_e)
    loc = info_row(which_r)
    for ex in range(N_EXPERTS):
        loc = loc + jnp.where(e == float(ex), loff_ref[tile * N_EXPERTS + ex].astype(F32), 0.0)
    return loc


def _dispatch_kernel(loff_ref, slen_ref, goff_ref, xn_ref, info_ref, zero_ref, xs_ref, buf, sem):
    del zero_ref
    i = pl.program_id(0)
    n = pl.num_programs(0)
    slot = i % 2

    def copy_for(s):
        def make(lo, go, rows):
            return pltpu.make_async_copy(buf.at[s, pl.ds(lo, rows), :], xs_ref.at[pl.ds(go, rows), :],
                                         sem.at[s])
        return make

    wait = lambda c: c.wait()
    start = lambda c: c.start()

    @pl.when(i >= 2)
    def _():
        _segment_copies(i - 2, loff_ref, slen_ref, goff_ref, copy_for(slot), wait)

    info_t = info_ref[...].T
    row_of = lambda r: info_t[r:r + 1, :]
    loc1 = _local_rows(row_of, loff_ref, i, INFO_E1, INFO_R1)
    loc2 = _local_rows(row_of, loff_ref, i, INFO_E2, INFO_R2)
    ridx = lax.broadcasted_iota(jnp.int32, (LOCAL_ROWS, TOK_TILE), 0).astype(F32)
    p1 = jnp.where(ridx == loc1, 1.0, 0.0).astype(BF16)
    p2 = jnp.where(ridx == loc2, 1.0, 0.0).astype(BF16)
    info = info_ref[...]
    lane = lax.broadcasted_iota(jnp.int32, info.shape, 1)

    def weight_parts(col):
        w = info[:, col:col + 1]
        hi = w.astype(BF16).astype(F32)
        mid = (w - hi).astype(BF16).astype(F32)
        lo = w - hi - mid
        return jnp.where(lane == 0, hi, jnp.where(lane == 1, mid, jnp.where(lane == 2, lo, 0.0))).astype(BF16)

    buf[slot, :, 0:D_MODEL] = _dot(p1 + p2, xn_ref[...]).astype(BF16)
    buf[slot, :, D_MODEL:XS_COLS] = (_dot(p1, weight_parts(INFO_W1)) + _dot(p2, weight_parts(INFO_W2))).astype(BF16)
    _segment_copies(i, loff_ref, slen_ref, goff_ref, copy_for(slot), start)

    @pl.when(i == n - 1)
    def _():
        _segment_copies(i, loff_ref, slen_ref, goff_ref, copy_for(slot), wait)

        @pl.when(i >= 1)
        def _():
            _segment_copies(i - 1, loff_ref, slen_ref, goff_ref, copy_for(1 - slot), wait)


def _dispatch(loff, slen, goff, xn, info, n_sort):
    rows = xn.shape[0]
    nt = rows // TOK_TILE
    zeros = jnp.zeros((n_sort, XS_COLS), BF16)
    return pl.pallas_call(
        _dispatch_kernel,
        grid_spec=pltpu.PrefetchScalarGridSpec(
            num_scalar_prefetch=3,
            grid=(nt,),
            in_specs=[pl.BlockSpec((TOK_TILE, D_MODEL), lambda i, *_: (i, 0)),
                      pl.BlockSpec((TOK_TILE, LANES), lambda i, *_: (i, 0)),
                      pl.BlockSpec(memory_space=pl.ANY)],
            out_specs=pl.BlockSpec(memory_space=pl.ANY),
            scratch_shapes=[pltpu.VMEM((2, LOCAL_ROWS, XS_COLS), BF16),
                            pltpu.SemaphoreType.DMA((2,))],
        ),
        out_shape=jax.ShapeDtypeStruct((n_sort, XS_COLS), BF16),
        input_output_aliases={5: 0},
        compiler_params=_cparams(("arbitrary",)),
        name="moe_dispatch",
    )(loff, slen, goff, xn, info, zeros)


def _moe_ffn_kernel(te_ref, tot_ref, xs_ref, wg_ref, wu_ref, wd_ref, y_ref, w_scr, acc_ref):
    del te_ref
    j = pl.program_id(0)
    f = pl.program_id(1)

    @pl.when(j < tot_ref[0])
    def _():
        @pl.when(f == 0)
        def _():
            w = jnp.sum(xs_ref[:, D_MODEL:XS_COLS].astype(F32), axis=-1, keepdims=True)
            w_scr[...] = jnp.broadcast_to(w, (FFN_TILE, LANES))
            acc_ref[...] = jnp.zeros_like(acc_ref)

        x = xs_ref[:, 0:D_MODEL]
        a = _silu(_dot(x, wg_ref[0])) * _dot(x, wu_ref[0]) * w_scr[:, 0:1]
        acc_ref[...] += _dot(a.astype(BF16), wd_ref[0])

        @pl.when(f == pl.num_programs(1) - 1)
        def _():
            y_ref[...] = acc_ref[...].astype(y_ref.dtype)

    @pl.when((j >= tot_ref[0]) & (f == pl.num_programs(1) - 1))
    def _():
        y_ref[...] = jnp.zeros_like(y_ref)


def _moe_ffn(tile_e, total, xs, wg, wu, wd, fc):
    n_sort = xs.shape[0]
    nt = n_sort // FFN_TILE
    nf = wg.shape[2] // fc
    tile_of = lambda j, tot: jnp.minimum(j, tot[0] - 1)
    chunk_of = lambda j, f, tot: jnp.where(j < tot[0], f, nf - 1)
    return pl.pallas_call(
        _moe_ffn_kernel,
        grid_spec=pltpu.PrefetchScalarGridSpec(
            num_scalar_prefetch=2,
            grid=(nt, nf),
            in_specs=[pl.BlockSpec((FFN_TILE, XS_COLS), lambda j, f, te, tot: (tile_of(j, tot), 0)),
                      pl.BlockSpec((1, D_MODEL, fc), lambda j, f, te, tot: (te[j], 0, chunk_of(j, f, tot))),
                      pl.BlockSpec((1, D_MODEL, fc), lambda j, f, te, tot: (te[j], 0, chunk_of(j, f, tot))),
                      pl.BlockSpec((1, fc, D_MODEL), lambda j, f, te, tot: (te[j], chunk_of(j, f, tot), 0))],
            out_specs=pl.BlockSpec((FFN_TILE, D_MODEL), lambda j, f, te, tot: (j, 0)),
            scratch_shapes=[pltpu.VMEM((FFN_TILE, LANES), F32),
                            pltpu.VMEM((FFN_TILE, D_MODEL), F32)],
        ),
        out_shape=jax.ShapeDtypeStruct((n_sort, D_MODEL), BF16),
        compiler_params=_cparams(("arbitrary", "arbitrary")),
        name="moe_ffn",
    )(tile_e, total, xs, wg, wu, wd)


def _combine_kernel(loff_ref, slen_ref, goff_ref, info_ref, h_ref, fn_ref, y_ref, o_ref, buf, sem):
    i = pl.program_id(0)
    n = pl.num_programs(0)
    slot = i % 2

    def copy_for(s):
        def make(lo, go, rows):
            return pltpu.make_async_copy(y_ref.at[pl.ds(go, rows), :], buf.at[s, pl.ds(lo, rows), :],
                                         sem.at[s])
        return make

    wait = lambda c: c.wait()
    start = lambda c: c.start()

    @pl.when(i == 0)
    def _():
        buf[...] = jnp.zeros_like(buf)
        _segment_copies(i, loff_ref, slen_ref, goff_ref, copy_for(slot), start)

    @pl.when(i + 1 < n)
    def _():
        _segment_copies(i + 1, loff_ref, slen_ref, goff_ref, copy_for(1 - slot), start)

    _segment_copies(i, loff_ref, slen_ref, goff_ref, copy_for(slot), wait)

    ys = buf[slot]
    info = info_ref[...]
    col_of = lambda c: info[:, c:c + 1]
    loc1 = _local_rows(col_of, loff_ref, i, INFO_E1, INFO_R1)
    loc2 = _local_rows(col_of, loff_ref, i, INFO_E2, INFO_R2)
    cidx = lax.broadcasted_iota(jnp.int32, (TOK_TILE, LOCAL_ROWS), 1).astype(F32)
    pt = (jnp.where(cidx == loc1, 1.0, 0.0) + jnp.where(cidx == loc2, 1.0, 0.0)).astype(BF16)
    o_ref[...] = _rms(h_ref[...] + _dot(pt, ys), fn_ref[...])


def _combine(loff, slen, goff, info, h, final_norm, y):
    rows = h.shape[0]
    nt = rows // TOK_TILE
    return pl.pallas_call(
        _combine_kernel,
        grid_spec=pltpu.PrefetchScalarGridSpec(
            num_scalar_prefetch=3,
            grid=(nt,),
            in_specs=[pl.BlockSpec((TOK_TILE, LANES), lambda i, *_: (i, 0)),
                      pl.BlockSpec((TOK_TILE, D_MODEL), lambda i, *_: (i, 0)),
                      pl.BlockSpec((1, D_MODEL), lambda i, *_: (0, 0)),
                      pl.BlockSpec(memory_space=pl.ANY)],
            out_specs=pl.BlockSpec((TOK_TILE, D_MODEL), lambda i, *_: (i, 0)),
            scratch_shapes=[pltpu.VMEM((2, LOCAL_ROWS, D_MODEL), BF16),
                            pltpu.SemaphoreType.DMA((2,))],
        ),
        out_shape=jax.ShapeDtypeStruct((rows, D_MODEL), F32),
        compiler_params=_cparams(("arbitrary",)),
        name="moe_combine",
    )(loff, slen, goff, info, h, final_norm, y)


def _routing_tables(cnt, n_tiles_max):
    slen = (cnt + SEG_ALIGN - 1) // SEG_ALIGN * SEG_ALIGN
    loff = jnp.cumsum(slen, axis=1) - slen
    rows_e = jnp.sum(slen, axis=0)
    rows_pad = (rows_e + FFN_TILE - 1) // FFN_TILE * FFN_TILE
    base = jnp.cumsum(rows_pad) - rows_pad
    goff = base[None, :] + jnp.cumsum(slen, axis=0) - slen
    tile_end = jnp.cumsum(rows_pad // FFN_TILE)
    total = tile_end[-1]
    j = jnp.arange(n_tiles_max, dtype=jnp.int32)
    tile_e = jnp.sum(jnp.minimum(j, total - 1)[:, None] >= tile_end[None, :], axis=1).astype(jnp.int32)
    flat = lambda a: a.reshape(-1).astype(jnp.int32)
    return flat(loff), flat(slen), flat(goff), tile_e, total.reshape(1).astype(jnp.int32)


def _moe(o, wo, h, gain, wr, wg, wu, wd, final_norm):
    rows = h.shape[0]
    assert rows % TOK_TILE == 0
    nt = rows // TOK_TILE
    max_rows = 2 * rows + nt * N_EXPERTS * (SEG_ALIGN - 1) + N_EXPERTS * (FFN_TILE - SEG_ALIGN)
    n_tiles_max = -(-max_rows // FFN_TILE)
    tri = jnp.tril(jnp.ones((TOK_TILE, TOK_TILE), BF16))
    h, xn, info, cnt = _router(o, wo, h, gain, wr, tri)
    cnt = cnt.reshape(nt, SUBLANES, LANES)[:, 0, :N_EXPERTS].astype(jnp.int32)
    loff, slen, goff, tile_e, total = _routing_tables(cnt, n_tiles_max)
    xs = _dispatch(loff, slen, goff, xn, info, n_tiles_max * FFN_TILE)
    y = _moe_ffn(tile_e, total, xs, wg, wu, wd, MOE_CHUNK)
    return _combine(loff, slen, goff, info, h, final_norm, y)


def _ssm_params(lam_re, lam_im, log_dt, b_re, b_im, c_re, c_im, d, glu_w, glu_b):
    g, n, hh = SSM_GROUPS, SSM_STATE, SSM_GROUP
    lr, li = lam_re.astype(F32), lam_im.astype(F32)
    dt = jnp.exp(log_dt.astype(F32))[:, None]
    mag = jnp.exp(dt * lr)
    ar, ai = mag * jnp.cos(dt * li), mag * jnp.sin(dt * li)
    den = lr * lr + li * li
    nr = ar - 1.0
    zr = (nr * lr + ai * li) / den
    zi = (ai * lr - nr * li) / den
    br = zr[..., None] * b_re - zi[..., None] * b_im
    bi = zr[..., None] * b_im + zi[..., None] * b_re
    eye = jnp.eye(g, dtype=F32)
    bblk = jnp.concatenate(
        [jnp.einsum("gnh,gk->ghkn", x, eye).reshape(SSM_WIDTH, SSM_LANES) for x in (br, bi)], axis=1)
    cblk = jnp.concatenate(
        [jnp.einsum("ghn,gk->gnkh", x, eye).reshape(SSM_LANES, SSM_WIDTH) for x in (c_re, -c_im)], axis=0)

    sub = jnp.arange(SUBLANES, dtype=jnp.int32)[:, None]

    def scan_tables(base_r, base_i, layout):
        def power(m):
            pr, pi = jnp.ones((SUBLANES, g, n), F32), jnp.zeros((SUBLANES, g, n), F32)
            for step in range(1, SUBLANES + 1):
                nr_, ni_ = pr * base_r - pi * base_i, pr * base_i + pi * base_r
                take = (m >= step)[:, :, None]
                pr, pi = jnp.where(take, nr_, pr), jnp.where(take, ni_, pi)
            return pr, pi

        tabs = []
        for s in (1, 2, 4):
            pr, pi = power(jnp.full((SUBLANES, 1), s, jnp.int32))
            keep = (sub >= s).astype(F32)[:, :, None]
            tabs.append(layout(pr * keep, pi * keep))
        tabs.append(layout(*power(sub + 1)))
        return jnp.concatenate(tabs, axis=0)

    flat = lambda pr, pi: jnp.concatenate([pr.reshape(SUBLANES, SSM_LANES), pi.reshape(SUBLANES, SSM_LANES)], axis=1)

    q, nv, gl = SSM_Q, SSM_TILES, SSM_GROUPS // SSM_TILES
    pw_r, pw_i = [jnp.ones_like(ar)], [jnp.zeros_like(ar)]
    for _ in range(q):
        pw_r, pw_i = pw_r + [pw_r[-1] * ar - pw_i[-1] * ai], pw_i + [pw_r[-1] * ai + pw_i[-1] * ar]
    pw_r, pw_i = jnp.stack(pw_r), jnp.stack(pw_i)
    cr, ci = c_re.astype(F32), c_im.astype(F32)
    eye = jnp.eye(gl, dtype=F32)
    tiled = lambda pr, pi: jnp.stack([pr, pi], axis=1).reshape(SUBLANES, 2, nv, gl, n).transpose(
        0, 2, 1, 3, 4).reshape(SUBLANES, 2 * SSM_LANES)
    crt, cit = cr.transpose(0, 2, 1)[:, :, :, None], ci.transpose(0, 2, 1)[:, :, :, None]
    m_r = crt * br[:, :, None, :] - cit * bi[:, :, None, :]
    m_i = crt * bi[:, :, None, :] + cit * br[:, :, None, :]
    kern = [jnp.sum(pw_r[dd][:, :, None, None] * m_r - pw_i[dd][:, :, None, None] * m_i, axis=1)
            for dd in range(q)]
    none = jnp.zeros_like(kern[0])
    kf = jnp.stack([jnp.stack([kern[o - i] if o >= i else none for o in range(q)]) for i in range(q)])
    kf = kf.transpose(2, 0, 1, 3, 4)
    kf = kf.reshape(nv, gl, q, q, hh, hh).transpose(0, 2, 1, 5, 3, 4)
    tz = (kf[:, :, :, :, :, None, :] * eye[None, None, :, None, None, :, None]).reshape(nv, SSM_VW, SSM_VW)
    rev_r = jnp.stack([pw_r[q - 1 - j] for j in range(q)])[:, :, :, None]
    rev_i = jnp.stack([pw_i[q - 1 - j] for j in range(q)])[:, :, :, None]
    wb = jnp.stack([rev_r * br[None] - rev_i * bi[None], rev_r * bi[None] + rev_i * br[None]])
    wb = wb.reshape(2, q, nv, gl, n, hh).transpose(2, 1, 3, 5, 0, 4)
    w2 = (wb[:, :, :, :, :, None, :] * eye[None, None, :, None, None, :, None]).reshape(nv, SSM_VW, SSM_VW)
    nx_r, nx_i = pw_r[1:][:, :, None, :], pw_i[1:][:, :, None, :]
    wc = jnp.stack([cr[None] * nx_r - ci[None] * nx_i, -(cr[None] * nx_i + ci[None] * nx_r)])
    wc = wc.reshape(2, q, nv, gl, hh, n).transpose(2, 0, 3, 5, 1, 4)
    w1 = (wc[:, :, :, :, :, None, :] * eye[None, None, :, None, None, :, None]).reshape(nv, SSM_VW, SSM_VW)
    ncb = SSM_PB // q
    r_out = jnp.arange(SSM_PB)
    src = q * (r_out % ncb) + r_out // ncb
    perm = (jnp.arange(SSM_PB)[None, :] == src[:, None]).astype(BF16)
    return {
        "bblk": bblk.astype(BF16), "cblk": cblk.astype(BF16),
        "d": d.reshape(1, SSM_WIDTH).astype(F32),
        "glu_w": glu_w.astype(BF16), "glu_b": glu_b.reshape(1, SSM_WIDTH).astype(F32),
        "powers": scan_tables(ar, ai, flat),
        "chunk_powers": scan_tables(pw_r[q], pw_i[q], tiled),
        "tz": tz.astype(BF16), "w2": w2.astype(BF16), "w1": w1.astype(BF16),
        "perm": perm, "permt": perm.T,
    }


def _row(x):
    return x.reshape(1, -1).astype(F32)


def _pad_cols(w, n):
    return jnp.pad(w, ((0, 0), (0, n - w.shape[1])))


def kernel(x, meta_tokens, ev_norm_mix, ev_w_in, ev_conv_w, ev_conv_b, ev_gate_a_w, ev_gate_a_b, ev_gate_x_w, ev_gate_x_b, ev_lru_lambda, ev_ssm_lambda_re, ev_ssm_lambda_im, ev_ssm_log_dt, ev_ssm_b_re, ev_ssm_b_im, ev_ssm_c_re, ev_ssm_c_im, ev_ssm_d, ev_ssm_glu_w, ev_ssm_glu_b, ev_w_out, ev_norm_ffn, ev_ffn_w_gate, ev_ffn_w_up, ev_ffn_w_down, od_norm_mix, od_w_in, od_gla_gate_w2, od_gla_gate_b, od_gla_norm, od_w_out, od_norm_ffn, od_router_w, od_moe_w_gate, od_moe_w_up, od_moe_w_down, final_norm):
    nb, seq, _ = x.shape
    rows = nb * seq
    assert ev_w_in.shape[0] == 1 and od_w_in.shape[0] == 1, "two-layer trunk only"

    w_in0 = ev_w_in[0].astype(BF16)
    lru_p = {
        "conv_w": ev_conv_w[0].astype(F32), "conv_b": _row(ev_conv_b[0]),
        "gate_a_w": ev_gate_a_w[0].astype(BF16), "gate_a_b": _row(ev_gate_a_b[0]),
        "gate_x_w": ev_gate_x_w[0].astype(BF16), "gate_x_b": _row(ev_gate_x_b[0]),
        "softplus": _row(jax.nn.softplus(-ev_lru_lambda[0].astype(F32))),
    }
    ssm_p = _ssm_params(ev_ssm_lambda_re[0], ev_ssm_lambda_im[0], ev_ssm_log_dt[0], ev_ssm_b_re[0],
                        ev_ssm_b_im[0], ev_ssm_c_re[0], ev_ssm_c_im[0], ev_ssm_d[0],
                        ev_ssm_glu_w[0], ev_ssm_glu_b[0])
    w_out0 = ev_w_out[0].astype(BF16)
    w_out0_a, w_out0_b = w_out0[:D_MODEL], w_out0[D_MODEL:]
    ffn_g, ffn_u, ffn_d = (w[0].astype(BF16) for w in (ev_ffn_w_gate, ev_ffn_w_up, ev_ffn_w_down))
    odd_in = 2 * GLA_KEY + 2 * GLA_VAL
    w_in1 = _pad_cols(od_w_in[0], odd_in + LANES).astype(BF16)
    w2 = jnp.pad(od_gla_gate_w2[0].astype(F32), ((0, LANES - GLA_RANK), (0, 0)))
    w2_hi = w2.astype(BF16)
    w2_split = jnp.stack([w2_hi, (w2 - w2_hi.astype(F32)).astype(BF16)])
    wr = _pad_cols(od_router_w[0].astype(F32), LANES)
    wr_hi = wr.astype(BF16)
    wr_split = jnp.stack([wr_hi, (wr - wr_hi.astype(F32)).astype(BF16)])
    w_out1 = od_w_out[0].astype(BF16)
    moe_g, moe_u, moe_d = (w[0].astype(BF16) for w in (od_moe_w_gate, od_moe_w_up, od_moe_w_down))

    tl = min(512, seq)
    tm = min(1024, rows)
    tm_ffn = min(512, rows)
    ck = min(128, tl)
    tl_ssm = min(2048, seq)
    even_splits = (D_MODEL, D_MODEL, SSM_WIDTH)

    def even_mixer(h, nbatch, tile_m, tile_f, tile_l, conv0, h0, s0):
        gate, rec, u = _norm_proj(h, _row(ev_norm_mix[0]), w_in0, even_splits, tile_m)
        a_out, conv_t, h_t = _lru(gate, rec, lru_p, conv0, h0, nbatch, tile_l)
        if tile_l % SSM_PB == 0:
            b_out, s_t = _ssm_chunked(u, ssm_p, s0, nbatch, tl_ssm)
        else:
            b_out, s_t = _ssm(u, ssm_p, s0, nbatch, tile_l)
            s_t = s_t.reshape(SUBLANES, 2, SSM_TILES, SSM_LANES // SSM_TILES).transpose(0, 2, 1, 3).reshape(
                SUBLANES, 2 * SSM_LANES)
        h = _mix_ffn(a_out, b_out, w_out0_a, w_out0_b, h, _row(ev_norm_ffn[0]), ffn_g, ffn_u, ffn_d, tile_f)
        return h, (conv_t, h_t, s_t)

    def gla_inputs(h, tile_m):
        return _gla_proj(h, _row(od_norm_mix[0]), w_in1, w2_split, _row(od_gla_gate_b[0]), tile_m)

    zeros = lambda *s: jnp.zeros(s, F32)
    hm = meta_tokens.astype(F32)
    hm, (conv_m, h_m, s_m) = even_mixer(hm, 1, N_META, N_META, N_META, zeros(SUBLANES, D_MODEL),
                                        zeros(SUBLANES, D_MODEL), zeros(SUBLANES, 2 * SSM_LANES))
    qm, km, vm, ogm, lgm = gla_inputs(hm, N_META)
    _, gla_s = _gla(qm, km, vm, ogm, lgm, _row(od_gla_norm[0]),
                    zeros(GLA_HEADS, GLA_DV, GLA_DK), 1, N_META, N_META)

    h = x.reshape(rows, D_MODEL).astype(F32)
    h, _ = even_mixer(h, nb, tm, tm_ffn, tl, conv_m, h_m, s_m)
    q, k, v, og, lg = gla_inputs(h, tm)
    o, _ = _gla(q, k, v, og, lg, _row(od_gla_norm[0]), gla_s, nb, tl, ck)
    out = _moe(o, w_out1, h, _row(od_norm_ffn[0]), wr_split, moe_g, moe_u, moe_d, _row(final_norm))
    return out.reshape(nb, seq, D_MODEL)
```

```python
import functools
import math

import jax
import jax.numpy as jnp
from jax import lax
from jax.experimental import pallas as pl
from jax.experimental.pallas import tpu as pltpu

F32 = jnp.float32
BF16 = jnp.bfloat16

D_MODEL = 1024
N_META = 16
EPS = 1e-6
LRU_BLOCKS = 4
LRU_BLOCK = 256
LRU_C = 8.0
SSM_WIDTH = 512
SSM_GROUP = 16
SSM_GROUPS = 32
SSM_STATE = 64
SSM_LANES = SSM_GROUPS * SSM_STATE
D_FF = 3072
GLA_HEADS = 4
GLA_DK = 128
GLA_DV = 256
GLA_KEY = 512
GLA_VAL = 1024
GLA_RANK = 16
GLA_TAU = 16.0
N_EXPERTS = 8
D_FF_EXPERT = 3584

SUBLANES = 8
LANES = 128
VMEM_LIMIT = 52 * 1024 * 1024


def _cparams(sem):
    return pltpu.CompilerParams(dimension_semantics=sem, vmem_limit_bytes=VMEM_LIMIT)


def _dot(a, b):
    return jnp.dot(a, b, preferred_element_type=F32)


def _rms(x, gain):
    ms = jnp.mean(x * x, axis=-1, keepdims=True)
    return x * lax.rsqrt(ms + EPS) * gain


def _gelu_tanh(x):
    c = math.sqrt(2.0 / math.pi)
    return 0.5 * x * (1.0 + jnp.tanh(c * (x + 0.044715 * (x * x * x))))


def _silu(x):
    return x * jax.nn.sigmoid(x)


def _split_bf16(x):
    hi = x.astype(BF16)
    lo = (x - hi.astype(F32)).astype(BF16)
    return hi, lo


def _norm_proj_kernel(x_ref, g_ref, w_ref, *out_refs):
    xn = _rms(x_ref[...], g_ref[...]).astype(BF16)
    off = 0
    for o_ref in out_refs:
        n = o_ref.shape[-1]
        o_ref[...] = _dot(xn, w_ref[:, off:off + n]).astype(o_ref.dtype)
        off += n


def _norm_proj(x, gain, w, splits, tm):
    rows = x.shape[0]
    n = w.shape[1]
    assert sum(splits) == n and rows % tm == 0
    return pl.pallas_call(
        _norm_proj_kernel,
        grid=(rows // tm,),
        in_specs=[
            pl.BlockSpec((tm, D_MODEL), lambda i: (i, 0)),
            pl.BlockSpec((1, D_MODEL), lambda i: (0, 0)),
            _resident((D_MODEL, n)),
        ],
        out_specs=[pl.BlockSpec((tm, s), lambda i: (i, 0)) for s in splits],
        out_shape=[jax.ShapeDtypeStruct((rows, s), BF16) for s in splits],
        compiler_params=_cparams(("parallel",)),
        name="norm_proj",
    )(x, gain, w)


PROJ_SUB = 256


def _gla_proj_kernel(x_ref, g_ref, w_ref, w2_ref, b2_ref, q_ref, k_ref, v_ref, og_ref, lg_ref):
    tm = x_ref.shape[0]
    sub = min(PROJ_SUB, tm)
    o0 = 2 * GLA_KEY + GLA_VAL
    for r in range(tm // sub):
        rows = slice(r * sub, (r + 1) * sub)
        xn = _rms(x_ref[rows, :], g_ref[...]).astype(BF16)
        q_ref[rows, :] = _dot(xn, w_ref[:, 0:GLA_KEY]).astype(BF16)
        k_ref[rows, :] = _dot(xn, w_ref[:, GLA_KEY:2 * GLA_KEY]).astype(BF16)
        v_ref[rows, :] = _dot(xn, w_ref[:, 2 * GLA_KEY:o0]).astype(BF16)
        og_ref[rows, :] = _dot(xn, w_ref[:, o0:o0 + GLA_VAL]).astype(BF16)
        glr = _dot(xn, w_ref[:, o0 + GLA_VAL:o0 + GLA_VAL + LANES])
        glr_hi, glr_lo = _split_bf16(glr)
        w2_hi = w2_ref[0]
        w2_lo = w2_ref[1]
        pre = _dot(glr_hi, w2_hi) + _dot(glr_lo, w2_hi) + _dot(glr_hi, w2_lo) + b2_ref[...]
        lg_ref[rows, :] = (jnp.minimum(pre, 0.0) - jnp.log1p(jnp.exp(-jnp.abs(pre)))) * (1.0 / GLA_TAU)


def _gla_proj(x, gain, w, w2, b2, tm):
    rows = x.shape[0]
    n = w.shape[1]
    row_spec = lambda c: pl.BlockSpec((tm, c), lambda i: (i, 0))
    return pl.pallas_call(
        _gla_proj_kernel,
        grid=(rows // tm,),
        in_specs=[
            row_spec(D_MODEL),
            pl.BlockSpec((1, D_MODEL), lambda i: (0, 0)),
            _resident((D_MODEL, n)),
            pl.BlockSpec((2, LANES, GLA_KEY), lambda i: (0, 0, 0)),
            pl.BlockSpec((1, GLA_KEY), lambda i: (0, 0)),
        ],
        out_specs=[row_spec(GLA_KEY), row_spec(GLA_KEY), row_spec(GLA_VAL), row_spec(GLA_VAL),
                   row_spec(GLA_KEY)],
        out_shape=[jax.ShapeDtypeStruct((rows, GLA_KEY), BF16),
                   jax.ShapeDtypeStruct((rows, GLA_KEY), BF16),
                   jax.ShapeDtypeStruct((rows, GLA_VAL), BF16),
                   jax.ShapeDtypeStruct((rows, GLA_VAL), BF16),
                   jax.ShapeDtypeStruct((rows, GLA_KEY), F32)],
        compiler_params=_cparams(("parallel",)),
        name="gla_proj",
    )(x, gain, w, w2, b2)


def _lru_kernel(gate_ref, rec_ref, cw_ref, cb_ref, wa_ref, ba_ref, wx_ref, bx_ref, sp_ref,
                conv0_ref, h0_ref, out_ref, conv_t_ref, h_t_ref, xbuf, abuf, bbuf, hcar, *, tl):
    @pl.when(pl.program_id(1) == 0)
    def _():
        xbuf[0:SUBLANES, :] = conv0_ref[...]
        hcar[...] = h0_ref[...]

    xbuf[SUBLANES:SUBLANES + tl, :] = rec_ref[...].astype(F32)
    xc = cb_ref[...]
    for tap in range(4):
        xc = xc + cw_ref[tap:tap + 1, :] * xbuf[SUBLANES - 3 + tap:SUBLANES - 3 + tap + tl, :]
    xbuf[0:SUBLANES, :] = xbuf[tl:tl + SUBLANES, :]

    for blk in range(LRU_BLOCKS):
        sl = slice(blk * LRU_BLOCK, (blk + 1) * LRU_BLOCK)
        xs = xc[:, sl]
        xb = xs.astype(BF16)
        r = jax.nn.sigmoid(_dot(xb, wa_ref[blk]) + ba_ref[:, sl])
        i = jax.nn.sigmoid(_dot(xb, wx_ref[blk]) + bx_ref[:, sl])
        a = jnp.exp(-LRU_C * r * sp_ref[:, sl])
        abuf[:, sl] = a
        bbuf[:, sl] = jnp.sqrt(1.0 - a * a) * (i * xs)

    row = lax.broadcasted_iota(jnp.int32, (SUBLANES, D_MODEL), 0)

    def body(gidx, carry):
        r0 = pl.multiple_of(gidx * SUBLANES, SUBLANES)
        a = abuf[pl.ds(r0, SUBLANES), :]
        b = bbuf[pl.ds(r0, SUBLANES), :]
        for s in (1, 2, 4):
            m = row >= s
            a_sh = pltpu.roll(a, s, 0)
            b_sh = pltpu.roll(b, s, 0)
            b = jnp.where(m, a * b_sh + b, b)
            a = jnp.where(m, a * a_sh, a)
        h = a * carry + b
        bbuf[pl.ds(r0, SUBLANES), :] = h
        return jnp.broadcast_to(h[SUBLANES - 1:SUBLANES, :], (SUBLANES, D_MODEL))

    carry = lax.fori_loop(0, tl // SUBLANES, body, hcar[...])
    hcar[...] = carry
    out_ref[...] = (_gelu_tanh(gate_ref[...].astype(F32)) * bbuf[...]).astype(out_ref.dtype)
    conv_t_ref[...] = xbuf[0:SUBLANES, :]
    h_t_ref[...] = carry


def _lru(gate, rec, p, conv0, h0, nb, tl):
    rows = gate.shape[0]
    nt = rows // (nb * tl)
    tile = pl.BlockSpec((tl, D_MODEL), lambda b, t: (b * nt + t, 0))
    full2 = lambda shp: pl.BlockSpec(shp, lambda b, t: (0, 0))
    full3 = lambda shp: pl.BlockSpec(shp, lambda b, t: (0, 0, 0))
    return pl.pallas_call(
        functools.partial(_lru_kernel, tl=tl),
        grid=(nb, nt),
        in_specs=[tile, tile,
                  full2((4, D_MODEL)), full2((1, D_MODEL)),
                  full3((LRU_BLOCKS, LRU_BLOCK, LRU_BLOCK)), full2((1, D_MODEL)),
                  full3((LRU_BLOCKS, LRU_BLOCK, LRU_BLOCK)), full2((1, D_MODEL)),
                  full2((1, D_MODEL)),
                  full2((SUBLANES, D_MODEL)), full2((SUBLANES, D_MODEL))],
        out_specs=[tile, full2((SUBLANES, D_MODEL)), full2((SUBLANES, D_MODEL))],
        out_shape=[jax.ShapeDtypeStruct((rows, D_MODEL), BF16),
                   jax.ShapeDtypeStruct((SUBLANES, D_MODEL), F32),
                   jax.ShapeDtypeStruct((SUBLANES, D_MODEL), F32)],
        scratch_shapes=[pltpu.VMEM((tl + SUBLANES, D_MODEL), F32),
                        pltpu.VMEM((tl, D_MODEL), F32),
                        pltpu.VMEM((tl, D_MODEL), F32),
                        pltpu.VMEM((SUBLANES, D_MODEL), F32)],
        compiler_params=_cparams(("arbitrary", "arbitrary")),
        name="rg_lru",
    )(gate, rec, p["conv_w"], p["conv_b"], p["gate_a_w"], p["gate_a_b"], p["gate_x_w"],
      p["gate_x_b"], p["softplus"], conv0, h0)


SSM_CHUNK = 256


def _complex_scan(xs, pw_ref, hcar, n_rows, lane_pairs, prev=None):
    row = lax.broadcasted_iota(jnp.int32, (SUBLANES, SSM_CHUNK), 0)
    for re0, im0 in lane_pairs:
        re = slice(re0, re0 + SSM_CHUNK)
        im = slice(im0, im0 + SSM_CHUNK)
        tabs = [(pw_ref[k * SUBLANES:(k + 1) * SUBLANES, re], pw_ref[k * SUBLANES:(k + 1) * SUBLANES, im])
                for k in range(4)]

        def body(gidx, carry, re=re, im=im, tabs=tabs):
            cr, ci = carry
            r0 = pl.multiple_of(gidx * SUBLANES, SUBLANES)
            hr = xs[pl.ds(r0, SUBLANES), re]
            hi = xs[pl.ds(r0, SUBLANES), im]
            for k, s in enumerate((1, 2, 4)):
                pr, pi = tabs[k]
                sr = pltpu.roll(hr, s, 0)
                si = pltpu.roll(hi, s, 0)
                hr, hi = hr + (pr * sr - pi * si), hi + (pr * si + pi * sr)
            pr, pi = tabs[3]
            hr, hi = hr + (pr * cr - pi * ci), hi + (pr * ci + pi * cr)
            xs[pl.ds(r0, SUBLANES), re] = hr
            xs[pl.ds(r0, SUBLANES), im] = hi
            if prev is not None:
                prev[pl.ds(r0, SUBLANES), re] = jnp.where(row == 0, cr, pltpu.roll(hr, 1, 0))
                prev[pl.ds(r0, SUBLANES), im] = jnp.where(row == 0, ci, pltpu.roll(hi, 1, 0))
            last = slice(SUBLANES - 1, SUBLANES)
            return (jnp.broadcast_to(hr[last, :], (SUBLANES, SSM_CHUNK)),
                    jnp.broadcast_to(hi[last, :], (SUBLANES, SSM_CHUNK)))

        cr, ci = lax.fori_loop(0, n_rows // SUBLANES, body, (hcar[:, re], hcar[:, im]))
        hcar[:, re] = cr
        hcar[:, im] = ci


def _ssm_kernel(u_ref, bblk_ref, cblk_ref, d_ref, gw_ref, gb_ref, pw_ref, h0_ref,
                out_ref, h_t_ref, xs, hcar, *, tl):
    @pl.when(pl.program_id(1) == 0)
    def _():
        hcar[...] = h0_ref[...]

    u = u_ref[...]
    xs[...] = _dot(u, bblk_ref[...])
    pairs = [(c * SSM_CHUNK, SSM_LANES + c * SSM_CHUNK) for c in range(SSM_LANES // SSM_CHUNK)]
    _complex_scan(xs, pw_ref, hcar, tl, pairs)
    y = _dot(xs[...].astype(BF16), cblk_ref[...]) + d_ref[...] * u.astype(F32)
    g = _gelu_tanh(y)
    out_ref[...] = (g * jax.nn.sigmoid(_dot(g.astype(BF16), gw_ref[...]) + gb_ref[...])).astype(out_ref.dtype)
    h_t_ref[...] = hcar[...]


def _ssm(u, p, h0, nb, tl):
    rows = u.shape[0]
    nt = rows // (nb * tl)
    tile = pl.BlockSpec((tl, SSM_WIDTH), lambda b, t: (b * nt + t, 0))
    full2 = lambda shp: pl.BlockSpec(shp, lambda b, t: (0, 0))
    return pl.pallas_call(
        functools.partial(_ssm_kernel, tl=tl),
        grid=(nb, nt),
        in_specs=[tile,
                  _resident((SSM_WIDTH, 2 * SSM_LANES)), _resident((2 * SSM_LANES, SSM_WIDTH)),
                  full2((1, SSM_WIDTH)), full2((SSM_WIDTH, SSM_WIDTH)), full2((1, SSM_WIDTH)),
                  full2((4 * SUBLANES, 2 * SSM_LANES)), full2((SUBLANES, 2 * SSM_LANES))],
        out_specs=[tile, full2((SUBLANES, 2 * SSM_LANES))],
        out_shape=[jax.ShapeDtypeStruct((rows, SSM_WIDTH), BF16),
                   jax.ShapeDtypeStruct((SUBLANES, 2 * SSM_LANES), F32)],
        scratch_shapes=[pltpu.VMEM((tl, 2 * SSM_LANES), F32),
                        pltpu.VMEM((SUBLANES, 2 * SSM_LANES), F32)],
        compiler_params=_cparams(("arbitrary", "arbitrary")),
        name="s5_ssm",
    )(u, p["bblk"], p["cblk"], p["d"], p["glu_w"], p["glu_b"], p["powers"], h0)


SSM_Q = 8
SSM_PB = 256
SSM_TILES = SSM_WIDTH // LANES
SSM_VW = SSM_Q * LANES


def _ssm_chunk_kernel(u_ref, perm_ref, permt_ref, w2_ref, tz_ref, w1_ref, pw_ref, d_ref, gw_ref, gb_ref,
                      h0_ref, out_ref, h_t_ref, a_scr, xs, hp, ypm, hcar, *, tl):
    nblk = tl // SSM_PB
    ncb = SSM_PB // SSM_Q
    nc = tl // SSM_Q

    @pl.when(pl.program_id(1) == 0)
    def _():
        hcar[...] = h0_ref[...]

    for b in range(nblk):
        pu = _dot(perm_ref[...], u_ref[b * SSM_PB:(b + 1) * SSM_PB, :]).astype(BF16)
        for j in range(SSM_Q):
            a_scr[j, b * ncb:(b + 1) * ncb, :] = pu[j * ncb:(j + 1) * ncb, :]

    def u_tile(v):
        return jnp.concatenate([a_scr[j, :, v * LANES:(v + 1) * LANES] for j in range(SSM_Q)], axis=1)

    for v in range(SSM_TILES):
        xs[:, v * SSM_VW:(v + 1) * SSM_VW] = _dot(u_tile(v), w2_ref[v])

    half = SSM_VW // 2
    pairs = [(v * SSM_VW + c * SSM_CHUNK, v * SSM_VW + half + c * SSM_CHUNK)
             for v in range(SSM_TILES) for c in range(half // SSM_CHUNK)]
    _complex_scan(xs, pw_ref, hcar, nc, pairs, prev=hp)

    for v in range(SSM_TILES):
        yv = _dot(u_tile(v), tz_ref[v]) + _dot(hp[:, v * SSM_VW:(v + 1) * SSM_VW].astype(BF16), w1_ref[v])
        for b in range(nblk):
            for j in range(SSM_Q):
                ypm[b * SSM_PB + j * ncb:b * SSM_PB + (j + 1) * ncb, v * LANES:(v + 1) * LANES] = (
                    yv[b * ncb:(b + 1) * ncb, j * LANES:(j + 1) * LANES])

    for b in range(nblk):
        rows = slice(b * SSM_PB, (b + 1) * SSM_PB)
        y_hi, y_lo = _split_bf16(ypm[rows, :])
        y = _dot(permt_ref[...], y_hi) + _dot(permt_ref[...], y_lo) + d_ref[...] * u_ref[rows, :].astype(F32)
        g = _gelu_tanh(y)
        out_ref[rows, :] = (g * jax.nn.sigmoid(_dot(g.astype(BF16), gw_ref[...]) + gb_ref[...])).astype(out_ref.dtype)
    h_t_ref[...] = hcar[...]


def _ssm_chunked(u, p, h0, nb, tl):
    rows = u.shape[0]
    nt = rows // (nb * tl)
    assert tl % SSM_PB == 0
    tile = pl.BlockSpec((tl, SSM_WIDTH), lambda b, t: (b * nt + t, 0))
    wspec = _resident((SSM_TILES, SSM_VW, SSM_VW))
    state = (SUBLANES, 2 * SSM_LANES)
    return pl.pallas_call(
        functools.partial(_ssm_chunk_kernel, tl=tl),
        grid=(nb, nt),
        in_specs=[tile, _resident((SSM_PB, SSM_PB)), _resident((SSM_PB, SSM_PB)), wspec, wspec, wspec,
                  _resident((4 * SUBLANES, 2 * SSM_LANES)), _resident((1, SSM_WIDTH)),
                  _resident((SSM_WIDTH, SSM_WIDTH)), _resident((1, SSM_WIDTH)), _resident(state)],
        out_specs=[tile, pl.BlockSpec(state, lambda b, t: (0, 0))],
        out_shape=[jax.ShapeDtypeStruct((rows, SSM_WIDTH), BF16), jax.ShapeDtypeStruct(state, F32)],
        scratch_shapes=[pltpu.VMEM((SSM_Q, tl // SSM_Q, SSM_WIDTH), BF16),
                        pltpu.VMEM((tl // SSM_Q, 2 * SSM_LANES), F32),
                        pltpu.VMEM((tl // SSM_Q, 2 * SSM_LANES), F32),
                        pltpu.VMEM((tl, SSM_WIDTH), F32),
                        pltpu.VMEM(state, F32)],
        compiler_params=_cparams(("arbitrary", "arbitrary")),
        name="s5_ssm_chunked",
    )(u, p["perm"], p["permt"], p["w2"], p["tz"], p["w1"], p["chunk_powers"], p["d"], p["glu_w"],
      p["glu_b"], h0)


FFN_CHUNK = 512


def _resident(shape):
    return pl.BlockSpec(shape, lambda *_: (0,) * len(shape), pipeline_mode=pl.Buffered(1))


def _mix_ffn_kernel(a_ref, b_ref, wa_ref, wb_ref, h_ref, g_ref, wg_ref, wu_ref, wd_ref, o_ref):
    h = h_ref[...] + _dot(a_ref[...], wa_ref[...]) + _dot(b_ref[...], wb_ref[...])
    o_ref[...] = h
    xn = _rms(h, g_ref[...]).astype(BF16)
    for c in range(wg_ref.shape[1] // FFN_CHUNK):
        sl = slice(c * FFN_CHUNK, (c + 1) * FFN_CHUNK)
        a = (_silu(_dot(xn, wg_ref[:, sl])) * _dot(xn, wu_ref[:, sl])).astype(BF16)
        o_ref[...] += _dot(a, wd_ref[sl, :])


def _mix_ffn(a, b, wa, wb, h, gain, wg, wu, wd, tm):
    rows = h.shape[0]
    row_spec = lambda c: pl.BlockSpec((tm, c), lambda i: (i, 0))
    return pl.pallas_call(
        _mix_ffn_kernel,
        grid=(rows // tm,),
        in_specs=[row_spec(a.shape[1]), row_spec(b.shape[1]), _resident(wa.shape), _resident(wb.shape),
                  row_spec(D_MODEL), _resident((1, D_MODEL)),
                  _resident(wg.shape), _resident(wu.shape), _resident(wd.shape)],
        out_specs=row_spec(D_MODEL),
        out_shape=jax.ShapeDtypeStruct((rows, D_MODEL), F32),
        compiler_params=_cparams(("parallel",)),
        name="mix_ffn",
    )(a, b, wa, wb, h, gain, wg, wu, wd)


def _gla_kernel(q_ref, k_ref, v_ref, og_ref, lg_ref, hn_ref, s0_ref, o_ref, s_t_ref, st, *, tl, ck):
    @pl.when(pl.program_id(1) == 0)
    def _():
        st[...] = s0_ref[...]

    ri = lax.broadcasted_iota(jnp.int32, (ck, ck), 0)
    ci = lax.broadcasted_iota(jnp.int32, (ck, ck), 1)
    causal = ri >= ci
    tri = causal.astype(F32).astype(BF16)
    scale = GLA_DK ** -0.5

    for c in range(tl // ck):
        rows = slice(c * ck, (c + 1) * ck)
        lg_hi, lg_lo = _split_bf16(lg_ref[rows, :])
        b_all = _dot(tri, lg_hi) + _dot(tri, lg_lo)
        for h in range(GLA_HEADS):
            ks = slice(h * GLA_DK, (h + 1) * GLA_DK)
            vs = slice(h * GLA_DV, (h + 1) * GLA_DV)
            b = b_all[:, ks]
            b_last = b[ck - 1:ck, :]
            q = q_ref[rows, ks].astype(F32)
            k = k_ref[rows, ks].astype(F32)
            v = v_ref[rows, vs]
            qd = (q * (scale * jnp.exp(b))).astype(BF16)
            kd = (k * jnp.exp(-b)).astype(BF16)
            kl = (k * jnp.exp(b_last - b)).astype(BF16)
            att = lax.dot_general(qd, kd, (((1,), (1,)), ((), ())), preferred_element_type=F32)
            att = jnp.where(causal, att, 0.0).astype(BF16)
            s_h = st[h]
            o = _dot(att, v) + lax.dot_general(qd, s_h.astype(BF16), (((1,), (1,)), ((), ())),
                                               preferred_element_type=F32)
            st[h] = s_h * jnp.exp(b_last) + lax.dot_general(v, kl, (((0,), (0,)), ((), ())),
                                                            preferred_element_type=F32)
            o = o * lax.rsqrt(jnp.mean(o * o, axis=-1, keepdims=True) + EPS) * hn_ref[:, vs]
            o_ref[rows, vs] = (o * _silu(og_ref[rows, vs].astype(F32))).astype(o_ref.dtype)
    s_t_ref[...] = st[...]


def _gla(q, k, v, og, lg, head_norm, s0, nb, tl, ck):
    rows = q.shape[0]
    nt = rows // (nb * tl)
    tile = lambda c: pl.BlockSpec((tl, c), lambda b, t: (b * nt + t, 0))
    state = pl.BlockSpec((GLA_HEADS, GLA_DV, GLA_DK), lambda b, t: (0, 0, 0))
    return pl.pallas_call(
        functools.partial(_gla_kernel, tl=tl, ck=ck),
        grid=(nb, nt),
        in_specs=[tile(GLA_KEY), tile(GLA_KEY), tile(GLA_VAL), tile(GLA_VAL), tile(GLA_KEY),
                  pl.BlockSpec((1, GLA_VAL), lambda b, t: (0, 0)), state],
        out_specs=[tile(GLA_VAL), state],
        out_shape=[jax.ShapeDtypeStruct((rows, GLA_VAL), BF16),
                   jax.ShapeDtypeStruct((GLA_HEADS, GLA_DV, GLA_DK), F32)],
        scratch_shapes=[pltpu.VMEM((GLA_HEADS, GLA_DV, GLA_DK), F32)],
        compiler_params=_cparams(("arbitrary", "arbitrary")),
        name="gla",
    )(q, k, v, og, lg, head_norm, s0)


TOK_TILE = 512
SEG_ALIGN = 16
LOCAL_ROWS = 1152
FFN_TILE = 512
MOE_CHUNK = 1792
XS_COLS = D_MODEL + LANES
SEG_BITS = (512, 256, 128, 64, 32, 16)
INFO_E1, INFO_E2, INFO_R1, INFO_R2, INFO_W1, INFO_W2 = range(6)
assert LOCAL_ROWS >= 2 * TOK_TILE + N_EXPERTS * (SEG_ALIGN - 1) and LOCAL_ROWS % LANES == 0


def _router_kernel(o_ref, wo_ref, h_ref, g_ref, wr_ref, tri_ref, hout_ref, xn_ref, info_ref, cnt_ref):
    tm = h_ref.shape[0]
    sub = tri_ref.shape[0]
    w_hi = wr_ref[0]
    w_lo = wr_ref[1]
    lane = lax.broadcasted_iota(jnp.int32, (sub, LANES), 1).astype(F32)
    neg = jnp.float32(-jnp.inf)
    count = jnp.zeros((1, LANES), F32)
    for r in range(tm // sub):
        rows = slice(r * sub, (r + 1) * sub)
        h = h_ref[rows, :] + _dot(o_ref[rows, :], wo_ref[...])
        hout_ref[rows, :] = h
        xn = _rms(h, g_ref[...])
        xn_ref[rows, :] = xn.astype(BF16)
        x_hi, x_lo = _split_bf16(xn)
        logits = _dot(x_hi, w_hi) + _dot(x_lo, w_hi) + _dot(x_hi, w_lo)
        logits = jnp.where(lane < N_EXPERTS, logits, neg)
        v1 = jnp.max(logits, axis=-1, keepdims=True)
        i1 = jnp.min(jnp.where(logits == v1, lane, float(LANES)), axis=-1, keepdims=True)
        rest = jnp.where(lane == i1, neg, logits)
        v2 = jnp.max(rest, axis=-1, keepdims=True)
        i2 = jnp.min(jnp.where(rest == v2, lane, float(LANES)), axis=-1, keepdims=True)
        e2 = jnp.exp(v2 - v1)
        w1 = 1.0 / (1.0 + e2)
        w2 = e2 / (1.0 + e2)
        hit1 = lane == i1
        hit2 = lane == i2
        sel = jnp.where(hit1, 1.0, 0.0) + jnp.where(hit2, 1.0, 0.0)
        cum = _dot(tri_ref[...], sel.astype(BF16)) + count
        count = cum[sub - 1:sub, :]
        rank1 = jnp.sum(jnp.where(hit1, cum, 0.0), axis=-1, keepdims=True) - 1.0
        rank2 = jnp.sum(jnp.where(hit2, cum, 0.0), axis=-1, keepdims=True) - 1.0
        info = jnp.zeros_like(logits)
        for col, val in ((INFO_E1, i1), (INFO_E2, i2), (INFO_R1, rank1), (INFO_R2, rank2),
                         (INFO_W1, w1), (INFO_W2, w2)):
            info = jnp.where(lane == col, val, info)
        info_ref[rows, :] = info
    cnt_ref[...] = jnp.broadcast_to(count, (SUBLANES, LANES))


def _router(o, wo, h, gain, wr, tri):
    rows = h.shape[0]
    tm = TOK_TILE
    nt = rows // tm
    row_spec = lambda c: pl.BlockSpec((tm, c), lambda i: (i, 0))
    return pl.pallas_call(
        _router_kernel,
        grid=(nt,),
        in_specs=[row_spec(o.shape[1]), _resident(wo.shape), row_spec(D_MODEL), _resident((1, D_MODEL)),
                  _resident((2, D_MODEL, LANES)), _resident(tri.shape)],
        out_specs=[row_spec(D_MODEL), row_spec(D_MODEL), row_spec(LANES),
                   pl.BlockSpec((SUBLANES, LANES), lambda i: (i, 0))],
        out_shape=[jax.ShapeDtypeStruct((rows, D_MODEL), F32),
                   jax.ShapeDtypeStruct((rows, D_MODEL), BF16),
                   jax.ShapeDtypeStruct((rows, LANES), F32),
                   jax.ShapeDtypeStruct((nt * SUBLANES, LANES), F32)],
        compiler_params=_cparams(("parallel",)),
        name="router",
    )(o, wo, h, gain, wr, tri)


def _segment_copies(tile, loff_ref, slen_ref, goff_ref, make_copy, fn):
    for e in range(N_EXPERTS):
        idx = tile * N_EXPERTS + e
        lo = loff_ref[idx]
        n = slen_ref[idx]
        go = goff_ref[idx]
        for bit in SEG_BITS:
            pos = n & (-2 * bit)

            @pl.when((n & bit) != 0)
            def _(lo=lo, go=go, pos=pos, bit=bit):
                fn(make_copy(pl.multiple_of(lo + pos, SEG_ALIGN), pl.multiple_of(go + pos, SEG_ALIGN), bit))


def _local_rows(info_row, loff_ref, tile, which_e, which_r):
    e = info_row(which_e)
    loc = info_row(which_r)
    for ex in range(N_EXPERTS):
        loc = loc + jnp.where(e == float(ex), loff_ref[tile * N_EXPERTS + ex].astype(F32), 0.0)
    return loc


def _dispatch_kernel(loff_ref, slen_ref, goff_ref, xn_ref, info_ref, zero_ref, xs_ref, buf, sem):
    del zero_ref
    i = pl.program_id(0)
    n = pl.num_programs(0)
    slot = i % 2

    def copy_for(s):
        def make(lo, go, rows):
            return pltpu.make_async_copy(buf.at[s, pl.ds(lo, rows), :], xs_ref.at[pl.ds(go, rows), :],
                                         sem.at[s])
        return make

    wait = lambda c: c.wait()
    start = lambda c: c.start()

    @pl.when(i >= 2)
    def _():
        _segment_copies(i - 2, loff_ref, slen_ref, goff_ref, copy_for(slot), wait)

    info_t = info_ref[...].T
    row_of = lambda r: info_t[r:r + 1, :]
    loc1 = _local_rows(row_of, loff_ref, i, INFO_E1, INFO_R1)
    loc2 = _local_rows(row_of, loff_ref, i, INFO_E2, INFO_R2)
    ridx = lax.broadcasted_iota(jnp.int32, (LOCAL_ROWS, TOK_TILE), 0).astype(F32)
    p1 = jnp.where(ridx == loc1, 1.0, 0.0).astype(BF16)
    p2 = jnp.where(ridx == loc2, 1.0, 0.0).astype(BF16)
    info = info_ref[...]
    lane = lax.broadcasted_iota(jnp.int32, info.shape, 1)

    def weight_parts(col):
        w = info[:, col:col + 1]
        hi = w.astype(BF16).astype(F32)
        mid = (w - hi).astype(BF16).astype(F32)
        lo = w - hi - mid
        return jnp.where(lane == 0, hi, jnp.where(lane == 1, mid, jnp.where(lane == 2, lo, 0.0))).astype(BF16)

    buf[slot, :, 0:D_MODEL] = _dot(p1 + p2, xn_ref[...]).astype(BF16)
    buf[slot, :, D_MODEL:XS_COLS] = (_dot(p1, weight_parts(INFO_W1)) + _dot(p2, weight_parts(INFO_W2))).astype(BF16)
    _segment_copies(i, loff_ref, slen_ref, goff_ref, copy_for(slot), start)

    @pl.when(i == n - 1)
    def _():
        _segment_copies(i, loff_ref, slen_ref, goff_ref, copy_for(slot), wait)

        @pl.when(i >= 1)
        def _():
            _segment_copies(i - 1, loff_ref, slen_ref, goff_ref, copy_for(1 - slot), wait)


def _dispatch(loff, slen, goff, xn, info, n_sort):
    rows = xn.shape[0]
    nt = rows // TOK_TILE
    zeros = jnp.zeros((n_sort, XS_COLS), BF16)
    return pl.pallas_call(
        _dispatch_kernel,
        grid_spec=pltpu.PrefetchScalarGridSpec(
            num_scalar_prefetch=3,
            grid=(nt,),
            in_specs=[pl.BlockSpec((TOK_TILE, D_MODEL), lambda i, *_: (i, 0)),
                      pl.BlockSpec((TOK_TILE, LANES), lambda i, *_: (i, 0)),
                      pl.BlockSpec(memory_space=pl.ANY)],
            out_specs=pl.BlockSpec(memory_space=pl.ANY),
            scratch_shapes=[pltpu.VMEM((2, LOCAL_ROWS, XS_COLS), BF16),
                            pltpu.SemaphoreType.DMA((2,))],
        ),
        out_shape=jax.ShapeDtypeStruct((n_sort, XS_COLS), BF16),
        input_output_aliases={5: 0},
        compiler_params=_cparams(("arbitrary",)),
        name="moe_dispatch",
    )(loff, slen, goff, xn, info, zeros)


def _moe_ffn_kernel(te_ref, tot_ref, xs_ref, wg_ref, wu_ref, wd_ref, y_ref, w_scr, acc_ref):
    del te_ref
    j = pl.program_id(0)
    f = pl.program_id(1)

    @pl.when(j < tot_ref[0])
    def _():
        @pl.when(f == 0)
        def _():
            w = jnp.sum(xs_ref[:, D_MODEL:XS_COLS].astype(F32), axis=-1, keepdims=True)
            w_scr[...] = jnp.broadcast_to(w, (FFN_TILE, LANES))
            acc_ref[...] = jnp.zeros_like(acc_ref)

        x = xs_ref[:, 0:D_MODEL]
        a = _silu(_dot(x, wg_ref[0])) * _dot(x, wu_ref[0]) * w_scr[:, 0:1]
        acc_ref[...] += _dot(a.astype(BF16), wd_ref[0])

        @pl.when(f == pl.num_programs(1) - 1)
        def _():
            y_ref[...] = acc_ref[...].astype(y_ref.dtype)

    @pl.when((j >= tot_ref[0]) & (f == pl.num_programs(1) - 1))
    def _():
        y_ref[...] = jnp.zeros_like(y_ref)


def _moe_ffn(tile_e, total, xs, wg, wu, wd, fc):
    n_sort = xs.shape[0]
    nt = n_sort // FFN_TILE
    nf = wg.shape[2] // fc
    tile_of = lambda j, tot: jnp.minimum(j, tot[0] - 1)
    chunk_of = lambda j, f, tot: jnp.where(j < tot[0], f, nf - 1)
    return pl.pallas_call(
        _moe_ffn_kernel,
        grid_spec=pltpu.PrefetchScalarGridSpec(
            num_scalar_prefetch=2,
            grid=(nt, nf),
            in_specs=[pl.BlockSpec((FFN_TILE, XS_COLS), lambda j, f, te, tot: (tile_of(j, tot), 0)),
                      pl.BlockSpec((1, D_MODEL, fc), lambda j, f, te, tot: (te[j], 0, chunk_of(j, f, tot))),
                      pl.BlockSpec((1, D_MODEL, fc), lambda j, f, te, tot: (te[j], 0, chunk_of(j, f, tot))),
                      pl.BlockSpec((1, fc, D_MODEL), lambda j, f, te, tot: (te[j], chunk_of(j, f, tot), 0))],
            out_specs=pl.BlockSpec((FFN_TILE, D_MODEL), lambda j, f, te, tot: (j, 0)),
            scratch_shapes=[pltpu.VMEM((FFN_TILE, LANES), F32),
                            pltpu.VMEM((FFN_TILE, D_MODEL), F32)],
        ),
        out_shape=jax.ShapeDtypeStruct((n_sort, D_MODEL), BF16),
        compiler_params=_cparams(("arbitrary", "arbitrary")),
        name="moe_ffn",
    )(tile_e, total, xs, wg, wu, wd)


def _combine_kernel(loff_ref, slen_ref, goff_ref, info_ref, h_ref, fn_ref, y_ref, o_ref, buf, sem):
    i = pl.program_id(0)
    n = pl.num_programs(0)
    slot = i % 2

    def copy_for(s):
        def make(lo, go, rows):
            return pltpu.make_async_copy(y_ref.at[pl.ds(go, rows), :], buf.at[s, pl.ds(lo, rows), :],
                                         sem.at[s])
        return make

    wait = lambda c: c.wait()
    start = lambda c: c.start()

    @pl.when(i == 0)
    def _():
        buf[...] = jnp.zeros_like(buf)
        _segment_copies(i, loff_ref, slen_ref, goff_ref, copy_for(slot), start)

    @pl.when(i + 1 < n)
    def _():
        _segment_copies(i + 1, loff_ref, slen_ref, goff_ref, copy_for(1 - slot), start)

    _segment_copies(i, loff_ref, slen_ref, goff_ref, copy_for(slot), wait)

    ys = buf[slot]
    info = info_ref[...]
    col_of = lambda c: info[:, c:c + 1]
    loc1 = _local_rows(col_of, loff_ref, i, INFO_E1, INFO_R1)
    loc2 = _local_rows(col_of, loff_ref, i, INFO_E2, INFO_R2)
    cidx = lax.broadcasted_iota(jnp.int32, (TOK_TILE, LOCAL_ROWS), 1).astype(F32)
    pt = (jnp.where(cidx == loc1, 1.0, 0.0) + jnp.where(cidx == loc2, 1.0, 0.0)).astype(BF16)
    o_ref[...] = _rms(h_ref[...] + _dot(pt, ys), fn_ref[...])


def _combine(loff, slen, goff, info, h, final_norm, y):
    rows = h.shape[0]
    nt = rows // TOK_TILE
    return pl.pallas_call(
        _combine_kernel,
        grid_spec=pltpu.PrefetchScalarGridSpec(
            num_scalar_prefetch=3,
            grid=(nt,),
            in_specs=[pl.BlockSpec((TOK_TILE, LANES), lambda i, *_: (i, 0)),
                      pl.BlockSpec((TOK_TILE, D_MODEL), lambda i, *_: (i, 0)),
                      pl.BlockSpec((1, D_MODEL), lambda i, *_: (0, 0)),
                      pl.BlockSpec(memory_space=pl.ANY)],
            out_specs=pl.BlockSpec((TOK_TILE, D_MODEL), lambda i, *_: (i, 0)),
            scratch_shapes=[pltpu.VMEM((2, LOCAL_ROWS, D_MODEL), BF16),
                            pltpu.SemaphoreType.DMA((2,))],
        ),
        out_shape=jax.ShapeDtypeStruct((rows, D_MODEL), F32),
        compiler_params=_cparams(("arbitrary",)),
        name="moe_combine",
    )(loff, slen, goff, info, h, final_norm, y)


def _routing_tables(cnt, n_tiles_max):
    slen = (cnt + SEG_ALIGN - 1) // SEG_ALIGN * SEG_ALIGN
    loff = jnp.cumsum(slen, axis=1) - slen
    rows_e = jnp.sum(slen, axis=0)
    rows_pad = (rows_e + FFN_TILE - 1) // FFN_TILE * FFN_TILE
    base = jnp.cumsum(rows_pad) - rows_pad
    goff = base[None, :] + jnp.cumsum(slen, axis=0) - slen
    tile_end = jnp.cumsum(rows_pad // FFN_TILE)
    total = tile_end[-1]
    j = jnp.arange(n_tiles_max, dtype=jnp.int32)
    tile_e = jnp.sum(jnp.minimum(j, total - 1)[:, None] >= tile_end[None, :], axis=1).astype(jnp.int32)
    flat = lambda a: a.reshape(-1).astype(jnp.int32)
    return flat(loff), flat(slen), flat(goff), tile_e, total.reshape(1).astype(jnp.int32)


def _moe(o, wo, h, gain, wr, wg, wu, wd, final_norm):
    rows = h.shape[0]
    assert rows % TOK_TILE == 0
    nt = rows // TOK_TILE
    max_rows = 2 * rows + nt * N_EXPERTS * (SEG_ALIGN - 1) + N_EXPERTS * (FFN_TILE - SEG_ALIGN)
    n_tiles_max = -(-max_rows // FFN_TILE)
    tri = jnp.tril(jnp.ones((PROJ_SUB, PROJ_SUB), BF16))
    h, xn, info, cnt = _router(o, wo, h, gain, wr, tri)
    cnt = cnt.reshape(nt, SUBLANES, LANES)[:, 0, :N_EXPERTS].astype(jnp.int32)
    loff, slen, goff, tile_e, total = _routing_tables(cnt, n_tiles_max)
    xs = _dispatch(loff, slen, goff, xn, info, n_tiles_max * FFN_TILE)
    y = _moe_ffn(tile_e, total, xs, wg, wu, wd, MOE_CHUNK)
    return _combine(loff, slen, goff, info, h, final_norm, y)


def _ssm_params(lam_re, lam_im, log_dt, b_re, b_im, c_re, c_im, d, glu_w, glu_b):
    g, n, hh = SSM_GROUPS, SSM_STATE, SSM_GROUP
    lr, li = lam_re.astype(F32), lam_im.astype(F32)
    dt = jnp.exp(log_dt.astype(F32))[:, None]
    mag = jnp.exp(dt * lr)
    ar, ai = mag * jnp.cos(dt * li), mag * jnp.sin(dt * li)
    den = lr * lr + li * li
    nr = ar - 1.0
    zr = (nr * lr + ai * li) / den
    zi = (ai * lr - nr * li) / den
    br = zr[..., None] * b_re - zi[..., None] * b_im
    bi = zr[..., None] * b_im + zi[..., None] * b_re
    eye = jnp.eye(g, dtype=F32)
    bblk = jnp.concatenate(
        [jnp.einsum("gnh,gk->ghkn", x, eye).reshape(SSM_WIDTH, SSM_LANES) for x in (br, bi)], axis=1)
    cblk = jnp.concatenate(
        [jnp.einsum("ghn,gk->gnkh", x, eye).reshape(SSM_LANES, SSM_WIDTH) for x in (c_re, -c_im)], axis=0)

    sub = jnp.arange(SUBLANES, dtype=jnp.int32)[:, None]

    def scan_tables(base_r, base_i, layout):
        def power(m):
            pr, pi = jnp.ones((SUBLANES, g, n), F32), jnp.zeros((SUBLANES, g, n), F32)
            for step in range(1, SUBLANES + 1):
                nr_, ni_ = pr * base_r - pi * base_i, pr * base_i + pi * base_r
                take = (m >= step)[:, :, None]
                pr, pi = jnp.where(take, nr_, pr), jnp.where(take, ni_, pi)
            return pr, pi

        tabs = []
        for s in (1, 2, 4):
            pr, pi = power(jnp.full((SUBLANES, 1), s, jnp.int32))
            keep = (sub >= s).astype(F32)[:, :, None]
            tabs.append(layout(pr * keep, pi * keep))
        tabs.append(layout(*power(sub + 1)))
        return jnp.concatenate(tabs, axis=0)

    flat = lambda pr, pi: jnp.concatenate([pr.reshape(SUBLANES, SSM_LANES), pi.reshape(SUBLANES, SSM_LANES)], axis=1)

    q, nv, gl = SSM_Q, SSM_TILES, SSM_GROUPS // SSM_TILES
    pw_r, pw_i = [jnp.ones_like(ar)], [jnp.zeros_like(ar)]
    for _ in range(q):
        pw_r, pw_i = pw_r + [pw_r[-1] * ar - pw_i[-1] * ai], pw_i + [pw_r[-1] * ai + pw_i[-1] * ar]
    pw_r, pw_i = jnp.stack(pw_r), jnp.stack(pw_i)
    cr, ci = c_re.astype(F32), c_im.astype(F32)

    def expand(compact, row_w, col_w):
        s_idx = jnp.arange(LANES)[:, None]
        c_idx = jnp.arange(SSM_VW)[None, :]
        onehot = (s_idx == (c_idx // (gl * col_w)) * col_w + c_idx % col_w).astype(BF16)
        full = jnp.einsum("vrs,sc->vrc", compact.astype(BF16), onehot, preferred_element_type=F32)
        r_idx = jnp.arange(SSM_VW)[:, None]
        same_group = (r_idx // row_w) % gl == (c_idx // col_w) % gl
        return jnp.where(same_group[None], full, 0.0).astype(BF16)

    tiled = lambda pr, pi: jnp.stack([pr, pi], axis=1).reshape(SUBLANES, 2, nv, gl, n).transpose(
        0, 2, 1, 3, 4).reshape(SUBLANES, 2 * SSM_LANES)
    crt, cit = cr.transpose(0, 2, 1)[:, :, :, None], ci.transpose(0, 2, 1)[:, :, :, None]
    m_r = crt * br[:, :, None, :] - cit * bi[:, :, None, :]
    m_i = crt * bi[:, :, None, :] + cit * br[:, :, None, :]
    kern = [jnp.sum(pw_r[dd][:, :, None, None] * m_r - pw_i[dd][:, :, None, None] * m_i, axis=1)
            for dd in range(q)]
    none = jnp.zeros_like(kern[0])
    kf = jnp.stack([jnp.stack([kern[o - i] if o >= i else none for o in range(q)]) for i in range(q)])
    kf = kf.transpose(2, 0, 1, 3, 4)
    kf = kf.reshape(nv, gl, q, q, hh, hh).transpose(0, 2, 1, 5, 3, 4)
    tz = expand(kf.reshape(nv, SSM_VW, LANES), hh, hh)
    rev_r = jnp.stack([pw_r[q - 1 - j] for j in range(q)])[:, :, :, None]
    rev_i = jnp.stack([pw_i[q - 1 - j] for j in range(q)])[:, :, :, None]
    wb = jnp.stack([rev_r * br[None] - rev_i * bi[None], rev_r * bi[None] + rev_i * br[None]])
    wb = wb.reshape(2, q, nv, gl, n, hh).transpose(2, 1, 3, 5, 0, 4)
    w2 = expand(wb.reshape(nv, SSM_VW, LANES), hh, n)
    nx_r, nx_i = pw_r[1:][:, :, None, :], pw_i[1:][:, :, None, :]
    wc = jnp.stack([cr[None] * nx_r - ci[None] * nx_i, -(cr[None] * nx_i + ci[None] * nx_r)])
    wc = wc.reshape(2, q, nv, gl, hh, n).transpose(2, 0, 3, 5, 1, 4)
    w1 = expand(wc.reshape(nv, SSM_VW, LANES), n, hh)
    ncb = SSM_PB // q
    r_out = jnp.arange(SSM_PB)
    src = q * (r_out % ncb) + r_out // ncb
    perm = (jnp.arange(SSM_PB)[None, :] == src[:, None]).astype(BF16)
    return {
        "bblk": bblk.astype(BF16), "cblk": cblk.astype(BF16),
        "d": d.reshape(1, SSM_WIDTH).astype(F32),
        "glu_w": glu_w.astype(BF16), "glu_b": glu_b.reshape(1, SSM_WIDTH).astype(F32),
        "powers": scan_tables(ar, ai, flat),
        "chunk_powers": scan_tables(pw_r[q], pw_i[q], tiled),
        "tz": tz.astype(BF16), "w2": w2.astype(BF16), "w1": w1.astype(BF16),
        "perm": perm, "permt": perm.T,
    }


def _row(x):
    return x.reshape(1, -1).astype(F32)


def _pad_cols(w, n):
    return jnp.pad(w, ((0, 0), (0, n - w.shape[1])))


def kernel(x, meta_tokens, ev_norm_mix, ev_w_in, ev_conv_w, ev_conv_b, ev_gate_a_w, ev_gate_a_b, ev_gate_x_w, ev_gate_x_b, ev_lru_lambda, ev_ssm_lambda_re, ev_ssm_lambda_im, ev_ssm_log_dt, ev_ssm_b_re, ev_ssm_b_im, ev_ssm_c_re, ev_ssm_c_im, ev_ssm_d, ev_ssm_glu_w, ev_ssm_glu_b, ev_w_out, ev_norm_ffn, ev_ffn_w_gate, ev_ffn_w_up, ev_ffn_w_down, od_norm_mix, od_w_in, od_gla_gate_w2, od_gla_gate_b, od_gla_norm, od_w_out, od_norm_ffn, od_router_w, od_moe_w_gate, od_moe_w_up, od_moe_w_down, final_norm):
    nb, seq, _ = x.shape
    rows = nb * seq
    assert ev_w_in.shape[0] == 1 and od_w_in.shape[0] == 1, "two-layer trunk only"

    w_in0 = ev_w_in[0].astype(BF16)
    lru_p = {
        "conv_w": ev_conv_w[0].astype(F32), "conv_b": _row(ev_conv_b[0]),
        "gate_a_w": ev_gate_a_w[0].astype(BF16), "gate_a_b": _row(ev_gate_a_b[0]),
        "gate_x_w": ev_gate_x_w[0].astype(BF16), "gate_x_b": _row(ev_gate_x_b[0]),
        "softplus": _row(jax.nn.softplus(-ev_lru_lambda[0].astype(F32))),
    }
    ssm_p = _ssm_params(ev_ssm_lambda_re[0], ev_ssm_lambda_im[0], ev_ssm_log_dt[0], ev_ssm_b_re[0],
                        ev_ssm_b_im[0], ev_ssm_c_re[0], ev_ssm_c_im[0], ev_ssm_d[0],
                        ev_ssm_glu_w[0], ev_ssm_glu_b[0])
    w_out0 = ev_w_out[0].astype(BF16)
    w_out0_a, w_out0_b = w_out0[:D_MODEL], w_out0[D_MODEL:]
    ffn_g, ffn_u, ffn_d = (w[0].astype(BF16) for w in (ev_ffn_w_gate, ev_ffn_w_up, ev_ffn_w_down))
    odd_in = 2 * GLA_KEY + 2 * GLA_VAL
    w_in1 = _pad_cols(od_w_in[0], odd_in + LANES).astype(BF16)
    w2 = jnp.pad(od_gla_gate_w2[0].astype(F32), ((0, LANES - GLA_RANK), (0, 0)))
    w2_hi = w2.astype(BF16)
    w2_split = jnp.stack([w2_hi, (w2 - w2_hi.astype(F32)).astype(BF16)])
    wr = _pad_cols(od_router_w[0].astype(F32), LANES)
    wr_hi = wr.astype(BF16)
    wr_split = jnp.stack([wr_hi, (wr - wr_hi.astype(F32)).astype(BF16)])
    w_out1 = od_w_out[0].astype(BF16)
    moe_g, moe_u, moe_d = (w[0].astype(BF16) for w in (od_moe_w_gate, od_moe_w_up, od_moe_w_down))

    tl = min(512, seq)
    tm = min(1024, rows)
    tm_ffn = min(512, rows)
    ck = min(128, tl)
    tl_ssm = min(2048, seq)
    even_splits = (D_MODEL, D_MODEL, SSM_WIDTH)

    def even_mixer(h, nbatch, tile_m, tile_f, tile_l, conv0, h0, s0):
        gate, rec, u = _norm_proj(h, _row(ev_norm_mix[0]), w_in0, even_splits, tile_m)
        a_out, conv_t, h_t = _lru(gate, rec, lru_p, conv0, h0, nbatch, tile_l)
        if tile_l % SSM_PB == 0:
            b_out, s_t = _ssm_chunked(u, ssm_p, s0, nbatch, tl_ssm)
        else:
            b_out, s_t = _ssm(u, ssm_p, s0, nbatch, tile_l)
            s_t = s_t.reshape(SUBLANES, 2, SSM_TILES, SSM_LANES // SSM_TILES).transpose(0, 2, 1, 3).reshape(
                SUBLANES, 2 * SSM_LANES)
        h = _mix_ffn(a_out, b_out, w_out0_a, w_out0_b, h, _row(ev_norm_ffn[0]), ffn_g, ffn_u, ffn_d, tile_f)
        return h, (conv_t, h_t, s_t)

    def gla_inputs(h, tile_m):
        return _gla_proj(h, _row(od_norm_mix[0]), w_in1, w2_split, _row(od_gla_gate_b[0]), tile_m)

    zeros = lambda *s: jnp.zeros(s, F32)
    hm = meta_tokens.astype(F32)
    hm, (conv_m, h_m, s_m) = even_mixer(hm, 1, N_META, N_META, N_META, zeros(SUBLANES, D_MODEL),
                                        zeros(SUBLANES, D_MODEL), zeros(SUBLANES, 2 * SSM_LANES))
    qm, km, vm, ogm, lgm = gla_inputs(hm, N_META)
    _, gla_s = _gla(qm, km, vm, ogm, lgm, _row(od_gla_norm[0]),
                    zeros(GLA_HEADS, GLA_DV, GLA_DK), 1, N_META, N_META)

    h = x.reshape(rows, D_MODEL).astype(F32)
    h, _ = even_mixer(h, nb, tm, tm_ffn, tl, conv_m, h_m, s_m)
    q, k, v, og, lg = gla_inputs(h, tm)
    o, _ = _gla(q, k, v, og, lg, _row(od_gla_norm[0]), gla_s, nb, tl, ck)
    out = _moe(o, w_out1, h, _row(od_norm_ffn[0]), wr_split, moe_g, moe_u, moe_d, _row(final_norm))
    return out.reshape(nb, seq, D_MODEL)
```

```python
import functools
import math

import jax
import jax.numpy as jnp
from jax import lax
from jax.experimental import pallas as pl
from jax.experimental.pallas import tpu as pltpu

F32 = jnp.float32
BF16 = jnp.bfloat16

D_MODEL = 1024
N_META = 16
EPS = 1e-6
LRU_BLOCKS = 4
LRU_BLOCK = 256
LRU_C = 8.0
SSM_WIDTH = 512
SSM_GROUP = 16
SSM_GROUPS = 32
SSM_STATE = 64
SSM_LANES = SSM_GROUPS * SSM_STATE
D_FF = 3072
GLA_HEADS = 4
GLA_DK = 128
GLA_DV = 256
GLA_KEY = 512
GLA_VAL = 1024
GLA_RANK = 16
GLA_TAU = 16.0
N_EXPERTS = 8
D_FF_EXPERT = 3584

SUBLANES = 8
LANES = 128
CHUNK_STEPS = 8
PERM_BLOCK = 256
VMEM_LIMIT = 52 * 1024 * 1024


def _cparams(sem):
    return pltpu.CompilerParams(dimension_semantics=sem, vmem_limit_bytes=VMEM_LIMIT)


def _dot(a, b):
    return jnp.dot(a, b, preferred_element_type=F32)


def _rms(x, gain):
    ms = jnp.mean(x * x, axis=-1, keepdims=True)
    return x * lax.rsqrt(ms + EPS) * gain


def _gelu_tanh(x):
    c = math.sqrt(2.0 / math.pi)
    return 0.5 * x * (1.0 + jnp.tanh(c * (x + 0.044715 * (x * x * x))))


def _sigmoid(x):
    return 0.5 * jnp.tanh(0.5 * x) + 0.5


def _silu(x):
    return x * jax.nn.sigmoid(x)


def _split_bf16(x):
    hi = x.astype(BF16)
    lo = (x - hi.astype(F32)).astype(BF16)
    return hi, lo


def _norm_proj_kernel(x_ref, g_ref, w_ref, perm_ref, *out_refs, permuted):
    tm = x_ref.shape[0]
    sub = min(PERM_BLOCK, tm)
    for r in range(tm // sub):
        rows = slice(r * sub, (r + 1) * sub)
        xn = _rms(x_ref[rows, :], g_ref[...]).astype(BF16)
        if permuted:
            xn = _dot(perm_ref[...], xn).astype(BF16)
        off = 0
        for o_ref in out_refs:
            n = o_ref.shape[-1]
            o_ref[rows, :] = _dot(xn, w_ref[:, off:off + n]).astype(o_ref.dtype)
            off += n


def _block_perm():
    r_out = jnp.arange(PERM_BLOCK)
    chunks = PERM_BLOCK // CHUNK_STEPS
    src = CHUNK_STEPS * (r_out % chunks) + r_out // chunks
    return (jnp.arange(PERM_BLOCK)[None, :] == src[:, None]).astype(BF16)


def _norm_proj(x, gain, w, splits, tm, permuted):
    rows = x.shape[0]
    n = w.shape[1]
    assert sum(splits) == n and rows % tm == 0 and (not permuted or tm % PERM_BLOCK == 0)
    return pl.pallas_call(
        functools.partial(_norm_proj_kernel, permuted=permuted),
        grid=(rows // tm,),
        in_specs=[
            pl.BlockSpec((tm, D_MODEL), lambda i: (i, 0)),
            pl.BlockSpec((1, D_MODEL), lambda i: (0, 0)),
            _resident((D_MODEL, n)),
            _resident((PERM_BLOCK, PERM_BLOCK)),
        ],
        out_specs=[pl.BlockSpec((tm, s), lambda i: (i, 0)) for s in splits],
        out_shape=[jax.ShapeDtypeStruct((rows, s), BF16) for s in splits],
        compiler_params=_cparams(("parallel",)),
        name="norm_proj",
    )(x, gain, w, _block_perm())


PROJ_SUB = 256


def _gla_proj_kernel(x_ref, g_ref, w_ref, w2_ref, b2_ref, q_ref, k_ref, v_ref, og_ref, lg_ref):
    tm = x_ref.shape[0]
    sub = min(PROJ_SUB, tm)
    o0 = 2 * GLA_KEY + GLA_VAL
    for r in range(tm // sub):
        rows = slice(r * sub, (r + 1) * sub)
        xn = _rms(x_ref[rows, :], g_ref[...]).astype(BF16)
        q_ref[rows, :] = _dot(xn, w_ref[:, 0:GLA_KEY]).astype(BF16)
        k_ref[rows, :] = _dot(xn, w_ref[:, GLA_KEY:2 * GLA_KEY]).astype(BF16)
        v_ref[rows, :] = _dot(xn, w_ref[:, 2 * GLA_KEY:o0]).astype(BF16)
        og_ref[rows, :] = _dot(xn, w_ref[:, o0:o0 + GLA_VAL]).astype(BF16)
        glr = _dot(xn, w_ref[:, o0 + GLA_VAL:o0 + GLA_VAL + LANES])
        glr_hi, glr_lo = _split_bf16(glr)
        w2_hi = w2_ref[0]
        w2_lo = w2_ref[1]
        pre = _dot(glr_hi, w2_hi) + _dot(glr_lo, w2_hi) + _dot(glr_hi, w2_lo) + b2_ref[...]
        lg_ref[rows, :] = (jnp.minimum(pre, 0.0) - jnp.log1p(jnp.exp(-jnp.abs(pre)))) * (1.0 / GLA_TAU)


def _gla_proj(x, gain, w, w2, b2, tm):
    rows = x.shape[0]
    n = w.shape[1]
    row_spec = lambda c: pl.BlockSpec((tm, c), lambda i: (i, 0))
    return pl.pallas_call(
        _gla_proj_kernel,
        grid=(rows // tm,),
        in_specs=[
            row_spec(D_MODEL),
            pl.BlockSpec((1, D_MODEL), lambda i: (0, 0)),
            _resident((D_MODEL, n)),
            pl.BlockSpec((2, LANES, GLA_KEY), lambda i: (0, 0, 0)),
            pl.BlockSpec((1, GLA_KEY), lambda i: (0, 0)),
        ],
        out_specs=[row_spec(GLA_KEY), row_spec(GLA_KEY), row_spec(GLA_VAL), row_spec(GLA_VAL),
                   row_spec(GLA_KEY)],
        out_shape=[jax.ShapeDtypeStruct((rows, GLA_KEY), BF16),
                   jax.ShapeDtypeStruct((rows, GLA_KEY), BF16),
                   jax.ShapeDtypeStruct((rows, GLA_VAL), BF16),
                   jax.ShapeDtypeStruct((rows, GLA_VAL), BF16),
                   jax.ShapeDtypeStruct((rows, GLA_KEY), F32)],
        compiler_params=_cparams(("parallel",)),
        name="gla_proj",
    )(x, gain, w, w2, b2)


def _affine_scan8(a, b, row):
    for s in (1, 2, 4):
        m = row >= s
        a_sh = pltpu.roll(a, s, 0)
        b_sh = pltpu.roll(b, s, 0)
        b = jnp.where(m, a * b_sh + b, b)
        a = jnp.where(m, a * a_sh, a)
    return a, b


def _lru_kernel(gate_ref, rec_ref, cw_ref, cb_ref, wa_ref, ba_ref, wx_ref, bx_ref, sp_ref,
                conv0_ref, h0_ref, out_ref, conv_t_ref, h_t_ref, xbuf, xcbuf, abuf, bbuf, hprev, hcar,
                *, tl, permuted):
    @pl.when(pl.program_id(1) == 0)
    def _():
        xbuf[0:SUBLANES, :] = conv0_ref[...]
        hcar[...] = h0_ref[...]

    row = lax.broadcasted_iota(jnp.int32, (SUBLANES, D_MODEL), 0)
    ns = PERM_BLOCK // CHUNK_STEPS
    slab = lambda b, j: slice(b * PERM_BLOCK + j * ns, b * PERM_BLOCK + (j + 1) * ns)

    if permuted:
        srow = lax.broadcasted_iota(jnp.int32, (ns, D_MODEL), 0)
        for b in range(tl // PERM_BLOCK):
            x = [rec_ref[slab(b, j), :].astype(F32) for j in range(CHUNK_STEPS)]
            back = {j: jnp.where(srow == 0, xbuf[j:j + 1, :], pltpu.roll(x[j], 1, 0))
                    for j in range(CHUNK_STEPS - 3, CHUNK_STEPS)}
            for j in range(CHUNK_STEPS):
                xc = cb_ref[...]
                for k in range(4):
                    src = x[j - k] if j >= k else back[j - k + CHUNK_STEPS]
                    xc = xc + cw_ref[3 - k:4 - k, :] * src
                xcbuf[slab(b, j), :] = xc
            last = [x[j][ns - 1:ns, :] for j in range(CHUNK_STEPS - 3, CHUNK_STEPS)]
            xbuf[0:SUBLANES, :] = jnp.where(row == 5, last[0], jnp.where(row == 6, last[1], last[2]))
    else:
        xbuf[SUBLANES:SUBLANES + tl, :] = rec_ref[...].astype(F32)
        xc = cb_ref[...]
        for tap in range(4):
            xc = xc + cw_ref[tap:tap + 1, :] * xbuf[SUBLANES - 3 + tap:SUBLANES - 3 + tap + tl, :]
        xcbuf[...] = xc
        xbuf[0:SUBLANES, :] = xbuf[tl:tl + SUBLANES, :]

    for blk in range(LRU_BLOCKS):
        sl = slice(blk * LRU_BLOCK, (blk + 1) * LRU_BLOCK)
        xs = xcbuf[:, sl]
        xb = xs.astype(BF16)
        r = _sigmoid(_dot(xb, wa_ref[blk]) + ba_ref[:, sl])
        i = _sigmoid(_dot(xb, wx_ref[blk]) + bx_ref[:, sl])
        a = jnp.exp(-LRU_C * r * sp_ref[:, sl])
        abuf[:, sl] = a
        v = 1.0 - a * a
        bbuf[:, sl] = jnp.where(v > 0.0, v * lax.rsqrt(v), 0.0) * (i * xs)

    last_row = lambda v: jnp.broadcast_to(v[SUBLANES - 1:SUBLANES, :], (SUBLANES, D_MODEL))
    if permuted:
        carry = hcar[...]
        for b in range(tl // PERM_BLOCK):
            for j in range(1, CHUNK_STEPS):
                a_j = abuf[slab(b, j), :]
                bbuf[slab(b, j), :] = a_j * bbuf[slab(b, j - 1), :] + bbuf[slab(b, j), :]
                abuf[slab(b, j), :] = a_j * abuf[slab(b, j - 1), :]
            top = b * PERM_BLOCK + (CHUNK_STEPS - 1) * ns
            for g in range(ns // SUBLANES):
                rg = slice(top + g * SUBLANES, top + (g + 1) * SUBLANES)
                a, bb = _affine_scan8(abuf[rg, :], bbuf[rg, :], row)
                h = a * carry + bb
                hprev[g * SUBLANES:(g + 1) * SUBLANES, :] = jnp.where(row == 0, carry, pltpu.roll(h, 1, 0))
                carry = last_row(h)
            for j in range(CHUNK_STEPS):
                bbuf[slab(b, j), :] = abuf[slab(b, j), :] * hprev[...] + bbuf[slab(b, j), :]
    else:
        def body(gidx, carry):
            r0 = pl.multiple_of(gidx * SUBLANES, SUBLANES)
            a, b = _affine_scan8(abuf[pl.ds(r0, SUBLANES), :], bbuf[pl.ds(r0, SUBLANES), :], row)
            h = a * carry + b
            bbuf[pl.ds(r0, SUBLANES), :] = h
            return last_row(h)

        carry = lax.fori_loop(0, tl // SUBLANES, body, hcar[...])
    hcar[...] = carry
    out_ref[...] = (_gelu_tanh(gate_ref[...].astype(F32)) * bbuf[...]).astype(out_ref.dtype)
    conv_t_ref[...] = xbuf[0:SUBLANES, :]
    h_t_ref[...] = carry


def _lru(gate, rec, p, conv0, h0, nb, tl, permuted):
    rows = gate.shape[0]
    nt = rows // (nb * tl)
    assert not permuted or tl % PERM_BLOCK == 0
    tile = pl.BlockSpec((tl, D_MODEL), lambda b, t: (b * nt + t, 0))
    full2 = lambda shp: pl.BlockSpec(shp, lambda b, t: (0, 0))
    full3 = lambda shp: pl.BlockSpec(shp, lambda b, t: (0, 0, 0))
    return pl.pallas_call(
        functools.partial(_lru_kernel, tl=tl, permuted=permuted),
        grid=(nb, nt),
        in_specs=[tile, tile,
                  full2((4, D_MODEL)), full2((1, D_MODEL)),
                  full3((LRU_BLOCKS, LRU_BLOCK, LRU_BLOCK)), full2((1, D_MODEL)),
                  full3((LRU_BLOCKS, LRU_BLOCK, LRU_BLOCK)), full2((1, D_MODEL)),
                  full2((1, D_MODEL)),
                  full2((SUBLANES, D_MODEL)), full2((SUBLANES, D_MODEL))],
        out_specs=[tile, full2((SUBLANES, D_MODEL)), full2((SUBLANES, D_MODEL))],
        out_shape=[jax.ShapeDtypeStruct((rows, D_MODEL), BF16),
                   jax.ShapeDtypeStruct((SUBLANES, D_MODEL), F32),
                   jax.ShapeDtypeStruct((SUBLANES, D_MODEL), F32)],
        scratch_shapes=[pltpu.VMEM(((0 if permuted else tl) + SUBLANES, D_MODEL), F32),
                        pltpu.VMEM((tl, D_MODEL), F32),
                        pltpu.VMEM((tl, D_MODEL), F32),
                        pltpu.VMEM((tl, D_MODEL), F32),
                        pltpu.VMEM((PERM_BLOCK // CHUNK_STEPS, D_MODEL), F32),
                        pltpu.VMEM((SUBLANES, D_MODEL), F32)],
        compiler_params=_cparams(("arbitrary", "arbitrary")),
        name="rg_lru",
    )(gate, rec, p["conv_w"], p["conv_b"], p["gate_a_w"], p["gate_a_b"], p["gate_x_w"],
      p["gate_x_b"], p["softplus"], conv0, h0)


SSM_CHUNK = 256


def _complex_scan(xs, pw_ref, hcar, n_rows, lane_pairs, prev=None):
    row = lax.broadcasted_iota(jnp.int32, (SUBLANES, SSM_CHUNK), 0)
    for re0, im0 in lane_pairs:
        re = slice(re0, re0 + SSM_CHUNK)
        im = slice(im0, im0 + SSM_CHUNK)
        tabs = [(pw_ref[k * SUBLANES:(k + 1) * SUBLANES, re], pw_ref[k * SUBLANES:(k + 1) * SUBLANES, im])
                for k in range(4)]

        def body(gidx, carry, re=re, im=im, tabs=tabs):
            cr, ci = carry
            r0 = pl.multiple_of(gidx * SUBLANES, SUBLANES)
            hr = xs[pl.ds(r0, SUBLANES), re]
            hi = xs[pl.ds(r0, SUBLANES), im]
            for k, s in enumerate((1, 2, 4)):
                pr, pi = tabs[k]
                sr = pltpu.roll(hr, s, 0)
                si = pltpu.roll(hi, s, 0)
                hr, hi = hr + (pr * sr - pi * si), hi + (pr * si + pi * sr)
            pr, pi = tabs[3]
            hr, hi = hr + (pr * cr - pi * ci), hi + (pr * ci + pi * cr)
            xs[pl.ds(r0, SUBLANES), re] = hr
            xs[pl.ds(r0, SUBLANES), im] = hi
            if prev is not None:
                prev[pl.ds(r0, SUBLANES), re] = jnp.where(row == 0, cr, pltpu.roll(hr, 1, 0))
                prev[pl.ds(r0, SUBLANES), im] = jnp.where(row == 0, ci, pltpu.roll(hi, 1, 0))
            last = slice(SUBLANES - 1, SUBLANES)
            return (jnp.broadcast_to(hr[last, :], (SUBLANES, SSM_CHUNK)),
                    jnp.broadcast_to(hi[last, :], (SUBLANES, SSM_CHUNK)))

        cr, ci = lax.fori_loop(0, n_rows // SUBLANES, body, (hcar[:, re], hcar[:, im]))
        hcar[:, re] = cr
        hcar[:, im] = ci


def _ssm_kernel(u_ref, bblk_ref, cblk_ref, d_ref, gw_ref, gb_ref, pw_ref, h0_ref,
                out_ref, h_t_ref, xs, hcar, *, tl):
    @pl.when(pl.program_id(1) == 0)
    def _():
        hcar[...] = h0_ref[...]

    u = u_ref[...]
    xs[...] = _dot(u, bblk_ref[...])
    pairs = [(c * SSM_CHUNK, SSM_LANES + c * SSM_CHUNK) for c in range(SSM_LANES // SSM_CHUNK)]
    _complex_scan(xs, pw_ref, hcar, tl, pairs)
    y = _dot(xs[...].astype(BF16), cblk_ref[...]) + d_ref[...] * u.astype(F32)
    g = _gelu_tanh(y)
    out_ref[...] = (g * jax.nn.sigmoid(_dot(g.astype(BF16), gw_ref[...]) + gb_ref[...])).astype(out_ref.dtype)
    h_t_ref[...] = hcar[...]


def _ssm(u, p, h0, nb, tl):
    rows = u.shape[0]
    nt = rows // (nb * tl)
    tile = pl.BlockSpec((tl, SSM_WIDTH), lambda b, t: (b * nt + t, 0))
    full2 = lambda shp: pl.BlockSpec(shp, lambda b, t: (0, 0))
    return pl.pallas_call(
        functools.partial(_ssm_kernel, tl=tl),
        grid=(nb, nt),
        in_specs=[tile,
                  _resident((SSM_WIDTH, 2 * SSM_LANES)), _resident((2 * SSM_LANES, SSM_WIDTH)),
                  full2((1, SSM_WIDTH)), full2((SSM_WIDTH, SSM_WIDTH)), full2((1, SSM_WIDTH)),
                  full2((4 * SUBLANES, 2 * SSM_LANES)), full2((SUBLANES, 2 * SSM_LANES))],
        out_specs=[tile, full2((SUBLANES, 2 * SSM_LANES))],
        out_shape=[jax.ShapeDtypeStruct((rows, SSM_WIDTH), BF16),
                   jax.ShapeDtypeStruct((SUBLANES, 2 * SSM_LANES), F32)],
        scratch_shapes=[pltpu.VMEM((tl, 2 * SSM_LANES), F32),
                        pltpu.VMEM((SUBLANES, 2 * SSM_LANES), F32)],
        compiler_params=_cparams(("arbitrary", "arbitrary")),
        name="s5_ssm",
    )(u, p["bblk"], p["cblk"], p["d"], p["glu_w"], p["glu_b"], p["powers"], h0)


SSM_Q = CHUNK_STEPS
SSM_PB = PERM_BLOCK
SSM_TILES = SSM_WIDTH // LANES
SSM_VW = SSM_Q * LANES


def _ssm_chunk_kernel(u_ref, w2_ref, tz_ref, w1_ref, pw_ref, d_ref, gw_ref, gb_ref,
                      h0_ref, out_ref, h_t_ref, a_scr, xs, hp, ypm, hcar, *, tl):
    nblk = tl // SSM_PB
    ncb = SSM_PB // SSM_Q
    nc = tl // SSM_Q

    @pl.when(pl.program_id(1) == 0)
    def _():
        hcar[...] = h0_ref[...]

    for b in range(nblk):
        for j in range(SSM_Q):
            a_scr[j, b * ncb:(b + 1) * ncb, :] = u_ref[b * SSM_PB + j * ncb:b * SSM_PB + (j + 1) * ncb, :]

    def u_tile(v):
        return jnp.concatenate([a_scr[j, :, v * LANES:(v + 1) * LANES] for j in range(SSM_Q)], axis=1)

    for v in range(SSM_TILES):
        xs[:, v * SSM_VW:(v + 1) * SSM_VW] = _dot(u_tile(v), w2_ref[v])

    half = SSM_VW // 2
    pairs = [(v * SSM_VW + c * SSM_CHUNK, v * SSM_VW + half + c * SSM_CHUNK)
             for v in range(SSM_TILES) for c in range(half // SSM_CHUNK)]
    _complex_scan(xs, pw_ref, hcar, nc, pairs, prev=hp)

    for v in range(SSM_TILES):
        yv = _dot(u_tile(v), tz_ref[v]) + _dot(hp[:, v * SSM_VW:(v + 1) * SSM_VW].astype(BF16), w1_ref[v])
        for b in range(nblk):
            for j in range(SSM_Q):
                ypm[b * SSM_PB + j * ncb:b * SSM_PB + (j + 1) * ncb, v * LANES:(v + 1) * LANES] = (
                    yv[b * ncb:(b + 1) * ncb, j * LANES:(j + 1) * LANES])

    for b in range(nblk):
        rows = slice(b * SSM_PB, (b + 1) * SSM_PB)
        y = ypm[rows, :] + d_ref[...] * u_ref[rows, :].astype(F32)
        g = _gelu_tanh(y)
        out_ref[rows, :] = (g * jax.nn.sigmoid(_dot(g.astype(BF16), gw_ref[...]) + gb_ref[...])).astype(out_ref.dtype)
    h_t_ref[...] = hcar[...]


def _ssm_chunked(u, p, h0, nb, tl):
    rows = u.shape[0]
    nt = rows // (nb * tl)
    assert tl % SSM_PB == 0
    tile = pl.BlockSpec((tl, SSM_WIDTH), lambda b, t: (b * nt + t, 0))
    wspec = _resident((SSM_TILES, SSM_VW, SSM_VW))
    state = (SUBLANES, 2 * SSM_LANES)
    return pl.pallas_call(
        functools.partial(_ssm_chunk_kernel, tl=tl),
        grid=(nb, nt),
        in_specs=[tile, wspec, wspec, wspec,
                  _resident((4 * SUBLANES, 2 * SSM_LANES)), _resident((1, SSM_WIDTH)),
                  _resident((SSM_WIDTH, SSM_WIDTH)), _resident((1, SSM_WIDTH)), _resident(state)],
        out_specs=[tile, pl.BlockSpec(state, lambda b, t: (0, 0))],
        out_shape=[jax.ShapeDtypeStruct((rows, SSM_WIDTH), BF16), jax.ShapeDtypeStruct(state, F32)],
        scratch_shapes=[pltpu.VMEM((SSM_Q, tl // SSM_Q, SSM_WIDTH), BF16),
                        pltpu.VMEM((tl // SSM_Q, 2 * SSM_LANES), F32),
                        pltpu.VMEM((tl // SSM_Q, 2 * SSM_LANES), F32),
                        pltpu.VMEM((tl, SSM_WIDTH), F32),
                        pltpu.VMEM(state, F32)],
        compiler_params=_cparams(("arbitrary", "arbitrary")),
        name="s5_ssm_chunked",
    )(u, p["w2"], p["tz"], p["w1"], p["chunk_powers"], p["d"], p["glu_w"], p["glu_b"], h0)


FFN_CHUNK = 512


def _resident(shape):
    return pl.BlockSpec(shape, lambda *_: (0,) * len(shape), pipeline_mode=pl.Buffered(1))


def _mix_ffn_kernel(a_ref, b_ref, permt_ref, wa_ref, wb_ref, h_ref, g_ref, wg_ref, wu_ref, wd_ref, o_ref,
                    *, permuted):
    if permuted:
        pt = permt_ref[...]
        blocks = [slice(r, r + PERM_BLOCK) for r in range(0, h_ref.shape[0], PERM_BLOCK)]
        a = jnp.concatenate([_dot(pt, a_ref[rows, :]).astype(BF16) for rows in blocks], axis=0)
        b = jnp.concatenate([_dot(pt, b_ref[rows, :]).astype(BF16) for rows in blocks], axis=0)
    else:
        a, b = a_ref[...], b_ref[...]
    h = h_ref[...] + _dot(a, wa_ref[...]) + _dot(b, wb_ref[...])
    o_ref[...] = h
    xn = _rms(h, g_ref[...]).astype(BF16)
    for c in range(wg_ref.shape[1] // FFN_CHUNK):
        sl = slice(c * FFN_CHUNK, (c + 1) * FFN_CHUNK)
        a = (_silu(_dot(xn, wg_ref[:, sl])) * _dot(xn, wu_ref[:, sl])).astype(BF16)
        o_ref[...] += _dot(a, wd_ref[sl, :])


def _mix_ffn(a, b, wa, wb, h, gain, wg, wu, wd, tm, permuted):
    rows = h.shape[0]
    assert not permuted or tm % PERM_BLOCK == 0
    row_spec = lambda c: pl.BlockSpec((tm, c), lambda i: (i, 0))
    return pl.pallas_call(
        functools.partial(_mix_ffn_kernel, permuted=permuted),
        grid=(rows // tm,),
        in_specs=[row_spec(a.shape[1]), row_spec(b.shape[1]), _resident((PERM_BLOCK, PERM_BLOCK)),
                  _resident(wa.shape), _resident(wb.shape),
                  row_spec(D_MODEL), _resident((1, D_MODEL)),
                  _resident(wg.shape), _resident(wu.shape), _resident(wd.shape)],
        out_specs=row_spec(D_MODEL),
        out_shape=jax.ShapeDtypeStruct((rows, D_MODEL), F32),
        compiler_params=_cparams(("parallel",)),
        name="mix_ffn",
    )(a, b, _block_perm().T, wa, wb, h, gain, wg, wu, wd)


def _gla_kernel(q_ref, k_ref, v_ref, og_ref, lg_ref, hn_ref, s0_ref, o_ref, s_t_ref, st, *, tl, ck):
    @pl.when(pl.program_id(1) == 0)
    def _():
        st[...] = s0_ref[...]

    ri = lax.broadcasted_iota(jnp.int32, (ck, ck), 0)
    ci = lax.broadcasted_iota(jnp.int32, (ck, ck), 1)
    causal = ri >= ci
    tri = causal.astype(F32).astype(BF16)
    scale = GLA_DK ** -0.5

    for c in range(tl // ck):
        rows = slice(c * ck, (c + 1) * ck)
        lg_hi, lg_lo = _split_bf16(lg_ref[rows, :])
        b_all = _dot(tri, lg_hi) + _dot(tri, lg_lo)
        for h in range(GLA_HEADS):
            ks = slice(h * GLA_DK, (h + 1) * GLA_DK)
            vs = slice(h * GLA_DV, (h + 1) * GLA_DV)
            b = b_all[:, ks]
            b_last = b[ck - 1:ck, :]
            q = q_ref[rows, ks].astype(F32)
            k = k_ref[rows, ks].astype(F32)
            v = v_ref[rows, vs]
            qd = (q * (scale * jnp.exp(b))).astype(BF16)
            kd = (k * jnp.exp(-b)).astype(BF16)
            kl = (k * jnp.exp(b_last - b)).astype(BF16)
            att = lax.dot_general(qd, kd, (((1,), (1,)), ((), ())), preferred_element_type=F32)
            att = jnp.where(causal, att, 0.0).astype(BF16)
            s_h = st[h]
            o = _dot(att, v) + lax.dot_general(qd, s_h.astype(BF16), (((1,), (1,)), ((), ())),
                                               preferred_element_type=F32)
            st[h] = s_h * jnp.exp(b_last) + lax.dot_general(v, kl, (((0,), (0,)), ((), ())),
                                                            preferred_element_type=F32)
            o = o * lax.rsqrt(jnp.mean(o * o, axis=-1, keepdims=True) + EPS) * hn_ref[:, vs]
            o_ref[rows, vs] = (o * _silu(og_ref[rows, vs].astype(F32))).astype(o_ref.dtype)
    s_t_ref[...] = st[...]


def _gla(q, k, v, og, lg, head_norm, s0, nb, tl, ck):
    rows = q.shape[0]
    nt = rows // (nb * tl)
    tile = lambda c: pl.BlockSpec((tl, c), lambda b, t: (b * nt + t, 0))
    state = pl.BlockSpec((GLA_HEADS, GLA_DV, GLA_DK), lambda b, t: (0, 0, 0))
    return pl.pallas_call(
        functools.partial(_gla_kernel, tl=tl, ck=ck),
        grid=(nb, nt),
        in_specs=[tile(GLA_KEY), tile(GLA_KEY), tile(GLA_VAL), tile(GLA_VAL), tile(GLA_KEY),
                  pl.BlockSpec((1, GLA_VAL), lambda b, t: (0, 0)), state],
        out_specs=[tile(GLA_VAL), state],
        out_shape=[jax.ShapeDtypeStruct((rows, GLA_VAL), BF16),
                   jax.ShapeDtypeStruct((GLA_HEADS, GLA_DV, GLA_DK), F32)],
        scratch_shapes=[pltpu.VMEM((GLA_HEADS, GLA_DV, GLA_DK), F32)],
        compiler_params=_cparams(("arbitrary", "arbitrary")),
        name="gla",
    )(q, k, v, og, lg, head_norm, s0)


TOK_TILE = 512
SEG_ALIGN = 16
LOCAL_ROWS = 1152
FFN_TILE = 512
MOE_CHUNK = 1792
SEG_BITS = (512, 256, 128, 64, 32, 16)
INFO_E1, INFO_E2, INFO_R1, INFO_R2, INFO_W1, INFO_W2 = range(6)
assert LOCAL_ROWS >= 2 * TOK_TILE + N_EXPERTS * (SEG_ALIGN - 1) and LOCAL_ROWS % LANES == 0


def _router_kernel(o_ref, wo_ref, h_ref, g_ref, wr_ref, tri_ref, hout_ref, xn_ref, info_ref, cnt_ref):
    tm = h_ref.shape[0]
    sub = tri_ref.shape[0]
    w_hi = wr_ref[0]
    w_lo = wr_ref[1]
    lane = lax.broadcasted_iota(jnp.int32, (sub, LANES), 1).astype(F32)
    neg = jnp.float32(-jnp.inf)
    count = jnp.zeros((1, LANES), F32)
    for r in range(tm // sub):
        rows = slice(r * sub, (r + 1) * sub)
        h = h_ref[rows, :] + _dot(o_ref[rows, :], wo_ref[...])
        hout_ref[rows, :] = h
        xn = _rms(h, g_ref[...])
        xn_ref[rows, :] = xn.astype(BF16)
        x_hi, x_lo = _split_bf16(xn)
        logits = _dot(x_hi, w_hi) + _dot(x_lo, w_hi) + _dot(x_hi, w_lo)
        logits = jnp.where(lane < N_EXPERTS, logits, neg)
        v1 = jnp.max(logits, axis=-1, keepdims=True)
        i1 = jnp.min(jnp.where(logits == v1, lane, float(LANES)), axis=-1, keepdims=True)
        rest = jnp.where(lane == i1, neg, logits)
        v2 = jnp.max(rest, axis=-1, keepdims=True)
        i2 = jnp.min(jnp.where(rest == v2, lane, float(LANES)), axis=-1, keepdims=True)
        e2 = jnp.exp(v2 - v1)
        w1 = 1.0 / (1.0 + e2)
        w2 = e2 / (1.0 + e2)
        hit1 = lane == i1
        hit2 = lane == i2
        sel = jnp.where(hit1, 1.0, 0.0) + jnp.where(hit2, 1.0, 0.0)
        cum = _dot(tri_ref[...], sel.astype(BF16)) + count
        count = cum[sub - 1:sub, :]
        rank1 = jnp.sum(jnp.where(hit1, cum, 0.0), axis=-1, keepdims=True) - 1.0
        rank2 = jnp.sum(jnp.where(hit2, cum, 0.0), axis=-1, keepdims=True) - 1.0
        info = jnp.zeros_like(logits)
        for col, val in ((INFO_E1, i1), (INFO_E2, i2), (INFO_R1, rank1), (INFO_R2, rank2),
                         (INFO_W1, w1), (INFO_W2, w2)):
            info = jnp.where(lane == col, val, info)
        info_ref[rows, :] = info
    cnt_ref[...] = jnp.broadcast_to(count, (SUBLANES, LANES))


def _router(o, wo, h, gain, wr, tri):
    rows = h.shape[0]
    tm = TOK_TILE
    nt = rows // tm
    row_spec = lambda c: pl.BlockSpec((tm, c), lambda i: (i, 0))
    return pl.pallas_call(
        _router_kernel,
        grid=(nt,),
        in_specs=[row_spec(o.shape[1]), _resident(wo.shape), row_spec(D_MODEL), _resident((1, D_MODEL)),
                  _resident((2, D_MODEL, LANES)), _resident(tri.shape)],
        out_specs=[row_spec(D_MODEL), row_spec(D_MODEL), row_spec(LANES),
                   pl.BlockSpec((SUBLANES, LANES), lambda i: (i, 0))],
        out_shape=[jax.ShapeDtypeStruct((rows, D_MODEL), F32),
                   jax.ShapeDtypeStruct((rows, D_MODEL), BF16),
                   jax.ShapeDtypeStruct((rows, LANES), F32),
                   jax.ShapeDtypeStruct((nt * SUBLANES, LANES), F32)],
        compiler_params=_cparams(("parallel",)),
        name="router",
    )(o, wo, h, gain, wr, tri)


def _segment_copies(tile, loff_ref, slen_ref, goff_ref, make_copy, fn):
    for e in range(N_EXPERTS):
        idx = tile * N_EXPERTS + e
        lo = loff_ref[idx]
        n = slen_ref[idx]
        go = goff_ref[idx]
        for bit in SEG_BITS:
            pos = n & (-2 * bit)

            @pl.when((n & bit) != 0)
            def _(lo=lo, go=go, pos=pos, bit=bit):
                fn(make_copy(pl.multiple_of(lo + pos, SEG_ALIGN), pl.multiple_of(go + pos, SEG_ALIGN), bit))


def _local_rows(info_row, loff_ref, tile, which_e, which_r):
    e = info_row(which_e)
    loc = info_row(which_r)
    for ex in range(N_EXPERTS):
        loc = loc + jnp.where(e == float(ex), loff_ref[tile * N_EXPERTS + ex].astype(F32), 0.0)
    return loc


def _dispatch_kernel(loff_ref, slen_ref, goff_ref, xn_ref, info_ref, zero_ref, xs_ref, buf, sem):
    del zero_ref
    i = pl.program_id(0)
    n = pl.num_programs(0)
    slot = i % 2

    def copy_for(s):
        def make(lo, go, rows):
            return pltpu.make_async_copy(buf.at[s, pl.ds(lo, rows), :], xs_ref.at[pl.ds(go, rows), :],
                                         sem.at[s])
        return make

    wait = lambda c: c.wait()
    start = lambda c: c.start()

    @pl.when(i >= 2)
    def _():
        _segment_copies(i - 2, loff_ref, slen_ref, goff_ref, copy_for(slot), wait)

    info_t = info_ref[...].T
    row_of = lambda r: info_t[r:r + 1, :]
    loc1 = _local_rows(row_of, loff_ref, i, INFO_E1, INFO_R1)
    loc2 = _local_rows(row_of, loff_ref, i, INFO_E2, INFO_R2)
    ridx = lax.broadcasted_iota(jnp.int32, (LOCAL_ROWS, TOK_TILE), 0).astype(F32)
    select = (jnp.where(ridx == loc1, 1.0, 0.0) + jnp.where(ridx == loc2, 1.0, 0.0)).astype(BF16)
    buf[slot] = _dot(select, xn_ref[...]).astype(BF16)
    _segment_copies(i, loff_ref, slen_ref, goff_ref, copy_for(slot), start)

    @pl.when(i == n - 1)
    def _():
        _segment_copies(i, loff_ref, slen_ref, goff_ref, copy_for(slot), wait)

        @pl.when(i >= 1)
        def _():
            _segment_copies(i - 1, loff_ref, slen_ref, goff_ref, copy_for(1 - slot), wait)


def _dispatch(loff, slen, goff, xn, info, n_sort):
    rows = xn.shape[0]
    nt = rows // TOK_TILE
    zeros = jnp.zeros((n_sort, D_MODEL), BF16)
    return pl.pallas_call(
        _dispatch_kernel,
        grid_spec=pltpu.PrefetchScalarGridSpec(
            num_scalar_prefetch=3,
            grid=(nt,),
            in_specs=[pl.BlockSpec((TOK_TILE, D_MODEL), lambda i, *_: (i, 0)),
                      pl.BlockSpec((TOK_TILE, LANES), lambda i, *_: (i, 0)),
                      pl.BlockSpec(memory_space=pl.ANY)],
            out_specs=pl.BlockSpec(memory_space=pl.ANY),
            scratch_shapes=[pltpu.VMEM((2, LOCAL_ROWS, D_MODEL), BF16),
                            pltpu.SemaphoreType.DMA((2,))],
        ),
        out_shape=jax.ShapeDtypeStruct((n_sort, D_MODEL), BF16),
        input_output_aliases={5: 0},
        compiler_params=_cparams(("arbitrary",)),
        name="moe_dispatch",
    )(loff, slen, goff, xn, info, zeros)


def _moe_ffn_kernel(te_ref, tot_ref, xs_ref, wg_ref, wu_ref, wd_ref, y_ref, acc_ref):
    del te_ref
    j = pl.program_id(0)
    f = pl.program_id(1)

    @pl.when(j < tot_ref[0])
    def _():
        @pl.when(f == 0)
        def _():
            acc_ref[...] = jnp.zeros_like(acc_ref)

        x = xs_ref[...]
        a = _silu(_dot(x, wg_ref[0])) * _dot(x, wu_ref[0])
        acc_ref[...] += _dot(a.astype(BF16), wd_ref[0])

        @pl.when(f == pl.num_programs(1) - 1)
        def _():
            y_ref[...] = acc_ref[...].astype(y_ref.dtype)

    @pl.when((j >= tot_ref[0]) & (f == pl.num_programs(1) - 1))
    def _():
        y_ref[...] = jnp.zeros_like(y_ref)


def _moe_ffn(tile_e, total, xs, wg, wu, wd, fc):
    n_sort = xs.shape[0]
    nt = n_sort // FFN_TILE
    nf = wg.shape[2] // fc
    tile_of = lambda j, tot: jnp.minimum(j, tot[0] - 1)
    chunk_of = lambda j, f, tot: jnp.where(j < tot[0], f, nf - 1)
    return pl.pallas_call(
        _moe_ffn_kernel,
        grid_spec=pltpu.PrefetchScalarGridSpec(
            num_scalar_prefetch=2,
            grid=(nt, nf),
            in_specs=[pl.BlockSpec((FFN_TILE, D_MODEL), lambda j, f, te, tot: (tile_of(j, tot), 0)),
                      pl.BlockSpec((1, D_MODEL, fc), lambda j, f, te, tot: (te[j], 0, chunk_of(j, f, tot))),
                      pl.BlockSpec((1, D_MODEL, fc), lambda j, f, te, tot: (te[j], 0, chunk_of(j, f, tot))),
                      pl.BlockSpec((1, fc, D_MODEL), lambda j, f, te, tot: (te[j], chunk_of(j, f, tot), 0))],
            out_specs=pl.BlockSpec((FFN_TILE, D_MODEL), lambda j, f, te, tot: (j, 0)),
            scratch_shapes=[pltpu.VMEM((FFN_TILE, D_MODEL), F32)],
        ),
        out_shape=jax.ShapeDtypeStruct((n_sort, D_MODEL), BF16),
        compiler_params=_cparams(("arbitrary", "arbitrary")),
        name="moe_ffn",
    )(tile_e, total, xs, wg, wu, wd)


def _combine_kernel(loff_ref, slen_ref, goff_ref, info_ref, h_ref, fn_ref, y_ref, o_ref, buf, sem):
    i = pl.program_id(0)
    n = pl.num_programs(0)
    slot = i % 2

    def copy_for(s):
        def make(lo, go, rows):
            return pltpu.make_async_copy(y_ref.at[pl.ds(go, rows), :], buf.at[s, pl.ds(lo, rows), :],
                                         sem.at[s])
        return make

    wait = lambda c: c.wait()
    start = lambda c: c.start()

    @pl.when(i == 0)
    def _():
        buf[...] = jnp.zeros_like(buf)
        _segment_copies(i, loff_ref, slen_ref, goff_ref, copy_for(slot), start)

    @pl.when(i + 1 < n)
    def _():
        _segment_copies(i + 1, loff_ref, slen_ref, goff_ref, copy_for(1 - slot), start)

    _segment_copies(i, loff_ref, slen_ref, goff_ref, copy_for(slot), wait)

    ys = buf[slot]
    info = info_ref[...]
    col_of = lambda c: info[:, c:c + 1]
    loc1 = _local_rows(col_of, loff_ref, i, INFO_E1, INFO_R1)
    loc2 = _local_rows(col_of, loff_ref, i, INFO_E2, INFO_R2)
    cidx = lax.broadcasted_iota(jnp.int32, (TOK_TILE, LOCAL_ROWS), 1).astype(F32)
    pt = (jnp.where(cidx == loc1, col_of(INFO_W1), 0.0) + jnp.where(cidx == loc2, col_of(INFO_W2), 0.0)).astype(BF16)
    o_ref[...] = _rms(h_ref[...] + _dot(pt, ys), fn_ref[...])


def _combine(loff, slen, goff, info, h, final_norm, y):
    rows = h.shape[0]
    nt = rows // TOK_TILE
    return pl.pallas_call(
        _combine_kernel,
        grid_spec=pltpu.PrefetchScalarGridSpec(
            num_scalar_prefetch=3,
            grid=(nt,),
            in_specs=[pl.BlockSpec((TOK_TILE, LANES), lambda i, *_: (i, 0)),
                      pl.BlockSpec((TOK_TILE, D_MODEL), lambda i, *_: (i, 0)),
                      pl.BlockSpec((1, D_MODEL), lambda i, *_: (0, 0)),
                      pl.BlockSpec(memory_space=pl.ANY)],
            out_specs=pl.BlockSpec((TOK_TILE, D_MODEL), lambda i, *_: (i, 0)),
            scratch_shapes=[pltpu.VMEM((2, LOCAL_ROWS, D_MODEL), BF16),
                            pltpu.SemaphoreType.DMA((2,))],
        ),
        out_shape=jax.ShapeDtypeStruct((rows, D_MODEL), F32),
        compiler_params=_cparams(("arbitrary",)),
        name="moe_combine",
    )(loff, slen, goff, info, h, final_norm, y)


def _routing_tables(cnt, n_tiles_max):
    slen = (cnt + SEG_ALIGN - 1) // SEG_ALIGN * SEG_ALIGN
    loff = jnp.cumsum(slen, axis=1) - slen
    rows_e = jnp.sum(slen, axis=0)
    rows_pad = (rows_e + FFN_TILE - 1) // FFN_TILE * FFN_TILE
    base = jnp.cumsum(rows_pad) - rows_pad
    goff = base[None, :] + jnp.cumsum(slen, axis=0) - slen
    tile_end = jnp.cumsum(rows_pad // FFN_TILE)
    total = tile_end[-1]
    j = jnp.arange(n_tiles_max, dtype=jnp.int32)
    tile_e = jnp.sum(jnp.minimum(j, total - 1)[:, None] >= tile_end[None, :], axis=1).astype(jnp.int32)
    flat = lambda a: a.reshape(-1).astype(jnp.int32)
    return flat(loff), flat(slen), flat(goff), tile_e, total.reshape(1).astype(jnp.int32)


def _moe(o, wo, h, gain, wr, wg, wu, wd, final_norm):
    rows = h.shape[0]
    assert rows % TOK_TILE == 0
    nt = rows // TOK_TILE
    max_rows = 2 * rows + nt * N_EXPERTS * (SEG_ALIGN - 1) + N_EXPERTS * (FFN_TILE - SEG_ALIGN)
    n_tiles_max = -(-max_rows // FFN_TILE)
    tri = jnp.tril(jnp.ones((PROJ_SUB, PROJ_SUB), BF16))
    h, xn, info, cnt = _router(o, wo, h, gain, wr, tri)
    cnt = cnt.reshape(nt, SUBLANES, LANES)[:, 0, :N_EXPERTS].astype(jnp.int32)
    loff, slen, goff, tile_e, total = _routing_tables(cnt, n_tiles_max)
    xs = _dispatch(loff, slen, goff, xn, info, n_tiles_max * FFN_TILE)
    y = _moe_ffn(tile_e, total, xs, wg, wu, wd, MOE_CHUNK)
    return _combine(loff, slen, goff, info, h, final_norm, y)


def _ssm_params(lam_re, lam_im, log_dt, b_re, b_im, c_re, c_im, d, glu_w, glu_b):
    g, n, hh = SSM_GROUPS, SSM_STATE, SSM_GROUP
    lr, li = lam_re.astype(F32), lam_im.astype(F32)
    dt = jnp.exp(log_dt.astype(F32))[:, None]
    mag = jnp.exp(dt * lr)
    ar, ai = mag * jnp.cos(dt * li), mag * jnp.sin(dt * li)
    den = lr * lr + li * li
    nr = ar - 1.0
    zr = (nr * lr + ai * li) / den
    zi = (ai * lr - nr * li) / den
    br = zr[..., None] * b_re - zi[..., None] * b_im
    bi = zr[..., None] * b_im + zi[..., None] * b_re
    eye = jnp.eye(g, dtype=F32)
    bblk = jnp.concatenate(
        [jnp.einsum("gnh,gk->ghkn", x, eye).reshape(SSM_WIDTH, SSM_LANES) for x in (br, bi)], axis=1)
    cblk = jnp.concatenate(
        [jnp.einsum("ghn,gk->gnkh", x, eye).reshape(SSM_LANES, SSM_WIDTH) for x in (c_re, -c_im)], axis=0)

    sub = jnp.arange(SUBLANES, dtype=jnp.int32)[:, None]

    def scan_tables(base_r, base_i, layout):
        def power(m):
            pr, pi = jnp.ones((SUBLANES, g, n), F32), jnp.zeros((SUBLANES, g, n), F32)
            for step in range(1, SUBLANES + 1):
                nr_, ni_ = pr * base_r - pi * base_i, pr * base_i + pi * base_r
                take = (m >= step)[:, :, None]
                pr, pi = jnp.where(take, nr_, pr), jnp.where(take, ni_, pi)
            return pr, pi

        tabs = []
        for s in (1, 2, 4):
            pr, pi = power(jnp.full((SUBLANES, 1), s, jnp.int32))
            keep = (sub >= s).astype(F32)[:, :, None]
            tabs.append(layout(pr * keep, pi * keep))
        tabs.append(layout(*power(sub + 1)))
        return jnp.concatenate(tabs, axis=0)

    flat = lambda pr, pi: jnp.concatenate([pr.reshape(SUBLANES, SSM_LANES), pi.reshape(SUBLANES, SSM_LANES)], axis=1)

    q, nv, gl = SSM_Q, SSM_TILES, SSM_GROUPS // SSM_TILES
    pw_r, pw_i = [jnp.ones_like(ar)], [jnp.zeros_like(ar)]
    for _ in range(q):
        pw_r, pw_i = pw_r + [pw_r[-1] * ar - pw_i[-1] * ai], pw_i + [pw_r[-1] * ai + pw_i[-1] * ar]
    pw_r, pw_i = jnp.stack(pw_r), jnp.stack(pw_i)
    cr, ci = c_re.astype(F32), c_im.astype(F32)

    def expand(compact, row_w, col_w):
        s_idx = jnp.arange(LANES)[:, None]
        c_idx = jnp.arange(SSM_VW)[None, :]
        onehot = (s_idx == (c_idx // (gl * col_w)) * col_w + c_idx % col_w).astype(BF16)
        full = jnp.einsum("vrs,sc->vrc", compact.astype(BF16), onehot, preferred_element_type=F32)
        r_idx = jnp.arange(SSM_VW)[:, None]
        same_group = (r_idx // row_w) % gl == (c_idx // col_w) % gl
        return jnp.where(same_group[None], full, 0.0).astype(BF16)

    tiled = lambda pr, pi: jnp.stack([pr, pi], axis=1).reshape(SUBLANES, 2, nv, gl, n).transpose(
        0, 2, 1, 3, 4).reshape(SUBLANES, 2 * SSM_LANES)
    crt, cit = cr.transpose(0, 2, 1)[:, :, :, None], ci.transpose(0, 2, 1)[:, :, :, None]
    m_r = crt * br[:, :, None, :] - cit * bi[:, :, None, :]
    m_i = crt * bi[:, :, None, :] + cit * br[:, :, None, :]
    kern = [jnp.sum(pw_r[dd][:, :, None, None] * m_r - pw_i[dd][:, :, None, None] * m_i, axis=1)
            for dd in range(q)]
    none = jnp.zeros_like(kern[0])
    kf = jnp.stack([jnp.stack([kern[o - i] if o >= i else none for o in range(q)]) for i in range(q)])
    kf = kf.transpose(2, 0, 1, 3, 4)
    kf = kf.reshape(nv, gl, q, q, hh, hh).transpose(0, 2, 1, 5, 3, 4)
    tz = expand(kf.reshape(nv, SSM_VW, LANES), hh, hh)
    rev_r = jnp.stack([pw_r[q - 1 - j] for j in range(q)])[:, :, :, None]
    rev_i = jnp.stack([pw_i[q - 1 - j] for j in range(q)])[:, :, :, None]
    wb = jnp.stack([rev_r * br[None] - rev_i * bi[None], rev_r * bi[None] + rev_i * br[None]])
    wb = wb.reshape(2, q, nv, gl, n, hh).transpose(2, 1, 3, 5, 0, 4)
    w2 = expand(wb.reshape(nv, SSM_VW, LANES), hh, n)
    nx_r, nx_i = pw_r[1:][:, :, None, :], pw_i[1:][:, :, None, :]
    wc = jnp.stack([cr[None] * nx_r - ci[None] * nx_i, -(cr[None] * nx_i + ci[None] * nx_r)])
    wc = wc.reshape(2, q, nv, gl, hh, n).transpose(2, 0, 3, 5, 1, 4)
    w1 = expand(wc.reshape(nv, SSM_VW, LANES), n, hh)
    return {
        "bblk": bblk.astype(BF16), "cblk": cblk.astype(BF16),
        "d": d.reshape(1, SSM_WIDTH).astype(F32),
        "glu_w": glu_w.astype(BF16), "glu_b": glu_b.reshape(1, SSM_WIDTH).astype(F32),
        "powers": scan_tables(ar, ai, flat),
        "chunk_powers": scan_tables(pw_r[q], pw_i[q], tiled),
        "tz": tz.astype(BF16), "w2": w2.astype(BF16), "w1": w1.astype(BF16),
    }


def _row(x):
    return x.reshape(1, -1).astype(F32)


def _pad_cols(w, n):
    return jnp.pad(w, ((0, 0), (0, n - w.shape[1])))


def kernel(x, meta_tokens, ev_norm_mix, ev_w_in, ev_conv_w, ev_conv_b, ev_gate_a_w, ev_gate_a_b, ev_gate_x_w, ev_gate_x_b, ev_lru_lambda, ev_ssm_lambda_re, ev_ssm_lambda_im, ev_ssm_log_dt, ev_ssm_b_re, ev_ssm_b_im, ev_ssm_c_re, ev_ssm_c_im, ev_ssm_d, ev_ssm_glu_w, ev_ssm_glu_b, ev_w_out, ev_norm_ffn, ev_ffn_w_gate, ev_ffn_w_up, ev_ffn_w_down, od_norm_mix, od_w_in, od_gla_gate_w2, od_gla_gate_b, od_gla_norm, od_w_out, od_norm_ffn, od_router_w, od_moe_w_gate, od_moe_w_up, od_moe_w_down, final_norm):
    nb, seq, _ = x.shape
    rows = nb * seq
    assert ev_w_in.shape[0] == 1 and od_w_in.shape[0] == 1, "two-layer trunk only"

    w_in0 = ev_w_in[0].astype(BF16)
    lru_p = {
        "conv_w": ev_conv_w[0].astype(F32), "conv_b": _row(ev_conv_b[0]),
        "gate_a_w": ev_gate_a_w[0].astype(BF16), "gate_a_b": _row(ev_gate_a_b[0]),
        "gate_x_w": ev_gate_x_w[0].astype(BF16), "gate_x_b": _row(ev_gate_x_b[0]),
        "softplus": _row(jax.nn.softplus(-ev_lru_lambda[0].astype(F32))),
    }
    ssm_p = _ssm_params(ev_ssm_lambda_re[0], ev_ssm_lambda_im[0], ev_ssm_log_dt[0], ev_ssm_b_re[0],
                        ev_ssm_b_im[0], ev_ssm_c_re[0], ev_ssm_c_im[0], ev_ssm_d[0],
                        ev_ssm_glu_w[0], ev_ssm_glu_b[0])
    w_out0 = ev_w_out[0].astype(BF16)
    w_out0_a, w_out0_b = w_out0[:D_MODEL], w_out0[D_MODEL:]
    ffn_g, ffn_u, ffn_d = (w[0].astype(BF16) for w in (ev_ffn_w_gate, ev_ffn_w_up, ev_ffn_w_down))
    odd_in = 2 * GLA_KEY + 2 * GLA_VAL
    w_in1 = _pad_cols(od_w_in[0], odd_in + LANES).astype(BF16)
    w2 = jnp.pad(od_gla_gate_w2[0].astype(F32), ((0, LANES - GLA_RANK), (0, 0)))
    w2_hi = w2.astype(BF16)
    w2_split = jnp.stack([w2_hi, (w2 - w2_hi.astype(F32)).astype(BF16)])
    wr = _pad_cols(od_router_w[0].astype(F32), LANES)
    wr_hi = wr.astype(BF16)
    wr_split = jnp.stack([wr_hi, (wr - wr_hi.astype(F32)).astype(BF16)])
    w_out1 = od_w_out[0].astype(BF16)
    moe_g, moe_u, moe_d = (w[0].astype(BF16) for w in (od_moe_w_gate, od_moe_w_up, od_moe_w_down))

    tl = min(512, seq)
    tm = min(1024, rows)
    tm_ffn = min(512, rows)
    ck = min(128, tl)
    tl_ssm = min(2048, seq)
    even_splits = (D_MODEL, D_MODEL, SSM_WIDTH)

    def even_mixer(h, nbatch, tile_m, tile_f, tile_l, conv0, h0, s0):
        permuted = tile_l % PERM_BLOCK == 0
        gate, rec, u = _norm_proj(h, _row(ev_norm_mix[0]), w_in0, even_splits, tile_m, permuted)
        a_out, conv_t, h_t = _lru(gate, rec, lru_p, conv0, h0, nbatch, tile_l, permuted)
        if permuted:
            b_out, s_t = _ssm_chunked(u, ssm_p, s0, nbatch, tl_ssm)
        else:
            b_out, s_t = _ssm(u, ssm_p, s0, nbatch, tile_l)
            s_t = s_t.reshape(SUBLANES, 2, SSM_TILES, SSM_LANES // SSM_TILES).transpose(0, 2, 1, 3).reshape(
                SUBLANES, 2 * SSM_LANES)
        h = _mix_ffn(a_out, b_out, w_out0_a, w_out0_b, h, _row(ev_norm_ffn[0]), ffn_g, ffn_u, ffn_d, tile_f,
                     permuted)
        return h, (conv_t, h_t, s_t)

    def gla_inputs(h, tile_m):
        return _gla_proj(h, _row(od_norm_mix[0]), w_in1, w2_split, _row(od_gla_gate_b[0]), tile_m)

    zeros = lambda *s: jnp.zeros(s, F32)
    hm = meta_tokens.astype(F32)
    hm, (conv_m, h_m, s_m) = even_mixer(hm, 1, N_META, N_META, N_META, zeros(SUBLANES, D_MODEL),
                                        zeros(SUBLANES, D_MODEL), zeros(SUBLANES, 2 * SSM_LANES))
    qm, km, vm, ogm, lgm = gla_inputs(hm, N_META)
    _, gla_s = _gla(qm, km, vm, ogm, lgm, _row(od_gla_norm[0]),
                    zeros(GLA_HEADS, GLA_DV, GLA_DK), 1, N_META, N_META)

    h = x.reshape(rows, D_MODEL).astype(F32)
    h, _ = even_mixer(h, nb, tm, tm_ffn, tl, conv_m, h_m, s_m)
    q, k, v, og, lg = gla_inputs(h, tm)
    o, _ = _gla(q, k, v, og, lg, _row(od_gla_norm[0]), gla_s, nb, tl, ck)
    out = _moe(o, w_out1, h, _row(od_norm_ffn[0]), wr_split, moe_g, moe_u, moe_d, _row(final_norm))
    return out.reshape(nb, seq, D_MODEL)
```

```python
import functools
import math

import jax
import jax.numpy as jnp
from jax import lax
from jax.experimental import pallas as pl
from jax.experimental.pallas import tpu as pltpu

F32 = jnp.float32
BF16 = jnp.bfloat16

D_MODEL = 1024
N_META = 16
EPS = 1e-6
LRU_BLOCKS = 4
LRU_BLOCK = 256
LRU_C = 8.0
SSM_WIDTH = 512
SSM_GROUP = 16
SSM_GROUPS = 32
SSM_STATE = 64
SSM_LANES = SSM_GROUPS * SSM_STATE
D_FF = 3072
GLA_HEADS = 4
GLA_DK = 128
GLA_DV = 256
GLA_KEY = 512
GLA_VAL = 1024
GLA_RANK = 16
GLA_TAU = 16.0
N_EXPERTS = 8
D_FF_EXPERT = 3584

SUBLANES = 8
LANES = 128
CHUNK_STEPS = 8
PERM_BLOCK = 256
VMEM_LIMIT = 52 * 1024 * 1024


def _cparams(sem):
    return pltpu.CompilerParams(dimension_semantics=sem, vmem_limit_bytes=VMEM_LIMIT)


def _dot(a, b):
    return jnp.dot(a, b, preferred_element_type=F32)


def _rms(x, gain):
    ms = jnp.mean(x * x, axis=-1, keepdims=True)
    return x * lax.rsqrt(ms + EPS) * gain


def _gelu_tanh(x):
    c = math.sqrt(2.0 / math.pi)
    return 0.5 * x * (1.0 + jnp.tanh(c * (x + 0.044715 * (x * x * x))))


def _sigmoid(x):
    return 0.5 * jnp.tanh(0.5 * x) + 0.5


def _silu(x):
    return x * _sigmoid(x)


def _split_bf16(x):
    hi = x.astype(BF16)
    lo = (x - hi.astype(F32)).astype(BF16)
    return hi, lo


def _norm_proj_kernel(x_ref, g_ref, w_ref, perm_ref, *out_refs, permuted):
    tm = x_ref.shape[0]
    sub = min(PERM_BLOCK, tm)
    for r in range(tm // sub):
        rows = slice(r * sub, (r + 1) * sub)
        xn = _rms(x_ref[rows, :], g_ref[...]).astype(BF16)
        if permuted:
            xn = _dot(perm_ref[...], xn).astype(BF16)
        off = 0
        for o_ref in out_refs:
            n = o_ref.shape[-1]
            o_ref[rows, :] = _dot(xn, w_ref[:, off:off + n]).astype(o_ref.dtype)
            off += n


def _block_perm():
    r_out = jnp.arange(PERM_BLOCK)
    chunks = PERM_BLOCK // CHUNK_STEPS
    src = CHUNK_STEPS * (r_out % chunks) + r_out // chunks
    return (jnp.arange(PERM_BLOCK)[None, :] == src[:, None]).astype(BF16)


def _norm_proj(x, gain, w, splits, tm, permuted):
    rows = x.shape[0]
    n = w.shape[1]
    assert sum(splits) == n and rows % tm == 0 and (not permuted or tm % PERM_BLOCK == 0)
    return pl.pallas_call(
        functools.partial(_norm_proj_kernel, permuted=permuted),
        grid=(rows // tm,),
        in_specs=[
            pl.BlockSpec((tm, D_MODEL), lambda i: (i, 0)),
            pl.BlockSpec((1, D_MODEL), lambda i: (0, 0)),
            _resident((D_MODEL, n)),
            _resident((PERM_BLOCK, PERM_BLOCK)),
        ],
        out_specs=[pl.BlockSpec((tm, s), lambda i: (i, 0)) for s in splits],
        out_shape=[jax.ShapeDtypeStruct((rows, s), BF16) for s in splits],
        compiler_params=_cparams(("parallel",)),
        name="norm_proj",
    )(x, gain, w, _block_perm())


PROJ_SUB = 256


def _gla_proj_kernel(x_ref, g_ref, w_ref, w2_ref, b2_ref, q_ref, k_ref, v_ref, og_ref, lg_ref):
    tm = x_ref.shape[0]
    sub = min(PROJ_SUB, tm)
    o0 = 2 * GLA_KEY + GLA_VAL
    for r in range(tm // sub):
        rows = slice(r * sub, (r + 1) * sub)
        xn = _rms(x_ref[rows, :], g_ref[...]).astype(BF16)
        q_ref[rows, :] = _dot(xn, w_ref[:, 0:GLA_KEY]).astype(BF16)
        k_ref[rows, :] = _dot(xn, w_ref[:, GLA_KEY:2 * GLA_KEY]).astype(BF16)
        v_ref[rows, :] = _dot(xn, w_ref[:, 2 * GLA_KEY:o0]).astype(BF16)
        og_ref[rows, :] = _dot(xn, w_ref[:, o0:o0 + GLA_VAL]).astype(BF16)
        glr = _dot(xn, w_ref[:, o0 + GLA_VAL:o0 + GLA_VAL + LANES])
        glr_hi, glr_lo = _split_bf16(glr)
        w2_hi = w2_ref[0]
        w2_lo = w2_ref[1]
        pre = _dot(glr_hi, w2_hi) + _dot(glr_lo, w2_hi) + _dot(glr_hi, w2_lo) + b2_ref[...]
        lg_ref[rows, :] = (jnp.minimum(pre, 0.0) - jnp.log1p(jnp.exp(-jnp.abs(pre)))) * (1.0 / GLA_TAU)


def _gla_proj(x, gain, w, w2, b2, tm):
    rows = x.shape[0]
    n = w.shape[1]
    row_spec = lambda c: pl.BlockSpec((tm, c), lambda i: (i, 0))
    return pl.pallas_call(
        _gla_proj_kernel,
        grid=(rows // tm,),
        in_specs=[
            row_spec(D_MODEL),
            pl.BlockSpec((1, D_MODEL), lambda i: (0, 0)),
            _resident((D_MODEL, n)),
            pl.BlockSpec((2, LANES, GLA_KEY), lambda i: (0, 0, 0)),
            pl.BlockSpec((1, GLA_KEY), lambda i: (0, 0)),
        ],
        out_specs=[row_spec(GLA_KEY), row_spec(GLA_KEY), row_spec(GLA_VAL), row_spec(GLA_VAL),
                   row_spec(GLA_KEY)],
        out_shape=[jax.ShapeDtypeStruct((rows, GLA_KEY), BF16),
                   jax.ShapeDtypeStruct((rows, GLA_KEY), BF16),
                   jax.ShapeDtypeStruct((rows, GLA_VAL), BF16),
                   jax.ShapeDtypeStruct((rows, GLA_VAL), BF16),
                   jax.ShapeDtypeStruct((rows, GLA_KEY), F32)],
        compiler_params=_cparams(("parallel",)),
        name="gla_proj",
    )(x, gain, w, w2, b2)


def _affine_scan8(a, b, row):
    for s in (1, 2, 4):
        m = row >= s
        a_sh = pltpu.roll(a, s, 0)
        b_sh = pltpu.roll(b, s, 0)
        b = jnp.where(m, a * b_sh + b, b)
        a = jnp.where(m, a * a_sh, a)
    return a, b


def _lru_kernel(gate_ref, rec_ref, cw_ref, cb_ref, wa_ref, ba_ref, wx_ref, bx_ref, sp_ref,
                conv0_ref, h0_ref, out_ref, conv_t_ref, h_t_ref, xbuf, xcbuf, abuf, bbuf, hprev, hcar,
                *, tl, permuted):
    @pl.when(pl.program_id(1) == 0)
    def _():
        xbuf[0:SUBLANES, :] = conv0_ref[...]
        hcar[...] = h0_ref[...]

    row = lax.broadcasted_iota(jnp.int32, (SUBLANES, D_MODEL), 0)
    ns = PERM_BLOCK // CHUNK_STEPS
    slab = lambda b, j: slice(b * PERM_BLOCK + j * ns, b * PERM_BLOCK + (j + 1) * ns)

    if permuted:
        srow = lax.broadcasted_iota(jnp.int32, (ns, D_MODEL), 0)
        for b in range(tl // PERM_BLOCK):
            x = [rec_ref[slab(b, j), :].astype(F32) for j in range(CHUNK_STEPS)]
            back = {j: jnp.where(srow == 0, xbuf[j:j + 1, :], pltpu.roll(x[j], 1, 0))
                    for j in range(CHUNK_STEPS - 3, CHUNK_STEPS)}
            for j in range(CHUNK_STEPS):
                xc = cb_ref[...]
                for k in range(4):
                    src = x[j - k] if j >= k else back[j - k + CHUNK_STEPS]
                    xc = xc + cw_ref[3 - k:4 - k, :] * src
                xcbuf[slab(b, j), :] = xc
            last = [x[j][ns - 1:ns, :] for j in range(CHUNK_STEPS - 3, CHUNK_STEPS)]
            xbuf[0:SUBLANES, :] = jnp.where(row == 5, last[0], jnp.where(row == 6, last[1], last[2]))
    else:
        xbuf[SUBLANES:SUBLANES + tl, :] = rec_ref[...].astype(F32)
        xc = cb_ref[...]
        for tap in range(4):
            xc = xc + cw_ref[tap:tap + 1, :] * xbuf[SUBLANES - 3 + tap:SUBLANES - 3 + tap + tl, :]
        xcbuf[...] = xc
        xbuf[0:SUBLANES, :] = xbuf[tl:tl + SUBLANES, :]

    for blk in range(LRU_BLOCKS):
        sl = slice(blk * LRU_BLOCK, (blk + 1) * LRU_BLOCK)
        xs = xcbuf[:, sl]
        xb = xs.astype(BF16)
        r = _sigmoid(_dot(xb, wa_ref[blk]) + ba_ref[:, sl])
        i = _sigmoid(_dot(xb, wx_ref[blk]) + bx_ref[:, sl])
        a = jnp.exp(-LRU_C * r * sp_ref[:, sl])
        abuf[:, sl] = a
        v = 1.0 - a * a
        bbuf[:, sl] = jnp.where(v > 0.0, v * lax.rsqrt(v), 0.0) * (i * xs)

    last_row = lambda v: jnp.broadcast_to(v[SUBLANES - 1:SUBLANES, :], (SUBLANES, D_MODEL))
    if permuted:
        carry = hcar[...]
        for b in range(tl // PERM_BLOCK):
            for j in range(1, CHUNK_STEPS):
                a_j = abuf[slab(b, j), :]
                bbuf[slab(b, j), :] = a_j * bbuf[slab(b, j - 1), :] + bbuf[slab(b, j), :]
                abuf[slab(b, j), :] = a_j * abuf[slab(b, j - 1), :]
            top = b * PERM_BLOCK + (CHUNK_STEPS - 1) * ns
            for g in range(ns // SUBLANES):
                rg = slice(top + g * SUBLANES, top + (g + 1) * SUBLANES)
                a, bb = _affine_scan8(abuf[rg, :], bbuf[rg, :], row)
                h = a * carry + bb
                hprev[g * SUBLANES:(g + 1) * SUBLANES, :] = jnp.where(row == 0, carry, pltpu.roll(h, 1, 0))
                carry = last_row(h)
            for j in range(CHUNK_STEPS):
                bbuf[slab(b, j), :] = abuf[slab(b, j), :] * hprev[...] + bbuf[slab(b, j), :]
    else:
        def body(gidx, carry):
            r0 = pl.multiple_of(gidx * SUBLANES, SUBLANES)
            a, b = _affine_scan8(abuf[pl.ds(r0, SUBLANES), :], bbuf[pl.ds(r0, SUBLANES), :], row)
            h = a * carry + b
            bbuf[pl.ds(r0, SUBLANES), :] = h
            return last_row(h)

        carry = lax.fori_loop(0, tl // SUBLANES, body, hcar[...])
    hcar[...] = carry
    out_ref[...] = (_gelu_tanh(gate_ref[...].astype(F32)) * bbuf[...]).astype(out_ref.dtype)
    conv_t_ref[...] = xbuf[0:SUBLANES, :]
    h_t_ref[...] = carry


def _lru(gate, rec, p, conv0, h0, nb, tl, permuted):
    rows = gate.shape[0]
    nt = rows // (nb * tl)
    assert not permuted or tl % PERM_BLOCK == 0
    tile = pl.BlockSpec((tl, D_MODEL), lambda b, t: (b * nt + t, 0))
    full2 = lambda shp: pl.BlockSpec(shp, lambda b, t: (0, 0))
    full3 = lambda shp: pl.BlockSpec(shp, lambda b, t: (0, 0, 0))
    return pl.pallas_call(
        functools.partial(_lru_kernel, tl=tl, permuted=permuted),
        grid=(nb, nt),
        in_specs=[tile, tile,
                  full2((4, D_MODEL)), full2((1, D_MODEL)),
                  full3((LRU_BLOCKS, LRU_BLOCK, LRU_BLOCK)), full2((1, D_MODEL)),
                  full3((LRU_BLOCKS, LRU_BLOCK, LRU_BLOCK)), full2((1, D_MODEL)),
                  full2((1, D_MODEL)),
                  full2((SUBLANES, D_MODEL)), full2((SUBLANES, D_MODEL))],
        out_specs=[tile, full2((SUBLANES, D_MODEL)), full2((SUBLANES, D_MODEL))],
        out_shape=[jax.ShapeDtypeStruct((rows, D_MODEL), BF16),
                   jax.ShapeDtypeStruct((SUBLANES, D_MODEL), F32),
                   jax.ShapeDtypeStruct((SUBLANES, D_MODEL), F32)],
        scratch_shapes=[pltpu.VMEM(((0 if permuted else tl) + SUBLANES, D_MODEL), F32),
                        pltpu.VMEM((tl, D_MODEL), F32),
                        pltpu.VMEM((tl, D_MODEL), F32),
                        pltpu.VMEM((tl, D_MODEL), F32),
                        pltpu.VMEM((PERM_BLOCK // CHUNK_STEPS, D_MODEL), F32),
                        pltpu.VMEM((SUBLANES, D_MODEL), F32)],
        compiler_params=_cparams(("arbitrary", "arbitrary")),
        name="rg_lru",
    )(gate, rec, p["conv_w"], p["conv_b"], p["gate_a_w"], p["gate_a_b"], p["gate_x_w"],
      p["gate_x_b"], p["softplus"], conv0, h0)


SSM_CHUNK = 256


def _complex_scan(xs, pw_ref, hcar, n_rows, lane_pairs, prev=None):
    row = lax.broadcasted_iota(jnp.int32, (SUBLANES, SSM_CHUNK), 0)
    for re0, im0 in lane_pairs:
        re = slice(re0, re0 + SSM_CHUNK)
        im = slice(im0, im0 + SSM_CHUNK)
        tabs = [(pw_ref[k * SUBLANES:(k + 1) * SUBLANES, re], pw_ref[k * SUBLANES:(k + 1) * SUBLANES, im])
                for k in range(4)]

        def body(gidx, carry, re=re, im=im, tabs=tabs):
            cr, ci = carry
            r0 = pl.multiple_of(gidx * SUBLANES, SUBLANES)
            hr = xs[pl.ds(r0, SUBLANES), re]
            hi = xs[pl.ds(r0, SUBLANES), im]
            for k, s in enumerate((1, 2, 4)):
                pr, pi = tabs[k]
                sr = pltpu.roll(hr, s, 0)
                si = pltpu.roll(hi, s, 0)
                hr, hi = hr + (pr * sr - pi * si), hi + (pr * si + pi * sr)
            pr, pi = tabs[3]
            hr, hi = hr + (pr * cr - pi * ci), hi + (pr * ci + pi * cr)
            xs[pl.ds(r0, SUBLANES), re] = hr
            xs[pl.ds(r0, SUBLANES), im] = hi
            if prev is not None:
                prev[pl.ds(r0, SUBLANES), re] = jnp.where(row == 0, cr, pltpu.roll(hr, 1, 0))
                prev[pl.ds(r0, SUBLANES), im] = jnp.where(row == 0, ci, pltpu.roll(hi, 1, 0))
            last = slice(SUBLANES - 1, SUBLANES)
            return (jnp.broadcast_to(hr[last, :], (SUBLANES, SSM_CHUNK)),
                    jnp.broadcast_to(hi[last, :], (SUBLANES, SSM_CHUNK)))

        cr, ci = lax.fori_loop(0, n_rows // SUBLANES, body, (hcar[:, re], hcar[:, im]))
        hcar[:, re] = cr
        hcar[:, im] = ci


def _ssm_kernel(u_ref, bblk_ref, cblk_ref, d_ref, gw_ref, gb_ref, pw_ref, h0_ref,
                out_ref, h_t_ref, xs, hcar, *, tl):
    @pl.when(pl.program_id(1) == 0)
    def _():
        hcar[...] = h0_ref[...]

    u = u_ref[...]
    xs[...] = _dot(u, bblk_ref[...])
    pairs = [(c * SSM_CHUNK, SSM_LANES + c * SSM_CHUNK) for c in range(SSM_LANES // SSM_CHUNK)]
    _complex_scan(xs, pw_ref, hcar, tl, pairs)
    y = _dot(xs[...].astype(BF16), cblk_ref[...]) + d_ref[...] * u.astype(F32)
    g = _gelu_tanh(y)
    out_ref[...] = (g * jax.nn.sigmoid(_dot(g.astype(BF16), gw_ref[...]) + gb_ref[...])).astype(out_ref.dtype)
    h_t_ref[...] = hcar[...]


def _ssm(u, p, h0, nb, tl):
    rows = u.shape[0]
    nt = rows // (nb * tl)
    tile = pl.BlockSpec((tl, SSM_WIDTH), lambda b, t: (b * nt + t, 0))
    full2 = lambda shp: pl.BlockSpec(shp, lambda b, t: (0, 0))
    return pl.pallas_call(
        functools.partial(_ssm_kernel, tl=tl),
        grid=(nb, nt),
        in_specs=[tile,
                  _resident((SSM_WIDTH, 2 * SSM_LANES)), _resident((2 * SSM_LANES, SSM_WIDTH)),
                  full2((1, SSM_WIDTH)), full2((SSM_WIDTH, SSM_WIDTH)), full2((1, SSM_WIDTH)),
                  full2((4 * SUBLANES, 2 * SSM_LANES)), full2((SUBLANES, 2 * SSM_LANES))],
        out_specs=[tile, full2((SUBLANES, 2 * SSM_LANES))],
        out_shape=[jax.ShapeDtypeStruct((rows, SSM_WIDTH), BF16),
                   jax.ShapeDtypeStruct((SUBLANES, 2 * SSM_LANES), F32)],
        scratch_shapes=[pltpu.VMEM((tl, 2 * SSM_LANES), F32),
                        pltpu.VMEM((SUBLANES, 2 * SSM_LANES), F32)],
        compiler_params=_cparams(("arbitrary", "arbitrary")),
        name="s5_ssm",
    )(u, p["bblk"], p["cblk"], p["d"], p["glu_w"], p["glu_b"], p["powers"], h0)


SSM_Q = CHUNK_STEPS
SSM_PB = PERM_BLOCK
SSM_TILES = SSM_WIDTH // LANES
SSM_VW = SSM_Q * LANES


def _ssm_chunk_kernel(u_ref, w2_ref, tz_ref, w1_ref, pw_ref, d_ref, gw_ref, gb_ref,
                      h0_ref, out_ref, h_t_ref, a_scr, xs, hp, ypm, hcar, *, tl):
    nblk = tl // SSM_PB
    ncb = SSM_PB // SSM_Q
    nc = tl // SSM_Q

    @pl.when(pl.program_id(1) == 0)
    def _():
        hcar[...] = h0_ref[...]

    for b in range(nblk):
        for j in range(SSM_Q):
            a_scr[j, b * ncb:(b + 1) * ncb, :] = u_ref[b * SSM_PB + j * ncb:b * SSM_PB + (j + 1) * ncb, :]

    def u_tile(v):
        return jnp.concatenate([a_scr[j, :, v * LANES:(v + 1) * LANES] for j in range(SSM_Q)], axis=1)

    for v in range(SSM_TILES):
        xs[:, v * SSM_VW:(v + 1) * SSM_VW] = _dot(u_tile(v), w2_ref[v])

    half = SSM_VW // 2
    pairs = [(v * SSM_VW + c * SSM_CHUNK, v * SSM_VW + half + c * SSM_CHUNK)
             for v in range(SSM_TILES) for c in range(half // SSM_CHUNK)]
    _complex_scan(xs, pw_ref, hcar, nc, pairs, prev=hp)

    for v in range(SSM_TILES):
        yv = _dot(u_tile(v), tz_ref[v]) + _dot(hp[:, v * SSM_VW:(v + 1) * SSM_VW].astype(BF16), w1_ref[v])
        for b in range(nblk):
            for j in range(SSM_Q):
                ypm[b * SSM_PB + j * ncb:b * SSM_PB + (j + 1) * ncb, v * LANES:(v + 1) * LANES] = (
                    yv[b * ncb:(b + 1) * ncb, j * LANES:(j + 1) * LANES])

    for b in range(nblk):
        rows = slice(b * SSM_PB, (b + 1) * SSM_PB)
        y = ypm[rows, :] + d_ref[...] * u_ref[rows, :].astype(F32)
        g = _gelu_tanh(y)
        out_ref[rows, :] = (g * jax.nn.sigmoid(_dot(g.astype(BF16), gw_ref[...]) + gb_ref[...])).astype(out_ref.dtype)
    h_t_ref[...] = hcar[...]


def _ssm_chunked(u, p, h0, nb, tl):
    rows = u.shape[0]
    nt = rows // (nb * tl)
    assert tl % SSM_PB == 0
    tile = pl.BlockSpec((tl, SSM_WIDTH), lambda b, t: (b * nt + t, 0))
    wspec = _resident((SSM_TILES, SSM_VW, SSM_VW))
    state = (SUBLANES, 2 * SSM_LANES)
    return pl.pallas_call(
        functools.partial(_ssm_chunk_kernel, tl=tl),
        grid=(nb, nt),
        in_specs=[tile, wspec, wspec, wspec,
                  _resident((4 * SUBLANES, 2 * SSM_LANES)), _resident((1, SSM_WIDTH)),
                  _resident((SSM_WIDTH, SSM_WIDTH)), _resident((1, SSM_WIDTH)), _resident(state)],
        out_specs=[tile, pl.BlockSpec(state, lambda b, t: (0, 0))],
        out_shape=[jax.ShapeDtypeStruct((rows, SSM_WIDTH), BF16), jax.ShapeDtypeStruct(state, F32)],
        scratch_shapes=[pltpu.VMEM((SSM_Q, tl // SSM_Q, SSM_WIDTH), BF16),
                        pltpu.VMEM((tl // SSM_Q, 2 * SSM_LANES), F32),
                        pltpu.VMEM((tl // SSM_Q, 2 * SSM_LANES), F32),
                        pltpu.VMEM((tl, SSM_WIDTH), F32),
                        pltpu.VMEM(state, F32)],
        compiler_params=_cparams(("arbitrary", "arbitrary")),
        name="s5_ssm_chunked",
    )(u, p["w2"], p["tz"], p["w1"], p["chunk_powers"], p["d"], p["glu_w"], p["glu_b"], h0)


FFN_CHUNK = 512


def _resident(shape):
    return pl.BlockSpec(shape, lambda *_: (0,) * len(shape), pipeline_mode=pl.Buffered(1))


def _mix_ffn_kernel(a_ref, b_ref, permt_ref, wa_ref, wb_ref, h_ref, g_ref, wg_ref, wu_ref, wd_ref, o_ref,
                    *, permuted):
    if permuted:
        pt = permt_ref[...]
        blocks = [slice(r, r + PERM_BLOCK) for r in range(0, h_ref.shape[0], PERM_BLOCK)]
        a = jnp.concatenate([_dot(pt, a_ref[rows, :]).astype(BF16) for rows in blocks], axis=0)
        b = jnp.concatenate([_dot(pt, b_ref[rows, :]).astype(BF16) for rows in blocks], axis=0)
    else:
        a, b = a_ref[...], b_ref[...]
    h = h_ref[...] + _dot(a, wa_ref[...]) + _dot(b, wb_ref[...])
    o_ref[...] = h
    xn = _rms(h, g_ref[...]).astype(BF16)
    for c in range(wg_ref.shape[1] // FFN_CHUNK):
        sl = slice(c * FFN_CHUNK, (c + 1) * FFN_CHUNK)
        a = (_silu(_dot(xn, wg_ref[:, sl])) * _dot(xn, wu_ref[:, sl])).astype(BF16)
        o_ref[...] += _dot(a, wd_ref[sl, :])


def _mix_ffn(a, b, wa, wb, h, gain, wg, wu, wd, tm, permuted):
    rows = h.shape[0]
    assert not permuted or tm % PERM_BLOCK == 0
    row_spec = lambda c: pl.BlockSpec((tm, c), lambda i: (i, 0))
    return pl.pallas_call(
        functools.partial(_mix_ffn_kernel, permuted=permuted),
        grid=(rows // tm,),
        in_specs=[row_spec(a.shape[1]), row_spec(b.shape[1]), _resident((PERM_BLOCK, PERM_BLOCK)),
                  _resident(wa.shape), _resident(wb.shape),
                  row_spec(D_MODEL), _resident((1, D_MODEL)),
                  _resident(wg.shape), _resident(wu.shape), _resident(wd.shape)],
        out_specs=row_spec(D_MODEL),
        out_shape=jax.ShapeDtypeStruct((rows, D_MODEL), F32),
        compiler_params=_cparams(("parallel",)),
        name="mix_ffn",
    )(a, b, _block_perm().T, wa, wb, h, gain, wg, wu, wd)


def _gla_kernel(q_ref, k_ref, v_ref, og_ref, lg_ref, hn_ref, s0_ref, o_ref, s_t_ref, st, *, tl, ck):
    @pl.when(pl.program_id(1) == 0)
    def _():
        st[...] = s0_ref[...]

    ri = lax.broadcasted_iota(jnp.int32, (ck, ck), 0)
    ci = lax.broadcasted_iota(jnp.int32, (ck, ck), 1)
    causal = ri >= ci
    tri = causal.astype(F32).astype(BF16)
    scale = GLA_DK ** -0.5

    for c in range(tl // ck):
        rows = slice(c * ck, (c + 1) * ck)
        lg_hi, lg_lo = _split_bf16(lg_ref[rows, :])
        b_all = _dot(tri, lg_hi) + _dot(tri, lg_lo)
        for h in range(GLA_HEADS):
            ks = slice(h * GLA_DK, (h + 1) * GLA_DK)
            vs = slice(h * GLA_DV, (h + 1) * GLA_DV)
            b = b_all[:, ks]
            b_last = b[ck - 1:ck, :]
            q = q_ref[rows, ks].astype(F32)
            k = k_ref[rows, ks].astype(F32)
            v = v_ref[rows, vs]
            qd = (q * (scale * jnp.exp(b))).astype(BF16)
            kd = (k * jnp.exp(-b)).astype(BF16)
            kl = (k * jnp.exp(b_last - b)).astype(BF16)
            att = lax.dot_general(qd, kd, (((1,), (1,)), ((), ())), preferred_element_type=F32)
            att = jnp.where(causal, att, 0.0).astype(BF16)
            s_h = st[h]
            o = _dot(att, v) + lax.dot_general(qd, s_h.astype(BF16), (((1,), (1,)), ((), ())),
                                               preferred_element_type=F32)
            st[h] = s_h * jnp.exp(b_last) + lax.dot_general(v, kl, (((0,), (0,)), ((), ())),
                                                            preferred_element_type=F32)
            o = o * lax.rsqrt(jnp.mean(o * o, axis=-1, keepdims=True) + EPS) * hn_ref[:, vs]
            o_ref[rows, vs] = (o * _silu(og_ref[rows, vs].astype(F32))).astype(o_ref.dtype)
    s_t_ref[...] = st[...]


def _gla(q, k, v, og, lg, head_norm, s0, nb, tl, ck):
    rows = q.shape[0]
    nt = rows // (nb * tl)
    tile = lambda c: pl.BlockSpec((tl, c), lambda b, t: (b * nt + t, 0))
    state = pl.BlockSpec((GLA_HEADS, GLA_DV, GLA_DK), lambda b, t: (0, 0, 0))
    return pl.pallas_call(
        functools.partial(_gla_kernel, tl=tl, ck=ck),
        grid=(nb, nt),
        in_specs=[tile(GLA_KEY), tile(GLA_KEY), tile(GLA_VAL), tile(GLA_VAL), tile(GLA_KEY),
                  pl.BlockSpec((1, GLA_VAL), lambda b, t: (0, 0)), state],
        out_specs=[tile(GLA_VAL), state],
        out_shape=[jax.ShapeDtypeStruct((rows, GLA_VAL), BF16),
                   jax.ShapeDtypeStruct((GLA_HEADS, GLA_DV, GLA_DK), F32)],
        scratch_shapes=[pltpu.VMEM((GLA_HEADS, GLA_DV, GLA_DK), F32)],
        compiler_params=_cparams(("arbitrary", "arbitrary")),
        name="gla",
    )(q, k, v, og, lg, head_norm, s0)


TOK_TILE = 512
SEG_ALIGN = 16
LOCAL_ROWS = 1152
FFN_TILE = 512
MOE_CHUNK = 1792
SEG_BITS = (512, 256, 128, 64, 32, 16)
INFO_E1, INFO_E2, INFO_R1, INFO_R2, INFO_W1, INFO_W2 = range(6)
assert LOCAL_ROWS >= 2 * TOK_TILE + N_EXPERTS * (SEG_ALIGN - 1) and LOCAL_ROWS % LANES == 0


def _router_kernel(o_ref, wo_ref, h_ref, g_ref, wr_ref, tri_ref, hout_ref, xn_ref, info_ref, cnt_ref):
    tm = h_ref.shape[0]
    sub = tri_ref.shape[0]
    w_both = wr_ref[...]
    w_hi = wr_ref[:, 0:LANES]
    lane = lax.broadcasted_iota(jnp.int32, (sub, LANES), 1).astype(F32)
    neg = jnp.float32(-jnp.inf)
    count = jnp.zeros((1, LANES), F32)
    for r in range(tm // sub):
        rows = slice(r * sub, (r + 1) * sub)
        h = h_ref[rows, :] + _dot(o_ref[rows, :], wo_ref[...])
        hout_ref[rows, :] = h
        xn = _rms(h, g_ref[...])
        xn_ref[rows, :] = xn.astype(BF16)
        x_hi, x_lo = _split_bf16(xn)
        both = _dot(x_hi, w_both)
        logits = both[:, 0:LANES] + both[:, LANES:2 * LANES] + _dot(x_lo, w_hi)
        logits = jnp.where(lane < N_EXPERTS, logits, neg)
        v1 = jnp.max(logits, axis=-1, keepdims=True)
        i1 = jnp.min(jnp.where(logits == v1, lane, float(LANES)), axis=-1, keepdims=True)
        rest = jnp.where(lane == i1, neg, logits)
        v2 = jnp.max(rest, axis=-1, keepdims=True)
        i2 = jnp.min(jnp.where(rest == v2, lane, float(LANES)), axis=-1, keepdims=True)
        e2 = jnp.exp(v2 - v1)
        w1 = 1.0 / (1.0 + e2)
        w2 = e2 / (1.0 + e2)
        hit1 = lane == i1
        hit2 = lane == i2
        sel = jnp.where(hit1, 1.0, 0.0) + jnp.where(hit2, 1.0, 0.0)
        cum = _dot(tri_ref[...], sel.astype(BF16)) + count
        count = cum[sub - 1:sub, :]
        rank1 = jnp.sum(jnp.where(hit1, cum, 0.0), axis=-1, keepdims=True) - 1.0
        rank2 = jnp.sum(jnp.where(hit2, cum, 0.0), axis=-1, keepdims=True) - 1.0
        info = jnp.zeros_like(logits)
        for col, val in ((INFO_E1, i1), (INFO_E2, i2), (INFO_R1, rank1), (INFO_R2, rank2),
                         (INFO_W1, w1), (INFO_W2, w2)):
            info = jnp.where(lane == col, val, info)
        info_ref[rows, :] = info
    cnt_ref[...] = jnp.broadcast_to(count, (SUBLANES, LANES))


def _router(o, wo, h, gain, wr, tri):
    rows = h.shape[0]
    tm = TOK_TILE
    nt = rows // tm
    row_spec = lambda c: pl.BlockSpec((tm, c), lambda i: (i, 0))
    return pl.pallas_call(
        _router_kernel,
        grid=(nt,),
        in_specs=[row_spec(o.shape[1]), _resident(wo.shape), row_spec(D_MODEL), _resident((1, D_MODEL)),
                  _resident((D_MODEL, 2 * LANES)), _resident(tri.shape)],
        out_specs=[row_spec(D_MODEL), row_spec(D_MODEL), row_spec(LANES),
                   pl.BlockSpec((SUBLANES, LANES), lambda i: (i, 0))],
        out_shape=[jax.ShapeDtypeStruct((rows, D_MODEL), F32),
                   jax.ShapeDtypeStruct((rows, D_MODEL), BF16),
                   jax.ShapeDtypeStruct((rows, LANES), F32),
                   jax.ShapeDtypeStruct((nt * SUBLANES, LANES), F32)],
        compiler_params=_cparams(("parallel",)),
        name="router",
    )(o, wo, h, gain, wr, tri)


def _segment_copies(tile, loff_ref, slen_ref, goff_ref, make_copy, fn):
    for e in range(N_EXPERTS):
        idx = tile * N_EXPERTS + e
        lo = loff_ref[idx]
        n = slen_ref[idx]
        go = goff_ref[idx]
        for bit in SEG_BITS:
            pos = n & (-2 * bit)

            @pl.when((n & bit) != 0)
            def _(lo=lo, go=go, pos=pos, bit=bit):
                fn(make_copy(pl.multiple_of(lo + pos, SEG_ALIGN), pl.multiple_of(go + pos, SEG_ALIGN), bit))


def _local_rows(info_row, loff_ref, tile, which_e, which_r):
    e = info_row(which_e)
    loc = info_row(which_r)
    for ex in range(N_EXPERTS):
        loc = loc + jnp.where(e == float(ex), loff_ref[tile * N_EXPERTS + ex].astype(F32), 0.0)
    return loc


def _dispatch_kernel(loff_ref, slen_ref, goff_ref, xn_ref, info_ref, zero_ref, xs_ref, buf, sem):
    del zero_ref
    i = pl.program_id(0)
    n = pl.num_programs(0)
    slot = i % 2

    def copy_for(s):
        def make(lo, go, rows):
            return pltpu.make_async_copy(buf.at[s, pl.ds(lo, rows), :], xs_ref.at[pl.ds(go, rows), :],
                                         sem.at[s])
        return make

    wait = lambda c: c.wait()
    start = lambda c: c.start()

    @pl.when(i >= 2)
    def _():
        _segment_copies(i - 2, loff_ref, slen_ref, goff_ref, copy_for(slot), wait)

    info_t = info_ref[...].T
    row_of = lambda r: info_t[r:r + 1, :]
    loc1 = _local_rows(row_of, loff_ref, i, INFO_E1, INFO_R1)
    loc2 = _local_rows(row_of, loff_ref, i, INFO_E2, INFO_R2)
    ridx = lax.broadcasted_iota(jnp.int32, (LOCAL_ROWS, TOK_TILE), 0).astype(F32)
    select = (jnp.where(ridx == loc1, 1.0, 0.0) + jnp.where(ridx == loc2, 1.0, 0.0)).astype(BF16)
    buf[slot] = _dot(select, xn_ref[...]).astype(BF16)
    _segment_copies(i, loff_ref, slen_ref, goff_ref, copy_for(slot), start)

    @pl.when(i == n - 1)
    def _():
        _segment_copies(i, loff_ref, slen_ref, goff_ref, copy_for(slot), wait)

        @pl.when(i >= 1)
        def _():
            _segment_copies(i - 1, loff_ref, slen_ref, goff_ref, copy_for(1 - slot), wait)


def _dispatch(loff, slen, goff, xn, info, n_sort):
    rows = xn.shape[0]
    nt = rows // TOK_TILE
    zeros = jnp.zeros((n_sort, D_MODEL), BF16)
    return pl.pallas_call(
        _dispatch_kernel,
        grid_spec=pltpu.PrefetchScalarGridSpec(
            num_scalar_prefetch=3,
            grid=(nt,),
            in_specs=[pl.BlockSpec((TOK_TILE, D_MODEL), lambda i, *_: (i, 0)),
                      pl.BlockSpec((TOK_TILE, LANES), lambda i, *_: (i, 0)),
                      pl.BlockSpec(memory_space=pl.ANY)],
            out_specs=pl.BlockSpec(memory_space=pl.ANY),
            scratch_shapes=[pltpu.VMEM((2, LOCAL_ROWS, D_MODEL), BF16),
                            pltpu.SemaphoreType.DMA((2,))],
        ),
        out_shape=jax.ShapeDtypeStruct((n_sort, D_MODEL), BF16),
        input_output_aliases={5: 0},
        compiler_params=_cparams(("arbitrary",)),
        name="moe_dispatch",
    )(loff, slen, goff, xn, info, zeros)


def _moe_ffn_kernel(te_ref, tot_ref, xs_ref, wg_ref, wu_ref, wd_ref, y_ref, acc_ref):
    del te_ref
    j = pl.program_id(0)
    f = pl.program_id(1)

    @pl.when(j < tot_ref[0])
    def _():
        @pl.when(f == 0)
        def _():
            acc_ref[...] = jnp.zeros_like(acc_ref)

        x = xs_ref[...]
        a = _silu(_dot(x, wg_ref[0])) * _dot(x, wu_ref[0])
        acc_ref[...] += _dot(a.astype(BF16), wd_ref[0])

        @pl.when(f == pl.num_programs(1) - 1)
        def _():
            y_ref[...] = acc_ref[...].astype(y_ref.dtype)

    @pl.when((j >= tot_ref[0]) & (f == pl.num_programs(1) - 1))
    def _():
        y_ref[...] = jnp.zeros_like(y_ref)


def _moe_ffn(tile_e, total, xs, wg, wu, wd, fc):
    n_sort = xs.shape[0]
    nt = n_sort // FFN_TILE
    nf = wg.shape[2] // fc
    tile_of = lambda j, tot: jnp.minimum(j, tot[0] - 1)
    chunk_of = lambda j, f, tot: jnp.where(j < tot[0], f, nf - 1)
    return pl.pallas_call(
        _moe_ffn_kernel,
        grid_spec=pltpu.PrefetchScalarGridSpec(
            num_scalar_prefetch=2,
            grid=(nt, nf),
            in_specs=[pl.BlockSpec((FFN_TILE, D_MODEL), lambda j, f, te, tot: (tile_of(j, tot), 0)),
                      pl.BlockSpec((1, D_MODEL, fc), lambda j, f, te, tot: (te[j], 0, chunk_of(j, f, tot))),
                      pl.BlockSpec((1, D_MODEL, fc), lambda j, f, te, tot: (te[j], 0, chunk_of(j, f, tot))),
                      pl.BlockSpec((1, fc, D_MODEL), lambda j, f, te, tot: (te[j], chunk_of(j, f, tot), 0))],
            out_specs=pl.BlockSpec((FFN_TILE, D_MODEL), lambda j, f, te, tot: (j, 0)),
            scratch_shapes=[pltpu.VMEM((FFN_TILE, D_MODEL), F32)],
        ),
        out_shape=jax.ShapeDtypeStruct((n_sort, D_MODEL), BF16),
        compiler_params=_cparams(("arbitrary", "arbitrary")),
        name="moe_ffn",
    )(tile_e, total, xs, wg, wu, wd)


def _combine_kernel(loff_ref, slen_ref, goff_ref, info_ref, h_ref, fn_ref, y_ref, o_ref, buf, sem):
    i = pl.program_id(0)
    n = pl.num_programs(0)
    slot = i % 2

    def copy_for(s):
        def make(lo, go, rows):
            return pltpu.make_async_copy(y_ref.at[pl.ds(go, rows), :], buf.at[s, pl.ds(lo, rows), :],
                                         sem.at[s])
        return make

    wait = lambda c: c.wait()
    start = lambda c: c.start()

    @pl.when(i == 0)
    def _():
        buf[...] = jnp.zeros_like(buf)
        _segment_copies(i, loff_ref, slen_ref, goff_ref, copy_for(slot), start)

    @pl.when(i + 1 < n)
    def _():
        _segment_copies(i + 1, loff_ref, slen_ref, goff_ref, copy_for(1 - slot), start)

    _segment_copies(i, loff_ref, slen_ref, goff_ref, copy_for(slot), wait)

    ys = buf[slot]
    cidx = lax.broadcasted_iota(jnp.int32, (PROJ_SUB, LOCAL_ROWS), 1).astype(F32)
    for r in range(TOK_TILE // PROJ_SUB):
        rows = slice(r * PROJ_SUB, (r + 1) * PROJ_SUB)
        info = info_ref[rows, :]
        col_of = lambda c, info=info: info[:, c:c + 1]
        loc1 = _local_rows(col_of, loff_ref, i, INFO_E1, INFO_R1)
        loc2 = _local_rows(col_of, loff_ref, i, INFO_E2, INFO_R2)
        pt = (jnp.where(cidx == loc1, col_of(INFO_W1), 0.0)
              + jnp.where(cidx == loc2, col_of(INFO_W2), 0.0)).astype(BF16)
        o_ref[rows, :] = _rms(h_ref[rows, :] + _dot(pt, ys), fn_ref[...])


def _combine(loff, slen, goff, info, h, final_norm, y):
    rows = h.shape[0]
    nt = rows // TOK_TILE
    return pl.pallas_call(
        _combine_kernel,
        grid_spec=pltpu.PrefetchScalarGridSpec(
            num_scalar_prefetch=3,
            grid=(nt,),
            in_specs=[pl.BlockSpec((TOK_TILE, LANES), lambda i, *_: (i, 0)),
                      pl.BlockSpec((TOK_TILE, D_MODEL), lambda i, *_: (i, 0)),
                      pl.BlockSpec((1, D_MODEL), lambda i, *_: (0, 0)),
                      pl.BlockSpec(memory_space=pl.ANY)],
            out_specs=pl.BlockSpec((TOK_TILE, D_MODEL), lambda i, *_: (i, 0)),
            scratch_shapes=[pltpu.VMEM((2, LOCAL_ROWS, D_MODEL), BF16),
                            pltpu.SemaphoreType.DMA((2,))],
        ),
        out_shape=jax.ShapeDtypeStruct((rows, D_MODEL), F32),
        compiler_params=_cparams(("arbitrary",)),
        name="moe_combine",
    )(loff, slen, goff, info, h, final_norm, y)


def _routing_tables(cnt, n_tiles_max):
    slen = (cnt + SEG_ALIGN - 1) // SEG_ALIGN * SEG_ALIGN
    loff = jnp.cumsum(slen, axis=1) - slen
    rows_e = jnp.sum(slen, axis=0)
    rows_pad = (rows_e + FFN_TILE - 1) // FFN_TILE * FFN_TILE
    base = jnp.cumsum(rows_pad) - rows_pad
    goff = base[None, :] + jnp.cumsum(slen, axis=0) - slen
    tile_end = jnp.cumsum(rows_pad // FFN_TILE)
    total = tile_end[-1]
    j = jnp.arange(n_tiles_max, dtype=jnp.int32)
    tile_e = jnp.sum(jnp.minimum(j, total - 1)[:, None] >= tile_end[None, :], axis=1).astype(jnp.int32)
    flat = lambda a: a.reshape(-1).astype(jnp.int32)
    return flat(loff), flat(slen), flat(goff), tile_e, total.reshape(1).astype(jnp.int32)


def _moe(o, wo, h, gain, wr, wg, wu, wd, final_norm):
    rows = h.shape[0]
    assert rows % TOK_TILE == 0
    nt = rows // TOK_TILE
    max_rows = 2 * rows + nt * N_EXPERTS * (SEG_ALIGN - 1) + N_EXPERTS * (FFN_TILE - SEG_ALIGN)
    n_tiles_max = -(-max_rows // FFN_TILE)
    tri = jnp.tril(jnp.ones((PROJ_SUB, PROJ_SUB), BF16))
    h, xn, info, cnt = _router(o, wo, h, gain, wr, tri)
    cnt = cnt.reshape(nt, SUBLANES, LANES)[:, 0, :N_EXPERTS].astype(jnp.int32)
    loff, slen, goff, tile_e, total = _routing_tables(cnt, n_tiles_max)
    xs = _dispatch(loff, slen, goff, xn, info, n_tiles_max * FFN_TILE)
    y = _moe_ffn(tile_e, total, xs, wg, wu, wd, MOE_CHUNK)
    return _combine(loff, slen, goff, info, h, final_norm, y)


def _ssm_params(lam_re, lam_im, log_dt, b_re, b_im, c_re, c_im, d, glu_w, glu_b):
    g, n, hh = SSM_GROUPS, SSM_STATE, SSM_GROUP
    lr, li = lam_re.astype(F32), lam_im.astype(F32)
    dt = jnp.exp(log_dt.astype(F32))[:, None]
    mag = jnp.exp(dt * lr)
    ar, ai = mag * jnp.cos(dt * li), mag * jnp.sin(dt * li)
    den = lr * lr + li * li
    nr = ar - 1.0
    zr = (nr * lr + ai * li) / den
    zi = (ai * lr - nr * li) / den
    br = zr[..., None] * b_re - zi[..., None] * b_im
    bi = zr[..., None] * b_im + zi[..., None] * b_re
    eye = jnp.eye(g, dtype=F32)
    bblk = jnp.concatenate(
        [jnp.einsum("gnh,gk->ghkn", x, eye).reshape(SSM_WIDTH, SSM_LANES) for x in (br, bi)], axis=1)
    cblk = jnp.concatenate(
        [jnp.einsum("ghn,gk->gnkh", x, eye).reshape(SSM_LANES, SSM_WIDTH) for x in (c_re, -c_im)], axis=0)

    sub = jnp.arange(SUBLANES, dtype=jnp.int32)[:, None]

    def scan_tables(base_r, base_i, layout):
        def power(m):
            pr, pi = jnp.ones((SUBLANES, g, n), F32), jnp.zeros((SUBLANES, g, n), F32)
            for step in range(1, SUBLANES + 1):
                nr_, ni_ = pr * base_r - pi * base_i, pr * base_i + pi * base_r
                take = (m >= step)[:, :, None]
                pr, pi = jnp.where(take, nr_, pr), jnp.where(take, ni_, pi)
            return pr, pi

        tabs = []
        for s in (1, 2, 4):
            pr, pi = power(jnp.full((SUBLANES, 1), s, jnp.int32))
            keep = (sub >= s).astype(F32)[:, :, None]
            tabs.append(layout(pr * keep, pi * keep))
        tabs.append(layout(*power(sub + 1)))
        return jnp.concatenate(tabs, axis=0)

    flat = lambda pr, pi: jnp.concatenate([pr.reshape(SUBLANES, SSM_LANES), pi.reshape(SUBLANES, SSM_LANES)], axis=1)

    q, nv, gl = SSM_Q, SSM_TILES, SSM_GROUPS // SSM_TILES
    pw_r, pw_i = [jnp.ones_like(ar)], [jnp.zeros_like(ar)]
    for _ in range(q):
        pw_r, pw_i = pw_r + [pw_r[-1] * ar - pw_i[-1] * ai], pw_i + [pw_r[-1] * ai + pw_i[-1] * ar]
    pw_r, pw_i = jnp.stack(pw_r), jnp.stack(pw_i)
    cr, ci = c_re.astype(F32), c_im.astype(F32)

    def expand(compact, row_w, col_w):
        s_idx = jnp.arange(LANES)[:, None]
        c_idx = jnp.arange(SSM_VW)[None, :]
        onehot = (s_idx == (c_idx // (gl * col_w)) * col_w + c_idx % col_w).astype(BF16)
        full = jnp.einsum("vrs,sc->vrc", compact.astype(BF16), onehot, preferred_element_type=F32)
        r_idx = jnp.arange(SSM_VW)[:, None]
        same_group = (r_idx // row_w) % gl == (c_idx // col_w) % gl
        return jnp.where(same_group[None], full, 0.0).astype(BF16)

    tiled = lambda pr, pi: jnp.stack([pr, pi], axis=1).reshape(SUBLANES, 2, nv, gl, n).transpose(
        0, 2, 1, 3, 4).reshape(SUBLANES, 2 * SSM_LANES)
    crt, cit = cr.transpose(0, 2, 1)[:, :, :, None], ci.transpose(0, 2, 1)[:, :, :, None]
    m_r = crt * br[:, :, None, :] - cit * bi[:, :, None, :]
    m_i = crt * bi[:, :, None, :] + cit * br[:, :, None, :]
    kern = [jnp.sum(pw_r[dd][:, :, None, None] * m_r - pw_i[dd][:, :, None, None] * m_i, axis=1)
            for dd in range(q)]
    none = jnp.zeros_like(kern[0])
    kf = jnp.stack([jnp.stack([kern[o - i] if o >= i else none for o in range(q)]) for i in range(q)])
    kf = kf.transpose(2, 0, 1, 3, 4)
    kf = kf.reshape(nv, gl, q, q, hh, hh).transpose(0, 2, 1, 5, 3, 4)
    tz = expand(kf.reshape(nv, SSM_VW, LANES), hh, hh)
    rev_r = jnp.stack([pw_r[q - 1 - j] for j in range(q)])[:, :, :, None]
    rev_i = jnp.stack([pw_i[q - 1 - j] for j in range(q)])[:, :, :, None]
    wb = jnp.stack([rev_r * br[None] - rev_i * bi[None], rev_r * bi[None] + rev_i * br[None]])
    wb = wb.reshape(2, q, nv, gl, n, hh).transpose(2, 1, 3, 5, 0, 4)
    w2 = expand(wb.reshape(nv, SSM_VW, LANES), hh, n)
    nx_r, nx_i = pw_r[1:][:, :, None, :], pw_i[1:][:, :, None, :]
    wc = jnp.stack([cr[None] * nx_r - ci[None] * nx_i, -(cr[None] * nx_i + ci[None] * nx_r)])
    wc = wc.reshape(2, q, nv, gl, hh, n).transpose(2, 0, 3, 5, 1, 4)
    w1 = expand(wc.reshape(nv, SSM_VW, LANES), n, hh)
    return {
        "bblk": bblk.astype(BF16), "cblk": cblk.astype(BF16),
        "d": d.reshape(1, SSM_WIDTH).astype(F32),
        "glu_w": glu_w.astype(BF16), "glu_b": glu_b.reshape(1, SSM_WIDTH).astype(F32),
        "powers": scan_tables(ar, ai, flat),
        "chunk_powers": scan_tables(pw_r[q], pw_i[q], tiled),
        "tz": tz.astype(BF16), "w2": w2.astype(BF16), "w1": w1.astype(BF16),
    }


def _row(x):
    return x.reshape(1, -1).astype(F32)


def _pad_cols(w, n):
    return jnp.pad(w, ((0, 0), (0, n - w.shape[1])))


def kernel(x, meta_tokens, ev_norm_mix, ev_w_in, ev_conv_w, ev_conv_b, ev_gate_a_w, ev_gate_a_b, ev_gate_x_w, ev_gate_x_b, ev_lru_lambda, ev_ssm_lambda_re, ev_ssm_lambda_im, ev_ssm_log_dt, ev_ssm_b_re, ev_ssm_b_im, ev_ssm_c_re, ev_ssm_c_im, ev_ssm_d, ev_ssm_glu_w, ev_ssm_glu_b, ev_w_out, ev_norm_ffn, ev_ffn_w_gate, ev_ffn_w_up, ev_ffn_w_down, od_norm_mix, od_w_in, od_gla_gate_w2, od_gla_gate_b, od_gla_norm, od_w_out, od_norm_ffn, od_router_w, od_moe_w_gate, od_moe_w_up, od_moe_w_down, final_norm):
    nb, seq, _ = x.shape
    rows = nb * seq
    assert ev_w_in.shape[0] == 1 and od_w_in.shape[0] == 1, "two-layer trunk only"

    w_in0 = ev_w_in[0].astype(BF16)
    lru_p = {
        "conv_w": ev_conv_w[0].astype(F32), "conv_b": _row(ev_conv_b[0]),
        "gate_a_w": ev_gate_a_w[0].astype(BF16), "gate_a_b": _row(ev_gate_a_b[0]),
        "gate_x_w": ev_gate_x_w[0].astype(BF16), "gate_x_b": _row(ev_gate_x_b[0]),
        "softplus": _row(jax.nn.softplus(-ev_lru_lambda[0].astype(F32))),
    }
    ssm_p = _ssm_params(ev_ssm_lambda_re[0], ev_ssm_lambda_im[0], ev_ssm_log_dt[0], ev_ssm_b_re[0],
                        ev_ssm_b_im[0], ev_ssm_c_re[0], ev_ssm_c_im[0], ev_ssm_d[0],
                        ev_ssm_glu_w[0], ev_ssm_glu_b[0])
    w_out0 = ev_w_out[0].astype(BF16)
    w_out0_a, w_out0_b = w_out0[:D_MODEL], w_out0[D_MODEL:]
    ffn_g, ffn_u, ffn_d = (w[0].astype(BF16) for w in (ev_ffn_w_gate, ev_ffn_w_up, ev_ffn_w_down))
    odd_in = 2 * GLA_KEY + 2 * GLA_VAL
    w_in1 = _pad_cols(od_w_in[0], odd_in + LANES).astype(BF16)
    w2 = jnp.pad(od_gla_gate_w2[0].astype(F32), ((0, LANES - GLA_RANK), (0, 0)))
    w2_hi = w2.astype(BF16)
    w2_split = jnp.stack([w2_hi, (w2 - w2_hi.astype(F32)).astype(BF16)])
    wr = _pad_cols(od_router_w[0].astype(F32), LANES)
    wr_hi = wr.astype(BF16)
    wr_split = jnp.concatenate([wr_hi, (wr - wr_hi.astype(F32)).astype(BF16)], axis=1)
    w_out1 = od_w_out[0].astype(BF16)
    moe_g, moe_u, moe_d = (w[0].astype(BF16) for w in (od_moe_w_gate, od_moe_w_up, od_moe_w_down))

    tl = min(512, seq)
    tm = min(1024, rows)
    tm_ffn = min(512, rows)
    ck = min(128, tl)
    tl_ssm = min(2048, seq)
    even_splits = (D_MODEL, D_MODEL, SSM_WIDTH)

    def even_mixer(h, nbatch, tile_m, tile_f, tile_l, conv0, h0, s0):
        permuted = tile_l % PERM_BLOCK == 0
        gate, rec, u = _norm_proj(h, _row(ev_norm_mix[0]), w_in0, even_splits, tile_m, permuted)
        a_out, conv_t, h_t = _lru(gate, rec, lru_p, conv0, h0, nbatch, tile_l, permuted)
        if permuted:
            b_out, s_t = _ssm_chunked(u, ssm_p, s0, nbatch, tl_ssm)
        else:
            b_out, s_t = _ssm(u, ssm_p, s0, nbatch, tile_l)
            s_t = s_t.reshape(SUBLANES, 2, SSM_TILES, SSM_LANES // SSM_TILES).transpose(0, 2, 1, 3).reshape(
                SUBLANES, 2 * SSM_LANES)
        h = _mix_ffn(a_out, b_out, w_out0_a, w_out0_b, h, _row(ev_norm_ffn[0]), ffn_g, ffn_u, ffn_d, tile_f,
                     permuted)
        return h, (conv_t, h_t, s_t)

    def gla_inputs(h, tile_m):
        return _gla_proj(h, _row(od_norm_mix[0]), w_in1, w2_split, _row(od_gla_gate_b[0]), tile_m)

    zeros = lambda *s: jnp.zeros(s, F32)
    hm = meta_tokens.astype(F32)
    hm, (conv_m, h_m, s_m) = even_mixer(hm, 1, N_META, N_META, N_META, zeros(SUBLANES, D_MODEL),
                                        zeros(SUBLANES, D_MODEL), zeros(SUBLANES, 2 * SSM_LANES))
    qm, km, vm, ogm, lgm = gla_inputs(hm, N_META)
    _, gla_s = _gla(qm, km, vm, ogm, lgm, _row(od_gla_norm[0]),
                    zeros(GLA_HEADS, GLA_DV, GLA_DK), 1, N_META, N_META)

    h = x.reshape(rows, D_MODEL).astype(F32)
    h, _ = even_mixer(h, nb, tm, tm_ffn, tl, conv_m, h_m, s_m)
    q, k, v, og, lg = gla_inputs(h, tm)
    o, _ = _gla(q, k, v, og, lg, _row(od_gla_norm[0]), gla_s, nb, tl, ck)
    out = _moe(o, w_out1, h, _row(od_norm_ffn[0]), wr_split, moe_g, moe_u, moe_d, _row(final_norm))
    return out.reshape(nb, seq, D_MODEL)
```

```python
import functools
import math

import jax
import jax.numpy as jnp
from jax import lax
from jax.experimental import pallas as pl
from jax.experimental.pallas import tpu as pltpu

F32 = jnp.float32
BF16 = jnp.bfloat16

D_MODEL = 1024
N_META = 16
EPS = 1e-6
LRU_BLOCKS = 4
LRU_BLOCK = 256
LRU_C = 8.0
SSM_WIDTH = 512
SSM_GROUP = 16
SSM_GROUPS = 32
SSM_STATE = 64
SSM_LANES = SSM_GROUPS * SSM_STATE
D_FF = 3072
GLA_HEADS = 4
GLA_DK = 128
GLA_DV = 256
GLA_KEY = 512
GLA_VAL = 1024
GLA_RANK = 16
GLA_TAU = 16.0
N_EXPERTS = 8
D_FF_EXPERT = 3584

SUBLANES = 8
LANES = 128
CHUNK_STEPS = 8
PERM_BLOCK = 256
VMEM_LIMIT = 52 * 1024 * 1024


def _cparams(sem):
    return pltpu.CompilerParams(dimension_semantics=sem, vmem_limit_bytes=VMEM_LIMIT)


def _dot(a, b):
    return jnp.dot(a, b, preferred_element_type=F32)


def _rms(x, gain):
    ms = jnp.mean(x * x, axis=-1, keepdims=True)
    return x * lax.rsqrt(ms + EPS) * gain


def _gelu_tanh(x):
    c = math.sqrt(2.0 / math.pi)
    return 0.5 * x * (1.0 + jnp.tanh(c * (x + 0.044715 * (x * x * x))))


def _sigmoid(x):
    return 0.5 * jnp.tanh(0.5 * x) + 0.5


def _silu(x):
    return x * _sigmoid(x)


def _split_bf16(x):
    hi = x.astype(BF16)
    lo = (x - hi.astype(F32)).astype(BF16)
    return hi, lo


PROJ_SUB = 256


def _gla_proj_kernel(x_ref, g_ref, w_ref, w2_ref, b2_ref, q_ref, k_ref, v_ref, og_ref, lg_ref):
    tm = x_ref.shape[0]
    sub = min(PROJ_SUB, tm)
    o0 = 2 * GLA_KEY + GLA_VAL
    for r in range(tm // sub):
        rows = slice(r * sub, (r + 1) * sub)
        xn = _rms(x_ref[rows, :], g_ref[...]).astype(BF16)
        q_ref[rows, :] = _dot(xn, w_ref[:, 0:GLA_KEY]).astype(BF16)
        k_ref[rows, :] = _dot(xn, w_ref[:, GLA_KEY:2 * GLA_KEY]).astype(BF16)
        v_ref[rows, :] = _dot(xn, w_ref[:, 2 * GLA_KEY:o0]).astype(BF16)
        og_ref[rows, :] = _dot(xn, w_ref[:, o0:o0 + GLA_VAL]).astype(BF16)
        glr = _dot(xn, w_ref[:, o0 + GLA_VAL:o0 + GLA_VAL + LANES])
        glr_hi, glr_lo = _split_bf16(glr)
        w2_hi = w2_ref[0]
        w2_lo = w2_ref[1]
        pre = _dot(glr_hi, w2_hi) + _dot(glr_lo, w2_hi) + _dot(glr_hi, w2_lo) + b2_ref[...]
        lg_ref[rows, :] = (jnp.minimum(pre, 0.0) - jnp.log1p(jnp.exp(-jnp.abs(pre)))) * (1.0 / GLA_TAU)


def _gla_proj(x, gain, w, w2, b2, tm):
    rows = x.shape[0]
    n = w.shape[1]
    row_spec = lambda c: pl.BlockSpec((tm, c), lambda i: (i, 0))
    return pl.pallas_call(
        _gla_proj_kernel,
        grid=(rows // tm,),
        in_specs=[
            row_spec(D_MODEL),
            pl.BlockSpec((1, D_MODEL), lambda i: (0, 0)),
            _resident((D_MODEL, n)),
            pl.BlockSpec((2, LANES, GLA_KEY), lambda i: (0, 0, 0)),
            pl.BlockSpec((1, GLA_KEY), lambda i: (0, 0)),
        ],
        out_specs=[row_spec(GLA_KEY), row_spec(GLA_KEY), row_spec(GLA_VAL), row_spec(GLA_VAL),
                   row_spec(GLA_KEY)],
        out_shape=[jax.ShapeDtypeStruct((rows, GLA_KEY), BF16),
                   jax.ShapeDtypeStruct((rows, GLA_KEY), BF16),
                   jax.ShapeDtypeStruct((rows, GLA_VAL), BF16),
                   jax.ShapeDtypeStruct((rows, GLA_VAL), BF16),
                   jax.ShapeDtypeStruct((rows, GLA_KEY), F32)],
        compiler_params=_cparams(("parallel",)),
        name="gla_proj",
    )(x, gain, w, w2, b2)


def _affine_scan8(a, b, row):
    for s in (1, 2, 4):
        m = row >= s
        a_sh = pltpu.roll(a, s, 0)
        b_sh = pltpu.roll(b, s, 0)
        b = jnp.where(m, a * b_sh + b, b)
        a = jnp.where(m, a * a_sh, a)
    return a, b


def _lru_tile(gate_ref, rec_ref, cw_ref, cb_ref, wa_ref, ba_ref, wx_ref, bx_ref, sp_ref,
              conv0_ref, h0_ref, out_ref, conv_t_ref, h_t_ref, xbuf, xcbuf, gbuf, abuf, bbuf, hprev, hcar,
              *, tl, permuted, restart, after_read):
    @pl.when(restart)
    def _():
        xbuf[0:SUBLANES, :] = conv0_ref[...]
        hcar[...] = h0_ref[...]

    row = lax.broadcasted_iota(jnp.int32, (SUBLANES, D_MODEL), 0)
    ns = PERM_BLOCK // CHUNK_STEPS
    slab = lambda b, j: slice(b * PERM_BLOCK + j * ns, b * PERM_BLOCK + (j + 1) * ns)

    if permuted:
        srow = lax.broadcasted_iota(jnp.int32, (ns, D_MODEL), 0)
        for b in range(tl // PERM_BLOCK):
            x = [rec_ref[slab(b, j), :].astype(F32) for j in range(CHUNK_STEPS)]
            back = {j: jnp.where(srow == 0, xbuf[j:j + 1, :], pltpu.roll(x[j], 1, 0))
                    for j in range(CHUNK_STEPS - 3, CHUNK_STEPS)}
            for j in range(CHUNK_STEPS):
                xc = cb_ref[...]
                for k in range(4):
                    src = x[j - k] if j >= k else back[j - k + CHUNK_STEPS]
                    xc = xc + cw_ref[3 - k:4 - k, :] * src
                xcbuf[slab(b, j), :] = xc
            last = [x[j][ns - 1:ns, :] for j in range(CHUNK_STEPS - 3, CHUNK_STEPS)]
            xbuf[0:SUBLANES, :] = jnp.where(row == 5, last[0], jnp.where(row == 6, last[1], last[2]))
    else:
        xbuf[SUBLANES:SUBLANES + tl, :] = rec_ref[...].astype(F32)
        xc = cb_ref[...]
        for tap in range(4):
            xc = xc + cw_ref[tap:tap + 1, :] * xbuf[SUBLANES - 3 + tap:SUBLANES - 3 + tap + tl, :]
        xcbuf[...] = xc
        xbuf[0:SUBLANES, :] = xbuf[tl:tl + SUBLANES, :]
    for blk in range(LRU_BLOCKS):
        sl = slice(blk * LRU_BLOCK, (blk + 1) * LRU_BLOCK)
        xb = xcbuf[:, sl].astype(BF16)
        abuf[:, sl] = _dot(xb, wa_ref[blk])
        bbuf[:, sl] = _dot(xb, wx_ref[blk])
    gbuf[...] = _gelu_tanh(gate_ref[...].astype(F32))
    after_read()

    for blk in range(LRU_BLOCKS):
        sl = slice(blk * LRU_BLOCK, (blk + 1) * LRU_BLOCK)
        xs = xcbuf[:, sl]
        r = _sigmoid(abuf[:, sl] + ba_ref[:, sl])
        i = _sigmoid(bbuf[:, sl] + bx_ref[:, sl])
        a = jnp.exp(-LRU_C * r * sp_ref[:, sl])
        abuf[:, sl] = a
        v = 1.0 - a * a
        bbuf[:, sl] = jnp.where(v > 0.0, v * lax.rsqrt(v), 0.0) * (i * xs)

    last_row = lambda v: jnp.broadcast_to(v[SUBLANES - 1:SUBLANES, :], (SUBLANES, D_MODEL))
    if permuted:
        carry = hcar[...]
        for b in range(tl // PERM_BLOCK):
            for j in range(1, CHUNK_STEPS):
                a_j = abuf[slab(b, j), :]
                bbuf[slab(b, j), :] = a_j * bbuf[slab(b, j - 1), :] + bbuf[slab(b, j), :]
                abuf[slab(b, j), :] = a_j * abuf[slab(b, j - 1), :]
            top = b * PERM_BLOCK + (CHUNK_STEPS - 1) * ns
            for g in range(ns // SUBLANES):
                rg = slice(top + g * SUBLANES, top + (g + 1) * SUBLANES)
                a, bb = _affine_scan8(abuf[rg, :], bbuf[rg, :], row)
                h = a * carry + bb
                hprev[g * SUBLANES:(g + 1) * SUBLANES, :] = jnp.where(row == 0, carry, pltpu.roll(h, 1, 0))
                carry = last_row(h)
            for j in range(CHUNK_STEPS):
                bbuf[slab(b, j), :] = abuf[slab(b, j), :] * hprev[...] + bbuf[slab(b, j), :]
    else:
        def body(gidx, carry):
            r0 = pl.multiple_of(gidx * SUBLANES, SUBLANES)
            a, b = _affine_scan8(abuf[pl.ds(r0, SUBLANES), :], bbuf[pl.ds(r0, SUBLANES), :], row)
            h = a * carry + b
            bbuf[pl.ds(r0, SUBLANES), :] = h
            return last_row(h)

        carry = lax.fori_loop(0, tl // SUBLANES, body, hcar[...])
    hcar[...] = carry
    out_ref[...] = (gbuf[...] * bbuf[...]).astype(out_ref.dtype)
    conv_t_ref[...] = xbuf[0:SUBLANES, :]
    h_t_ref[...] = carry


def _in_proj_lru_kernel(x_ref, g_ref, w_ref, perm_ref, cw_ref, cb_ref, wa_ref, ba_ref, wx_ref, bx_ref, sp_ref,
                        conv0_ref, h0_ref, a_out_ref, u_ref, conv_t_ref, h_t_ref,
                        zbuf, xbuf, xcbuf, gbuf, abuf, bbuf, hprev, hcar, *, tl, permuted):
    t = pl.program_id(1)

    @pl.when(t == 0)
    def _():
        zbuf[...] = jnp.zeros_like(zbuf)

    def project():
        sub = min(PERM_BLOCK, tl)
        for r in range(tl // sub):
            rows = slice(r * sub, (r + 1) * sub)
            xn = _rms(x_ref[rows, :], g_ref[...]).astype(BF16)
            if permuted:
                xn = _dot(perm_ref[...], xn).astype(BF16)
            zbuf[rows, :] = _dot(xn, w_ref[:, 0:2 * D_MODEL]).astype(BF16)
            u_ref[rows, :] = _dot(xn, w_ref[:, 2 * D_MODEL:]).astype(BF16)

    _lru_tile(zbuf.at[:, 0:D_MODEL], zbuf.at[:, D_MODEL:2 * D_MODEL], cw_ref, cb_ref, wa_ref, ba_ref,
              wx_ref, bx_ref, sp_ref, conv0_ref, h0_ref, a_out_ref, conv_t_ref, h_t_ref,
              xbuf, xcbuf, gbuf, abuf, bbuf, hprev, hcar, tl=tl, permuted=permuted, restart=t <= 1,
              after_read=project)


def _block_perm():
    r_out = jnp.arange(PERM_BLOCK)
    chunks = PERM_BLOCK // CHUNK_STEPS
    src = CHUNK_STEPS * (r_out % chunks) + r_out // chunks
    return (jnp.arange(PERM_BLOCK)[None, :] == src[:, None]).astype(BF16)


def _in_proj_lru(x, gain, w, p, conv0, h0, nb, tl, permuted):
    rows = x.shape[0]
    nt = rows // (nb * tl)
    assert not permuted or tl % PERM_BLOCK == 0
    n_u = w.shape[1] - 2 * D_MODEL
    cur = lambda c: pl.BlockSpec((tl, c), lambda b, t: (b * nt + jnp.minimum(t, nt - 1), 0))
    prev = lambda c: pl.BlockSpec((tl, c), lambda b, t: (b * nt + jnp.maximum(t - 1, 0), 0))
    state = pl.BlockSpec((SUBLANES, D_MODEL), lambda b, t: (0, 0))
    return pl.pallas_call(
        functools.partial(_in_proj_lru_kernel, tl=tl, permuted=permuted),
        grid=(nb, nt + 1),
        in_specs=[cur(D_MODEL), _resident((1, D_MODEL)), _resident(w.shape), _resident((PERM_BLOCK, PERM_BLOCK)),
                  _resident((4, D_MODEL)), _resident((1, D_MODEL)),
                  _resident((LRU_BLOCKS, LRU_BLOCK, LRU_BLOCK)), _resident((1, D_MODEL)),
                  _resident((LRU_BLOCKS, LRU_BLOCK, LRU_BLOCK)), _resident((1, D_MODEL)),
                  _resident((1, D_MODEL)), _resident((SUBLANES, D_MODEL)), _resident((SUBLANES, D_MODEL))],
        out_specs=[prev(D_MODEL), cur(n_u), state, state],
        out_shape=[jax.ShapeDtypeStruct((rows, D_MODEL), BF16),
                   jax.ShapeDtypeStruct((rows, n_u), BF16),
                   jax.ShapeDtypeStruct((SUBLANES, D_MODEL), F32),
                   jax.ShapeDtypeStruct((SUBLANES, D_MODEL), F32)],
        scratch_shapes=[pltpu.VMEM((tl, 2 * D_MODEL), BF16),
                        pltpu.VMEM(((0 if permuted else tl) + SUBLANES, D_MODEL), F32),
                        pltpu.VMEM((tl, D_MODEL), F32),
                        pltpu.VMEM((tl, D_MODEL), F32),
                        pltpu.VMEM((tl, D_MODEL), F32),
                        pltpu.VMEM((tl, D_MODEL), F32),
                        pltpu.VMEM((PERM_BLOCK // CHUNK_STEPS, D_MODEL), F32),
                        pltpu.VMEM((SUBLANES, D_MODEL), F32)],
        compiler_params=_cparams(("arbitrary", "arbitrary")),
        name="in_proj_lru",
    )(x, gain, w, _block_perm(), p["conv_w"], p["conv_b"], p["gate_a_w"], p["gate_a_b"], p["gate_x_w"],
      p["gate_x_b"], p["softplus"], conv0, h0)


SSM_CHUNK = 256


def _complex_scan(xs, pw_ref, hcar, n_rows, lane_pairs, prev=None):
    row = lax.broadcasted_iota(jnp.int32, (SUBLANES, SSM_CHUNK), 0)
    for re0, im0 in lane_pairs:
        re = slice(re0, re0 + SSM_CHUNK)
        im = slice(im0, im0 + SSM_CHUNK)
        tabs = [(pw_ref[k * SUBLANES:(k + 1) * SUBLANES, re], pw_ref[k * SUBLANES:(k + 1) * SUBLANES, im])
                for k in range(4)]

        def body(gidx, carry, re=re, im=im, tabs=tabs):
            cr, ci = carry
            r0 = pl.multiple_of(gidx * SUBLANES, SUBLANES)
            hr = xs[pl.ds(r0, SUBLANES), re]
            hi = xs[pl.ds(r0, SUBLANES), im]
            for k, s in enumerate((1, 2, 4)):
                pr, pi = tabs[k]
                sr = pltpu.roll(hr, s, 0)
                si = pltpu.roll(hi, s, 0)
                hr, hi = hr + (pr * sr - pi * si), hi + (pr * si + pi * sr)
            pr, pi = tabs[3]
            hr, hi = hr + (pr * cr - pi * ci), hi + (pr * ci + pi * cr)
            xs[pl.ds(r0, SUBLANES), re] = hr
            xs[pl.ds(r0, SUBLANES), im] = hi
            if prev is not None:
                prev[pl.ds(r0, SUBLANES), re] = jnp.where(row == 0, cr, pltpu.roll(hr, 1, 0))
                prev[pl.ds(r0, SUBLANES), im] = jnp.where(row == 0, ci, pltpu.roll(hi, 1, 0))
            last = slice(SUBLANES - 1, SUBLANES)
            return (jnp.broadcast_to(hr[last, :], (SUBLANES, SSM_CHUNK)),
                    jnp.broadcast_to(hi[last, :], (SUBLANES, SSM_CHUNK)))

        cr, ci = lax.fori_loop(0, n_rows // SUBLANES, body, (hcar[:, re], hcar[:, im]))
        hcar[:, re] = cr
        hcar[:, im] = ci


def _ssm_kernel(u_ref, bblk_ref, cblk_ref, d_ref, gw_ref, gb_ref, pw_ref, h0_ref,
                out_ref, h_t_ref, xs, hcar, *, tl):
    @pl.when(pl.program_id(1) == 0)
    def _():
        hcar[...] = h0_ref[...]

    u = u_ref[...]
    xs[...] = _dot(u, bblk_ref[...])
    pairs = [(c * SSM_CHUNK, SSM_LANES + c * SSM_CHUNK) for c in range(SSM_LANES // SSM_CHUNK)]
    _complex_scan(xs, pw_ref, hcar, tl, pairs)
    y = _dot(xs[...].astype(BF16), cblk_ref[...]) + d_ref[...] * u.astype(F32)
    g = _gelu_tanh(y)
    out_ref[...] = (g * jax.nn.sigmoid(_dot(g.astype(BF16), gw_ref[...]) + gb_ref[...])).astype(out_ref.dtype)
    h_t_ref[...] = hcar[...]


def _ssm(u, p, h0, nb, tl):
    rows = u.shape[0]
    nt = rows // (nb * tl)
    tile = pl.BlockSpec((tl, SSM_WIDTH), lambda b, t: (b * nt + t, 0))
    full2 = lambda shp: pl.BlockSpec(shp, lambda b, t: (0, 0))
    return pl.pallas_call(
        functools.partial(_ssm_kernel, tl=tl),
        grid=(nb, nt),
        in_specs=[tile,
                  _resident((SSM_WIDTH, 2 * SSM_LANES)), _resident((2 * SSM_LANES, SSM_WIDTH)),
                  full2((1, SSM_WIDTH)), full2((SSM_WIDTH, SSM_WIDTH)), full2((1, SSM_WIDTH)),
                  full2((4 * SUBLANES, 2 * SSM_LANES)), full2((SUBLANES, 2 * SSM_LANES))],
        out_specs=[tile, full2((SUBLANES, 2 * SSM_LANES))],
        out_shape=[jax.ShapeDtypeStruct((rows, SSM_WIDTH), BF16),
                   jax.ShapeDtypeStruct((SUBLANES, 2 * SSM_LANES), F32)],
        scratch_shapes=[pltpu.VMEM((tl, 2 * SSM_LANES), F32),
                        pltpu.VMEM((SUBLANES, 2 * SSM_LANES), F32)],
        compiler_params=_cparams(("arbitrary", "arbitrary")),
        name="s5_ssm",
    )(u, p["bblk"], p["cblk"], p["d"], p["glu_w"], p["glu_b"], p["powers"], h0)


SSM_Q = CHUNK_STEPS
SSM_PB = PERM_BLOCK
SSM_TILES = SSM_WIDTH // LANES
SSM_VW = SSM_Q * LANES


def _ssm_chunk_kernel(u_ref, w2_ref, tz_ref, w1_ref, pw_ref, d_ref, gw_ref, gb_ref,
                      h0_ref, out_ref, h_t_ref, a_scr, xs, hp, ypm, hcar, *, tl):
    nblk = tl // SSM_PB
    ncb = SSM_PB // SSM_Q
    nc = tl // SSM_Q

    @pl.when(pl.program_id(1) == 0)
    def _():
        hcar[...] = h0_ref[...]

    for b in range(nblk):
        for j in range(SSM_Q):
            a_scr[j, b * ncb:(b + 1) * ncb, :] = u_ref[b * SSM_PB + j * ncb:b * SSM_PB + (j + 1) * ncb, :]

    def u_tile(v):
        return jnp.concatenate([a_scr[j, :, v * LANES:(v + 1) * LANES] for j in range(SSM_Q)], axis=1)

    for v in range(SSM_TILES):
        xs[:, v * SSM_VW:(v + 1) * SSM_VW] = _dot(u_tile(v), w2_ref[v])

    half = SSM_VW // 2
    pairs = [(v * SSM_VW + c * SSM_CHUNK, v * SSM_VW + half + c * SSM_CHUNK)
             for v in range(SSM_TILES) for c in range(half // SSM_CHUNK)]
    _complex_scan(xs, pw_ref, hcar, nc, pairs, prev=hp)

    for v in range(SSM_TILES):
        yv = _dot(u_tile(v), tz_ref[v]) + _dot(hp[:, v * SSM_VW:(v + 1) * SSM_VW].astype(BF16), w1_ref[v])
        for b in range(nblk):
            for j in range(SSM_Q):
                ypm[b * SSM_PB + j * ncb:b * SSM_PB + (j + 1) * ncb, v * LANES:(v + 1) * LANES] = (
                    yv[b * ncb:(b + 1) * ncb, j * LANES:(j + 1) * LANES])

    for b in range(nblk):
        rows = slice(b * SSM_PB, (b + 1) * SSM_PB)
        y = ypm[rows, :] + d_ref[...] * u_ref[rows, :].astype(F32)
        g = _gelu_tanh(y)
        out_ref[rows, :] = (g * jax.nn.sigmoid(_dot(g.astype(BF16), gw_ref[...]) + gb_ref[...])).astype(out_ref.dtype)
    h_t_ref[...] = hcar[...]


def _ssm_chunked(u, p, h0, nb, tl):
    rows = u.shape[0]
    nt = rows // (nb * tl)
    assert tl % SSM_PB == 0
    tile = pl.BlockSpec((tl, SSM_WIDTH), lambda b, t: (b * nt + t, 0))
    wspec = _resident((SSM_TILES, SSM_VW, SSM_VW))
    state = (SUBLANES, 2 * SSM_LANES)
    return pl.pallas_call(
        functools.partial(_ssm_chunk_kernel, tl=tl),
        grid=(nb, nt),
        in_specs=[tile, wspec, wspec, wspec,
                  _resident((4 * SUBLANES, 2 * SSM_LANES)), _resident((1, SSM_WIDTH)),
                  _resident((SSM_WIDTH, SSM_WIDTH)), _resident((1, SSM_WIDTH)), _resident(state)],
        out_specs=[tile, pl.BlockSpec(state, lambda b, t: (0, 0))],
        out_shape=[jax.ShapeDtypeStruct((rows, SSM_WIDTH), BF16), jax.ShapeDtypeStruct(state, F32)],
        scratch_shapes=[pltpu.VMEM((SSM_Q, tl // SSM_Q, SSM_WIDTH), BF16),
                        pltpu.VMEM((tl // SSM_Q, 2 * SSM_LANES), F32),
                        pltpu.VMEM((tl // SSM_Q, 2 * SSM_LANES), F32),
                        pltpu.VMEM((tl, SSM_WIDTH), F32),
                        pltpu.VMEM(state, F32)],
        compiler_params=_cparams(("arbitrary", "arbitrary")),
        name="s5_ssm_chunked",
    )(u, p["w2"], p["tz"], p["w1"], p["chunk_powers"], p["d"], p["glu_w"], p["glu_b"], h0)


FFN_CHUNK = 512


def _resident(shape):
    return pl.BlockSpec(shape, lambda *_: (0,) * len(shape), pipeline_mode=pl.Buffered(1))


def _mix_ffn_kernel(a_ref, b_ref, permt_ref, wa_ref, wb_ref, h_ref, g_ref, wg_ref, wu_ref, wd_ref, o_ref,
                    *, permuted):
    if permuted:
        pt = permt_ref[...]
        blocks = [slice(r, r + PERM_BLOCK) for r in range(0, h_ref.shape[0], PERM_BLOCK)]
        a = jnp.concatenate([_dot(pt, a_ref[rows, :]).astype(BF16) for rows in blocks], axis=0)
        b = jnp.concatenate([_dot(pt, b_ref[rows, :]).astype(BF16) for rows in blocks], axis=0)
    else:
        a, b = a_ref[...], b_ref[...]
    h = h_ref[...] + _dot(a, wa_ref[...]) + _dot(b, wb_ref[...])
    o_ref[...] = h
    xn = _rms(h, g_ref[...]).astype(BF16)
    for c in range(wg_ref.shape[1] // FFN_CHUNK):
        sl = slice(c * FFN_CHUNK, (c + 1) * FFN_CHUNK)
        a = (_silu(_dot(xn, wg_ref[:, sl])) * _dot(xn, wu_ref[:, sl])).astype(BF16)
        o_ref[...] += _dot(a, wd_ref[sl, :])


def _mix_ffn(a, b, wa, wb, h, gain, wg, wu, wd, tm, permuted):
    rows = h.shape[0]
    assert not permuted or tm % PERM_BLOCK == 0
    row_spec = lambda c: pl.BlockSpec((tm, c), lambda i: (i, 0))
    return pl.pallas_call(
        functools.partial(_mix_ffn_kernel, permuted=permuted),
        grid=(rows // tm,),
        in_specs=[row_spec(a.shape[1]), row_spec(b.shape[1]), _resident((PERM_BLOCK, PERM_BLOCK)),
                  _resident(wa.shape), _resident(wb.shape),
                  row_spec(D_MODEL), _resident((1, D_MODEL)),
                  _resident(wg.shape), _resident(wu.shape), _resident(wd.shape)],
        out_specs=row_spec(D_MODEL),
        out_shape=jax.ShapeDtypeStruct((rows, D_MODEL), F32),
        compiler_params=_cparams(("parallel",)),
        name="mix_ffn",
    )(a, b, _block_perm().T, wa, wb, h, gain, wg, wu, wd)


def _gla_kernel(q_ref, k_ref, v_ref, og_ref, lg_ref, hn_ref, s0_ref, o_ref, s_t_ref, st, *, tl, ck):
    @pl.when(pl.program_id(1) == 0)
    def _():
        st[...] = s0_ref[...]

    ri = lax.broadcasted_iota(jnp.int32, (ck, ck), 0)
    ci = lax.broadcasted_iota(jnp.int32, (ck, ck), 1)
    causal = ri >= ci
    tri = causal.astype(F32).astype(BF16)
    scale = GLA_DK ** -0.5

    for c in range(tl // ck):
        rows = slice(c * ck, (c + 1) * ck)
        lg_hi, lg_lo = _split_bf16(lg_ref[rows, :])
        b_all = _dot(tri, lg_hi) + _dot(tri, lg_lo)
        for h in range(GLA_HEADS):
            ks = slice(h * GLA_DK, (h + 1) * GLA_DK)
            vs = slice(h * GLA_DV, (h + 1) * GLA_DV)
            b = b_all[:, ks]
            b_last = b[ck - 1:ck, :]
            q = q_ref[rows, ks].astype(F32)
            k = k_ref[rows, ks].astype(F32)
            v = v_ref[rows, vs]
            qd = (q * (scale * jnp.exp(b))).astype(BF16)
            kd = (k * jnp.exp(-b)).astype(BF16)
            kl = (k * jnp.exp(b_last - b)).astype(BF16)
            att = lax.dot_general(qd, kd, (((1,), (1,)), ((), ())), preferred_element_type=F32)
            att = jnp.where(causal, att, 0.0).astype(BF16)
            s_h = st[h]
            o = _dot(att, v) + lax.dot_general(qd, s_h.astype(BF16), (((1,), (1,)), ((), ())),
                                               preferred_element_type=F32)
            st[h] = s_h * jnp.exp(b_last) + lax.dot_general(v, kl, (((0,), (0,)), ((), ())),
                                                            preferred_element_type=F32)
            o = o * lax.rsqrt(jnp.mean(o * o, axis=-1, keepdims=True) + EPS) * hn_ref[:, vs]
            o_ref[rows, vs] = (o * _silu(og_ref[rows, vs].astype(F32))).astype(o_ref.dtype)
    s_t_ref[...] = st[...]


def _gla(q, k, v, og, lg, head_norm, s0, nb, tl, ck):
    rows = q.shape[0]
    nt = rows // (nb * tl)
    tile = lambda c: pl.BlockSpec((tl, c), lambda b, t: (b * nt + t, 0))
    state = pl.BlockSpec((GLA_HEADS, GLA_DV, GLA_DK), lambda b, t: (0, 0, 0))
    return pl.pallas_call(
        functools.partial(_gla_kernel, tl=tl, ck=ck),
        grid=(nb, nt),
        in_specs=[tile(GLA_KEY), tile(GLA_KEY), tile(GLA_VAL), tile(GLA_VAL), tile(GLA_KEY),
                  pl.BlockSpec((1, GLA_VAL), lambda b, t: (0, 0)), state],
        out_specs=[tile(GLA_VAL), state],
        out_shape=[jax.ShapeDtypeStruct((rows, GLA_VAL), BF16),
                   jax.ShapeDtypeStruct((GLA_HEADS, GLA_DV, GLA_DK), F32)],
        scratch_shapes=[pltpu.VMEM((GLA_HEADS, GLA_DV, GLA_DK), F32)],
        compiler_params=_cparams(("arbitrary", "arbitrary")),
        name="gla",
    )(q, k, v, og, lg, head_norm, s0)


TOK_TILE = 512
SEG_ALIGN = 16
LOCAL_ROWS = 1152
FFN_TILE = 512
MOE_CHUNK = 1792
SEG_BITS = (512, 256, 128, 64, 32, 16)
INFO_E1, INFO_E2, INFO_R1, INFO_R2, INFO_W1, INFO_W2 = range(6)
assert LOCAL_ROWS >= 2 * TOK_TILE + N_EXPERTS * (SEG_ALIGN - 1) and LOCAL_ROWS % LANES == 0


def _router_kernel(o_ref, wo_ref, h_ref, g_ref, wr_ref, tri_ref, hout_ref, xn_ref, info_ref, cnt_ref):
    tm = h_ref.shape[0]
    sub = tri_ref.shape[0]
    w_both = wr_ref[...]
    w_hi = wr_ref[:, 0:LANES]
    lane = lax.broadcasted_iota(jnp.int32, (sub, LANES), 1).astype(F32)
    neg = jnp.float32(-jnp.inf)
    count = jnp.zeros((1, LANES), F32)
    for r in range(tm // sub):
        rows = slice(r * sub, (r + 1) * sub)
        h = h_ref[rows, :] + _dot(o_ref[rows, :], wo_ref[...])
        hout_ref[rows, :] = h
        xn = _rms(h, g_ref[...])
        xn_ref[rows, :] = xn.astype(BF16)
        x_hi, x_lo = _split_bf16(xn)
        both = _dot(x_hi, w_both)
        logits = both[:, 0:LANES] + both[:, LANES:2 * LANES] + _dot(x_lo, w_hi)
        logits = jnp.where(lane < N_EXPERTS, logits, neg)
        v1 = jnp.max(logits, axis=-1, keepdims=True)
        i1 = jnp.min(jnp.where(logits == v1, lane, float(LANES)), axis=-1, keepdims=True)
        rest = jnp.where(lane == i1, neg, logits)
        v2 = jnp.max(rest, axis=-1, keepdims=True)
        i2 = jnp.min(jnp.where(rest == v2, lane, float(LANES)), axis=-1, keepdims=True)
        e2 = jnp.exp(v2 - v1)
        w1 = 1.0 / (1.0 + e2)
        w2 = e2 / (1.0 + e2)
        hit1 = lane == i1
        hit2 = lane == i2
        sel = jnp.where(hit1, 1.0, 0.0) + jnp.where(hit2, 1.0, 0.0)
        cum = _dot(tri_ref[...], sel.astype(BF16)) + count
        count = cum[sub - 1:sub, :]
        rank1 = jnp.sum(jnp.where(hit1, cum, 0.0), axis=-1, keepdims=True) - 1.0
        rank2 = jnp.sum(jnp.where(hit2, cum, 0.0), axis=-1, keepdims=True) - 1.0
        info = jnp.zeros_like(logits)
        for col, val in ((INFO_E1, i1), (INFO_E2, i2), (INFO_R1, rank1), (INFO_R2, rank2),
                         (INFO_W1, w1), (INFO_W2, w2)):
            info = jnp.where(lane == col, val, info)
        info_ref[rows, :] = info
    cnt_ref[...] = jnp.broadcast_to(count, (SUBLANES, LANES))


def _router(o, wo, h, gain, wr, tri):
    rows = h.shape[0]
    tm = TOK_TILE
    nt = rows // tm
    row_spec = lambda c: pl.BlockSpec((tm, c), lambda i: (i, 0))
    return pl.pallas_call(
        _router_kernel,
        grid=(nt,),
        in_specs=[row_spec(o.shape[1]), _resident(wo.shape), row_spec(D_MODEL), _resident((1, D_MODEL)),
                  _resident((D_MODEL, 2 * LANES)), _resident(tri.shape)],
        out_specs=[row_spec(D_MODEL), row_spec(D_MODEL), row_spec(LANES),
                   pl.BlockSpec((SUBLANES, LANES), lambda i: (i, 0))],
        out_shape=[jax.ShapeDtypeStruct((rows, D_MODEL), F32),
                   jax.ShapeDtypeStruct((rows, D_MODEL), BF16),
                   jax.ShapeDtypeStruct((rows, LANES), F32),
                   jax.ShapeDtypeStruct((nt * SUBLANES, LANES), F32)],
        compiler_params=_cparams(("parallel",)),
        name="router",
    )(o, wo, h, gain, wr, tri)


def _segment_copies(tile, loff_ref, slen_ref, goff_ref, make_copy, fn):
    for e in range(N_EXPERTS):
        idx = tile * N_EXPERTS + e
        lo = loff_ref[idx]
        n = slen_ref[idx]
        go = goff_ref[idx]
        for bit in SEG_BITS:
            pos = n & (-2 * bit)

            @pl.when((n & bit) != 0)
            def _(lo=lo, go=go, pos=pos, bit=bit):
                fn(make_copy(pl.multiple_of(lo + pos, SEG_ALIGN), pl.multiple_of(go + pos, SEG_ALIGN), bit))


def _local_rows(info_row, loff_ref, tile, which_e, which_r):
    e = info_row(which_e)
    loc = info_row(which_r)
    for ex in range(N_EXPERTS):
        loc = loc + jnp.where(e == float(ex), loff_ref[tile * N_EXPERTS + ex].astype(F32), 0.0)
    return loc


def _dispatch_kernel(loff_ref, slen_ref, goff_ref, xn_ref, info_ref, zero_ref, xs_ref, buf, sem):
    del zero_ref
    i = pl.program_id(0)
    n = pl.num_programs(0)
    slot = i % 2

    def copy_for(s):
        def make(lo, go, rows):
            return pltpu.make_async_copy(buf.at[s, pl.ds(lo, rows), :], xs_ref.at[pl.ds(go, rows), :],
                                         sem.at[s])
        return make

    wait = lambda c: c.wait()
    start = lambda c: c.start()

    @pl.when(i >= 2)
    def _():
        _segment_copies(i - 2, loff_ref, slen_ref, goff_ref, copy_for(slot), wait)

    info_t = info_ref[...].T
    row_of = lambda r: info_t[r:r + 1, :]
    loc1 = _local_rows(row_of, loff_ref, i, INFO_E1, INFO_R1)
    loc2 = _local_rows(row_of, loff_ref, i, INFO_E2, INFO_R2)
    ridx = lax.broadcasted_iota(jnp.int32, (LOCAL_ROWS, TOK_TILE), 0).astype(F32)
    select = (jnp.where(ridx == loc1, 1.0, 0.0) + jnp.where(ridx == loc2, 1.0, 0.0)).astype(BF16)
    buf[slot] = _dot(select, xn_ref[...]).astype(BF16)
    _segment_copies(i, loff_ref, slen_ref, goff_ref, copy_for(slot), start)

    @pl.when(i == n - 1)
    def _():
        _segment_copies(i, loff_ref, slen_ref, goff_ref, copy_for(slot), wait)

        @pl.when(i >= 1)
        def _():
            _segment_copies(i - 1, loff_ref, slen_ref, goff_ref, copy_for(1 - slot), wait)


def _dispatch(loff, slen, goff, xn, info, n_sort):
    rows = xn.shape[0]
    nt = rows // TOK_TILE
    zeros = jnp.zeros((n_sort, D_MODEL), BF16)
    return pl.pallas_call(
        _dispatch_kernel,
        grid_spec=pltpu.PrefetchScalarGridSpec(
            num_scalar_prefetch=3,
            grid=(nt,),
            in_specs=[pl.BlockSpec((TOK_TILE, D_MODEL), lambda i, *_: (i, 0)),
                      pl.BlockSpec((TOK_TILE, LANES), lambda i, *_: (i, 0)),
                      pl.BlockSpec(memory_space=pl.ANY)],
            out_specs=pl.BlockSpec(memory_space=pl.ANY),
            scratch_shapes=[pltpu.VMEM((2, LOCAL_ROWS, D_MODEL), BF16),
                            pltpu.SemaphoreType.DMA((2,))],
        ),
        out_shape=jax.ShapeDtypeStruct((n_sort, D_MODEL), BF16),
        input_output_aliases={5: 0},
        compiler_params=_cparams(("arbitrary",)),
        name="moe_dispatch",
    )(loff, slen, goff, xn, info, zeros)


def _moe_ffn_kernel(te_ref, tot_ref, xs_ref, wg_ref, wu_ref, wd_ref, y_ref, acc_ref):
    del te_ref
    j = pl.program_id(0)
    f = pl.program_id(1)

    @pl.when(j < tot_ref[0])
    def _():
        @pl.when(f == 0)
        def _():
            acc_ref[...] = jnp.zeros_like(acc_ref)

        x = xs_ref[...]
        a = _silu(_dot(x, wg_ref[0])) * _dot(x, wu_ref[0])
        acc_ref[...] += _dot(a.astype(BF16), wd_ref[0])

        @pl.when(f == pl.num_programs(1) - 1)
        def _():
            y_ref[...] = acc_ref[...].astype(y_ref.dtype)

    @pl.when((j >= tot_ref[0]) & (f == pl.num_programs(1) - 1))
    def _():
        y_ref[...] = jnp.zeros_like(y_ref)


def _moe_ffn(tile_e, total, xs, wg, wu, wd, fc):
    n_sort = xs.shape[0]
    nt = n_sort // FFN_TILE
    nf = wg.shape[2] // fc
    tile_of = lambda j, tot: jnp.minimum(j, tot[0] - 1)
    chunk_of = lambda j, f, tot: jnp.where(j < tot[0], f, nf - 1)
    return pl.pallas_call(
        _moe_ffn_kernel,
        grid_spec=pltpu.PrefetchScalarGridSpec(
            num_scalar_prefetch=2,
            grid=(nt, nf),
            in_specs=[pl.BlockSpec((FFN_TILE, D_MODEL), lambda j, f, te, tot: (tile_of(j, tot), 0)),
                      pl.BlockSpec((1, D_MODEL, fc), lambda j, f, te, tot: (te[j], 0, chunk_of(j, f, tot))),
                      pl.BlockSpec((1, D_MODEL, fc), lambda j, f, te, tot: (te[j], 0, chunk_of(j, f, tot))),
                      pl.BlockSpec((1, fc, D_MODEL), lambda j, f, te, tot: (te[j], chunk_of(j, f, tot), 0))],
            out_specs=pl.BlockSpec((FFN_TILE, D_MODEL), lambda j, f, te, tot: (j, 0)),
            scratch_shapes=[pltpu.VMEM((FFN_TILE, D_MODEL), F32)],
        ),
        out_shape=jax.ShapeDtypeStruct((n_sort, D_MODEL), BF16),
        compiler_params=_cparams(("arbitrary", "arbitrary")),
        name="moe_ffn",
    )(tile_e, total, xs, wg, wu, wd)


def _combine_kernel(loff_ref, slen_ref, goff_ref, info_ref, h_ref, fn_ref, y_ref, o_ref, buf, sem):
    i = pl.program_id(0)
    n = pl.num_programs(0)
    slot = i % 2

    def copy_for(s):
        def make(lo, go, rows):
            return pltpu.make_async_copy(y_ref.at[pl.ds(go, rows), :], buf.at[s, pl.ds(lo, rows), :],
                                         sem.at[s])
        return make

    wait = lambda c: c.wait()
    start = lambda c: c.start()

    @pl.when(i == 0)
    def _():
        buf[...] = jnp.zeros_like(buf)
        _segment_copies(i, loff_ref, slen_ref, goff_ref, copy_for(slot), start)

    @pl.when(i + 1 < n)
    def _():
        _segment_copies(i + 1, loff_ref, slen_ref, goff_ref, copy_for(1 - slot), start)

    _segment_copies(i, loff_ref, slen_ref, goff_ref, copy_for(slot), wait)

    ys = buf[slot]
    cidx = lax.broadcasted_iota(jnp.int32, (PROJ_SUB, LOCAL_ROWS), 1).astype(F32)
    for r in range(TOK_TILE // PROJ_SUB):
        rows = slice(r * PROJ_SUB, (r + 1) * PROJ_SUB)
        info = info_ref[rows, :]
        col_of = lambda c, info=info: info[:, c:c + 1]
        loc1 = _local_rows(col_of, loff_ref, i, INFO_E1, INFO_R1)
        loc2 = _local_rows(col_of, loff_ref, i, INFO_E2, INFO_R2)
        pt = (jnp.where(cidx == loc1, col_of(INFO_W1), 0.0)
              + jnp.where(cidx == loc2, col_of(INFO_W2), 0.0)).astype(BF16)
        o_ref[rows, :] = _rms(h_ref[rows, :] + _dot(pt, ys), fn_ref[...])


def _combine(loff, slen, goff, info, h, final_norm, y):
    rows = h.shape[0]
    nt = rows // TOK_TILE
    return pl.pallas_call(
        _combine_kernel,
        grid_spec=pltpu.PrefetchScalarGridSpec(
            num_scalar_prefetch=3,
            grid=(nt,),
            in_specs=[pl.BlockSpec((TOK_TILE, LANES), lambda i, *_: (i, 0)),
                      pl.BlockSpec((TOK_TILE, D_MODEL), lambda i, *_: (i, 0)),
                      pl.BlockSpec((1, D_MODEL), lambda i, *_: (0, 0)),
                      pl.BlockSpec(memory_space=pl.ANY)],
            out_specs=pl.BlockSpec((TOK_TILE, D_MODEL), lambda i, *_: (i, 0)),
            scratch_shapes=[pltpu.VMEM((2, LOCAL_ROWS, D_MODEL), BF16),
                            pltpu.SemaphoreType.DMA((2,))],
        ),
        out_shape=jax.ShapeDtypeStruct((rows, D_MODEL), F32),
        compiler_params=_cparams(("arbitrary",)),
        name="moe_combine",
    )(loff, slen, goff, info, h, final_norm, y)


def _routing_tables(cnt, n_tiles_max):
    slen = (cnt + SEG_ALIGN - 1) // SEG_ALIGN * SEG_ALIGN
    loff = jnp.cumsum(slen, axis=1) - slen
    rows_e = jnp.sum(slen, axis=0)
    rows_pad = (rows_e + FFN_TILE - 1) // FFN_TILE * FFN_TILE
    base = jnp.cumsum(rows_pad) - rows_pad
    goff = base[None, :] + jnp.cumsum(slen, axis=0) - slen
    tile_end = jnp.cumsum(rows_pad // FFN_TILE)
    total = tile_end[-1]
    j = jnp.arange(n_tiles_max, dtype=jnp.int32)
    tile_e = jnp.sum(jnp.minimum(j, total - 1)[:, None] >= tile_end[None, :], axis=1).astype(jnp.int32)
    flat = lambda a: a.reshape(-1).astype(jnp.int32)
    return flat(loff), flat(slen), flat(goff), tile_e, total.reshape(1).astype(jnp.int32)


def _moe(o, wo, h, gain, wr, wg, wu, wd, final_norm):
    rows = h.shape[0]
    assert rows % TOK_TILE == 0
    nt = rows // TOK_TILE
    max_rows = 2 * rows + nt * N_EXPERTS * (SEG_ALIGN - 1) + N_EXPERTS * (FFN_TILE - SEG_ALIGN)
    n_tiles_max = -(-max_rows // FFN_TILE)
    tri = jnp.tril(jnp.ones((PROJ_SUB, PROJ_SUB), BF16))
    h, xn, info, cnt = _router(o, wo, h, gain, wr, tri)
    cnt = cnt.reshape(nt, SUBLANES, LANES)[:, 0, :N_EXPERTS].astype(jnp.int32)
    loff, slen, goff, tile_e, total = _routing_tables(cnt, n_tiles_max)
    xs = _dispatch(loff, slen, goff, xn, info, n_tiles_max * FFN_TILE)
    y = _moe_ffn(tile_e, total, xs, wg, wu, wd, MOE_CHUNK)
    return _combine(loff, slen, goff, info, h, final_norm, y)


def _ssm_params(lam_re, lam_im, log_dt, b_re, b_im, c_re, c_im, d, glu_w, glu_b):
    g, n, hh = SSM_GROUPS, SSM_STATE, SSM_GROUP
    lr, li = lam_re.astype(F32), lam_im.astype(F32)
    dt = jnp.exp(log_dt.astype(F32))[:, None]
    mag = jnp.exp(dt * lr)
    ar, ai = mag * jnp.cos(dt * li), mag * jnp.sin(dt * li)
    den = lr * lr + li * li
    nr = ar - 1.0
    zr = (nr * lr + ai * li) / den
    zi = (ai * lr - nr * li) / den
    br = zr[..., None] * b_re - zi[..., None] * b_im
    bi = zr[..., None] * b_im + zi[..., None] * b_re
    eye = jnp.eye(g, dtype=F32)
    bblk = jnp.concatenate(
        [jnp.einsum("gnh,gk->ghkn", x, eye).reshape(SSM_WIDTH, SSM_LANES) for x in (br, bi)], axis=1)
    cblk = jnp.concatenate(
        [jnp.einsum("ghn,gk->gnkh", x, eye).reshape(SSM_LANES, SSM_WIDTH) for x in (c_re, -c_im)], axis=0)

    sub = jnp.arange(SUBLANES, dtype=jnp.int32)[:, None]

    def scan_tables(base_r, base_i, layout):
        def power(m):
            pr, pi = jnp.ones((SUBLANES, g, n), F32), jnp.zeros((SUBLANES, g, n), F32)
            for step in range(1, SUBLANES + 1):
                nr_, ni_ = pr * base_r - pi * base_i, pr * base_i + pi * base_r
                take = (m >= step)[:, :, None]
                pr, pi = jnp.where(take, nr_, pr), jnp.where(take, ni_, pi)
            return pr, pi

        tabs = []
        for s in (1, 2, 4):
            pr, pi = power(jnp.full((SUBLANES, 1), s, jnp.int32))
            keep = (sub >= s).astype(F32)[:, :, None]
            tabs.append(layout(pr * keep, pi * keep))
        tabs.append(layout(*power(sub + 1)))
        return jnp.concatenate(tabs, axis=0)

    flat = lambda pr, pi: jnp.concatenate([pr.reshape(SUBLANES, SSM_LANES), pi.reshape(SUBLANES, SSM_LANES)], axis=1)

    q, nv, gl = SSM_Q, SSM_TILES, SSM_GROUPS // SSM_TILES
    pw_r, pw_i = [jnp.ones_like(ar)], [jnp.zeros_like(ar)]
    for _ in range(q):
        pw_r, pw_i = pw_r + [pw_r[-1] * ar - pw_i[-1] * ai], pw_i + [pw_r[-1] * ai + pw_i[-1] * ar]
    pw_r, pw_i = jnp.stack(pw_r), jnp.stack(pw_i)
    cr, ci = c_re.astype(F32), c_im.astype(F32)

    def expand(compact, row_w, col_w):
        s_idx = jnp.arange(LANES)[:, None]
        c_idx = jnp.arange(SSM_VW)[None, :]
        onehot = (s_idx == (c_idx // (gl * col_w)) * col_w + c_idx % col_w).astype(BF16)
        full = jnp.einsum("vrs,sc->vrc", compact.astype(BF16), onehot, preferred_element_type=F32)
        r_idx = jnp.arange(SSM_VW)[:, None]
        same_group = (r_idx // row_w) % gl == (c_idx // col_w) % gl
        return jnp.where(same_group[None], full, 0.0).astype(BF16)

    tiled = lambda pr, pi: jnp.stack([pr, pi], axis=1).reshape(SUBLANES, 2, nv, gl, n).transpose(
        0, 2, 1, 3, 4).reshape(SUBLANES, 2 * SSM_LANES)
    crt, cit = cr.transpose(0, 2, 1)[:, :, :, None], ci.transpose(0, 2, 1)[:, :, :, None]
    m_r = crt * br[:, :, None, :] - cit * bi[:, :, None, :]
    m_i = crt * bi[:, :, None, :] + cit * br[:, :, None, :]
    kern = [jnp.sum(pw_r[dd][:, :, None, None] * m_r - pw_i[dd][:, :, None, None] * m_i, axis=1)
            for dd in range(q)]
    none = jnp.zeros_like(kern[0])
    kf = jnp.stack([jnp.stack([kern[o - i] if o >= i else none for o in range(q)]) for i in range(q)])
    kf = kf.transpose(2, 0, 1, 3, 4)
    kf = kf.reshape(nv, gl, q, q, hh, hh).transpose(0, 2, 1, 5, 3, 4)
    tz = expand(kf.reshape(nv, SSM_VW, LANES), hh, hh)
    rev_r = jnp.stack([pw_r[q - 1 - j] for j in range(q)])[:, :, :, None]
    rev_i = jnp.stack([pw_i[q - 1 - j] for j in range(q)])[:, :, :, None]
    wb = jnp.stack([rev_r * br[None] - rev_i * bi[None], rev_r * bi[None] + rev_i * br[None]])
    wb = wb.reshape(2, q, nv, gl, n, hh).transpose(2, 1, 3, 5, 0, 4)
    w2 = expand(wb.reshape(nv, SSM_VW, LANES), hh, n)
    nx_r, nx_i = pw_r[1:][:, :, None, :], pw_i[1:][:, :, None, :]
    wc = jnp.stack([cr[None] * nx_r - ci[None] * nx_i, -(cr[None] * nx_i + ci[None] * nx_r)])
    wc = wc.reshape(2, q, nv, gl, hh, n).transpose(2, 0, 3, 5, 1, 4)
    w1 = expand(wc.reshape(nv, SSM_VW, LANES), n, hh)
    return {
        "bblk": bblk.astype(BF16), "cblk": cblk.astype(BF16),
        "d": d.reshape(1, SSM_WIDTH).astype(F32),
        "glu_w": glu_w.astype(BF16), "glu_b": glu_b.reshape(1, SSM_WIDTH).astype(F32),
        "powers": scan_tables(ar, ai, flat),
        "chunk_powers": scan_tables(pw_r[q], pw_i[q], tiled),
        "tz": tz.astype(BF16), "w2": w2.astype(BF16), "w1": w1.astype(BF16),
    }


def _row(x):
    return x.reshape(1, -1).astype(F32)


def _pad_cols(w, n):
    return jnp.pad(w, ((0, 0), (0, n - w.shape[1])))


def kernel(x, meta_tokens, ev_norm_mix, ev_w_in, ev_conv_w, ev_conv_b, ev_gate_a_w, ev_gate_a_b, ev_gate_x_w, ev_gate_x_b, ev_lru_lambda, ev_ssm_lambda_re, ev_ssm_lambda_im, ev_ssm_log_dt, ev_ssm_b_re, ev_ssm_b_im, ev_ssm_c_re, ev_ssm_c_im, ev_ssm_d, ev_ssm_glu_w, ev_ssm_glu_b, ev_w_out, ev_norm_ffn, ev_ffn_w_gate, ev_ffn_w_up, ev_ffn_w_down, od_norm_mix, od_w_in, od_gla_gate_w2, od_gla_gate_b, od_gla_norm, od_w_out, od_norm_ffn, od_router_w, od_moe_w_gate, od_moe_w_up, od_moe_w_down, final_norm):
    nb, seq, _ = x.shape
    rows = nb * seq
    assert ev_w_in.shape[0] == 1 and od_w_in.shape[0] == 1, "two-layer trunk only"

    w_in0 = ev_w_in[0].astype(BF16)
    lru_p = {
        "conv_w": ev_conv_w[0].astype(F32), "conv_b": _row(ev_conv_b[0]),
        "gate_a_w": ev_gate_a_w[0].astype(BF16), "gate_a_b": _row(ev_gate_a_b[0]),
        "gate_x_w": ev_gate_x_w[0].astype(BF16), "gate_x_b": _row(ev_gate_x_b[0]),
        "softplus": _row(jax.nn.softplus(-ev_lru_lambda[0].astype(F32))),
    }
    ssm_p = _ssm_params(ev_ssm_lambda_re[0], ev_ssm_lambda_im[0], ev_ssm_log_dt[0], ev_ssm_b_re[0],
                        ev_ssm_b_im[0], ev_ssm_c_re[0], ev_ssm_c_im[0], ev_ssm_d[0],
                        ev_ssm_glu_w[0], ev_ssm_glu_b[0])
    w_out0 = ev_w_out[0].astype(BF16)
    w_out0_a, w_out0_b = w_out0[:D_MODEL], w_out0[D_MODEL:]
    ffn_g, ffn_u, ffn_d = (w[0].astype(BF16) for w in (ev_ffn_w_gate, ev_ffn_w_up, ev_ffn_w_down))
    odd_in = 2 * GLA_KEY + 2 * GLA_VAL
    w_in1 = _pad_cols(od_w_in[0], odd_in + LANES).astype(BF16)
    w2 = jnp.pad(od_gla_gate_w2[0].astype(F32), ((0, LANES - GLA_RANK), (0, 0)))
    w2_hi = w2.astype(BF16)
    w2_split = jnp.stack([w2_hi, (w2 - w2_hi.astype(F32)).astype(BF16)])
    wr = _pad_cols(od_router_w[0].astype(F32), LANES)
    wr_hi = wr.astype(BF16)
    wr_split = jnp.concatenate([wr_hi, (wr - wr_hi.astype(F32)).astype(BF16)], axis=1)
    w_out1 = od_w_out[0].astype(BF16)
    moe_g, moe_u, moe_d = (w[0].astype(BF16) for w in (od_moe_w_gate, od_moe_w_up, od_moe_w_down))

    tl = min(512, seq)
    tm = min(1024, rows)
    tm_ffn = min(512, rows)
    ck = min(128, tl)
    tl_ssm = min(2048, seq)

    def even_mixer(h, nbatch, tile_m, tile_f, tile_l, conv0, h0, s0):
        permuted = tile_l % PERM_BLOCK == 0
        a_out, u, conv_t, h_t = _in_proj_lru(h, _row(ev_norm_mix[0]), w_in0, lru_p, conv0, h0, nbatch, tile_l,
                                             permuted)
        if permuted:
            b_out, s_t = _ssm_chunked(u, ssm_p, s0, nbatch, tl_ssm)
        else:
            b_out, s_t = _ssm(u, ssm_p, s0, nbatch, tile_l)
            s_t = s_t.reshape(SUBLANES, 2, SSM_TILES, SSM_LANES // SSM_TILES).transpose(0, 2, 1, 3).reshape(
                SUBLANES, 2 * SSM_LANES)
        h = _mix_ffn(a_out, b_out, w_out0_a, w_out0_b, h, _row(ev_norm_ffn[0]), ffn_g, ffn_u, ffn_d, tile_f,
                     permuted)
        return h, (conv_t, h_t, s_t)

    def gla_inputs(h, tile_m):
        return _gla_proj(h, _row(od_norm_mix[0]), w_in1, w2_split, _row(od_gla_gate_b[0]), tile_m)

    zeros = lambda *s: jnp.zeros(s, F32)
    hm = meta_tokens.astype(F32)
    hm, (conv_m, h_m, s_m) = even_mixer(hm, 1, N_META, N_META, N_META, zeros(SUBLANES, D_MODEL),
                                        zeros(SUBLANES, D_MODEL), zeros(SUBLANES, 2 * SSM_LANES))
    qm, km, vm, ogm, lgm = gla_inputs(hm, N_META)
    _, gla_s = _gla(qm, km, vm, ogm, lgm, _row(od_gla_norm[0]),
                    zeros(GLA_HEADS, GLA_DV, GLA_DK), 1, N_META, N_META)

    h = x.reshape(rows, D_MODEL).astype(F32)
    h, _ = even_mixer(h, nb, tm, tm_ffn, tl, conv_m, h_m, s_m)
    q, k, v, og, lg = gla_inputs(h, tm)
    o, _ = _gla(q, k, v, og, lg, _row(od_gla_norm[0]), gla_s, nb, tl, ck)
    out = _moe(o, w_out1, h, _row(od_norm_ffn[0]), wr_split, moe_g, moe_u, moe_d, _row(final_norm))
    return out.reshape(nb, seq, D_MODEL)
```

```python
import functools
import math

import jax
import jax.numpy as jnp
from jax import lax
from jax.experimental import pallas as pl
from jax.experimental.pallas import tpu as pltpu

F32 = jnp.float32
BF16 = jnp.bfloat16

D_MODEL = 1024
N_META = 16
EPS = 1e-6
LRU_BLOCKS = 4
LRU_BLOCK = 256
LRU_C = 8.0
SSM_WIDTH = 512
SSM_GROUP = 16
SSM_GROUPS = 32
SSM_STATE = 64
SSM_LANES = SSM_GROUPS * SSM_STATE
D_FF = 3072
GLA_HEADS = 4
GLA_DK = 128
GLA_DV = 256
GLA_KEY = 512
GLA_VAL = 1024
GLA_RANK = 16
GLA_TAU = 16.0
N_EXPERTS = 8
D_FF_EXPERT = 3584

SUBLANES = 8
LANES = 128
CHUNK_STEPS = 8
PERM_BLOCK = 256
VMEM_LIMIT = 52 * 1024 * 1024


def _cparams(sem):
    return pltpu.CompilerParams(dimension_semantics=sem, vmem_limit_bytes=VMEM_LIMIT)


def _dot(a, b):
    return jnp.dot(a, b, preferred_element_type=F32)


def _rms(x, gain):
    ms = jnp.mean(x * x, axis=-1, keepdims=True)
    return x * lax.rsqrt(ms + EPS) * gain


def _gelu_tanh(x):
    c = math.sqrt(2.0 / math.pi)
    return 0.5 * x * (1.0 + jnp.tanh(c * (x + 0.044715 * (x * x * x))))


def _sigmoid(x):
    return 0.5 * jnp.tanh(0.5 * x) + 0.5


def _silu(x):
    return x * _sigmoid(x)


def _split_bf16(x):
    hi = x.astype(BF16)
    lo = (x - hi.astype(F32)).astype(BF16)
    return hi, lo


PROJ_SUB = 256


def _gla_proj_kernel(x_ref, g_ref, w_ref, w2_ref, b2_ref, q_ref, k_ref, v_ref, og_ref, lg_ref):
    tm = x_ref.shape[0]
    sub = min(PROJ_SUB, tm)
    o0 = 2 * GLA_KEY + GLA_VAL
    for r in range(tm // sub):
        rows = slice(r * sub, (r + 1) * sub)
        xn = _rms(x_ref[rows, :], g_ref[...]).astype(BF16)
        q_ref[rows, :] = _dot(xn, w_ref[:, 0:GLA_KEY]).astype(BF16)
        k_ref[rows, :] = _dot(xn, w_ref[:, GLA_KEY:2 * GLA_KEY]).astype(BF16)
        v_ref[rows, :] = _dot(xn, w_ref[:, 2 * GLA_KEY:o0]).astype(BF16)
        og_ref[rows, :] = _dot(xn, w_ref[:, o0:o0 + GLA_VAL]).astype(BF16)
        glr = _dot(xn, w_ref[:, o0 + GLA_VAL:o0 + GLA_VAL + LANES])
        glr_hi, glr_lo = _split_bf16(glr)
        w2_hi = w2_ref[0]
        w2_lo = w2_ref[1]
        pre = _dot(glr_hi, w2_hi) + _dot(glr_lo, w2_hi) + _dot(glr_hi, w2_lo) + b2_ref[...]
        lg_ref[rows, :] = (jnp.minimum(pre, 0.0) - jnp.log1p(jnp.exp(-jnp.abs(pre)))) * (1.0 / GLA_TAU)


def _gla_proj(x, gain, w, w2, b2, tm):
    rows = x.shape[0]
    n = w.shape[1]
    row_spec = lambda c: pl.BlockSpec((tm, c), lambda i: (i, 0))
    return pl.pallas_call(
        _gla_proj_kernel,
        grid=(rows // tm,),
        in_specs=[
            row_spec(D_MODEL),
            pl.BlockSpec((1, D_MODEL), lambda i: (0, 0)),
            _resident((D_MODEL, n)),
            pl.BlockSpec((2, LANES, GLA_KEY), lambda i: (0, 0, 0)),
            pl.BlockSpec((1, GLA_KEY), lambda i: (0, 0)),
        ],
        out_specs=[row_spec(GLA_KEY), row_spec(GLA_KEY), row_spec(GLA_VAL), row_spec(GLA_VAL),
                   row_spec(GLA_KEY)],
        out_shape=[jax.ShapeDtypeStruct((rows, GLA_KEY), BF16),
                   jax.ShapeDtypeStruct((rows, GLA_KEY), BF16),
                   jax.ShapeDtypeStruct((rows, GLA_VAL), BF16),
                   jax.ShapeDtypeStruct((rows, GLA_VAL), BF16),
                   jax.ShapeDtypeStruct((rows, GLA_KEY), F32)],
        compiler_params=_cparams(("parallel",)),
        name="gla_proj",
    )(x, gain, w, w2, b2)


def _affine_scan8(a, b, row):
    for s in (1, 2, 4):
        m = row >= s
        a_sh = pltpu.roll(a, s, 0)
        b_sh = pltpu.roll(b, s, 0)
        b = jnp.where(m, a * b_sh + b, b)
        a = jnp.where(m, a * a_sh, a)
    return a, b


def _lru_tile(gate_ref, rec_ref, cw_ref, cb_ref, wa_ref, ba_ref, wx_ref, bx_ref, sp_ref,
              conv0_ref, h0_ref, out_ref, conv_t_ref, h_t_ref, xbuf, xcbuf, abuf, bbuf, hprev, hcar,
              *, tl, permuted, restart, between=()):
    between = list(between)
    pause = lambda: between.pop(0)() if between else None
    if restart is not None:
        @pl.when(restart)
        def _():
            xbuf[0:SUBLANES, :] = conv0_ref[...]
            hcar[...] = h0_ref[...]

    row = lax.broadcasted_iota(jnp.int32, (SUBLANES, D_MODEL), 0)
    ns = PERM_BLOCK // CHUNK_STEPS
    slab = lambda b, j: slice(b * PERM_BLOCK + j * ns, b * PERM_BLOCK + (j + 1) * ns)

    if permuted:
        srow = lax.broadcasted_iota(jnp.int32, (ns, D_MODEL), 0)
        for b in range(tl // PERM_BLOCK):
            x = [rec_ref[slab(b, j), :].astype(F32) for j in range(CHUNK_STEPS)]
            back = {j: jnp.where(srow == 0, xbuf[j:j + 1, :], pltpu.roll(x[j], 1, 0))
                    for j in range(CHUNK_STEPS - 3, CHUNK_STEPS)}
            for j in range(CHUNK_STEPS):
                xc = cb_ref[...]
                for k in range(4):
                    src = x[j - k] if j >= k else back[j - k + CHUNK_STEPS]
                    xc = xc + cw_ref[3 - k:4 - k, :] * src
                xcbuf[slab(b, j), :] = xc
            last = [x[j][ns - 1:ns, :] for j in range(CHUNK_STEPS - 3, CHUNK_STEPS)]
            xbuf[0:SUBLANES, :] = jnp.where(row == 5, last[0], jnp.where(row == 6, last[1], last[2]))
    else:
        xbuf[SUBLANES:SUBLANES + tl, :] = rec_ref[...].astype(F32)
        xc = cb_ref[...]
        for tap in range(4):
            xc = xc + cw_ref[tap:tap + 1, :] * xbuf[SUBLANES - 3 + tap:SUBLANES - 3 + tap + tl, :]
        xcbuf[...] = xc
        xbuf[0:SUBLANES, :] = xbuf[tl:tl + SUBLANES, :]
    pause()
    for blk in range(LRU_BLOCKS):
        pause()
        sl = slice(blk * LRU_BLOCK, (blk + 1) * LRU_BLOCK)
        xs = xcbuf[:, sl]
        xb = xs.astype(BF16)
        r = _sigmoid(_dot(xb, wa_ref[blk]) + ba_ref[:, sl])
        i = _sigmoid(_dot(xb, wx_ref[blk]) + bx_ref[:, sl])
        a = jnp.exp(-LRU_C * r * sp_ref[:, sl])
        abuf[:, sl] = a
        v = 1.0 - a * a
        bbuf[:, sl] = jnp.where(v > 0.0, v * lax.rsqrt(v), 0.0) * (i * xs)

    last_row = lambda v: jnp.broadcast_to(v[SUBLANES - 1:SUBLANES, :], (SUBLANES, D_MODEL))
    if permuted:
        carry = hcar[...]
        for b in range(tl // PERM_BLOCK):
            for j in range(1, CHUNK_STEPS):
                a_j = abuf[slab(b, j), :]
                bbuf[slab(b, j), :] = a_j * bbuf[slab(b, j - 1), :] + bbuf[slab(b, j), :]
                abuf[slab(b, j), :] = a_j * abuf[slab(b, j - 1), :]
            top = b * PERM_BLOCK + (CHUNK_STEPS - 1) * ns
            for g in range(ns // SUBLANES):
                rg = slice(top + g * SUBLANES, top + (g + 1) * SUBLANES)
                a, bb = _affine_scan8(abuf[rg, :], bbuf[rg, :], row)
                h = a * carry + bb
                hprev[g * SUBLANES:(g + 1) * SUBLANES, :] = jnp.where(row == 0, carry, pltpu.roll(h, 1, 0))
                carry = last_row(h)
            for j in range(CHUNK_STEPS):
                bbuf[slab(b, j), :] = abuf[slab(b, j), :] * hprev[...] + bbuf[slab(b, j), :]
    else:
        def body(gidx, carry):
            r0 = pl.multiple_of(gidx * SUBLANES, SUBLANES)
            a, b = _affine_scan8(abuf[pl.ds(r0, SUBLANES), :], bbuf[pl.ds(r0, SUBLANES), :], row)
            h = a * carry + b
            bbuf[pl.ds(r0, SUBLANES), :] = h
            return last_row(h)

        carry = lax.fori_loop(0, tl // SUBLANES, body, hcar[...])
    hcar[...] = carry
    pause()
    out_ref[...] = (_gelu_tanh(gate_ref[...].astype(F32)) * bbuf[...]).astype(out_ref.dtype)
    while between:
        pause()
    conv_t_ref[...] = xbuf[0:SUBLANES, :]
    h_t_ref[...] = carry


def _in_proj_lru_kernel(x_ref, g_ref, w_ref, perm_ref, cw_ref, cb_ref, wa_ref, ba_ref, wx_ref, bx_ref, sp_ref,
                        conv0_ref, h0_ref, a_out_ref, u_ref, conv_t_ref, h_t_ref,
                        zbuf, xbuf, xcbuf, abuf, bbuf, hprev, hcar, *, tl, permuted):
    blk = zbuf.shape[1]
    nblk = tl // blk
    block = lambda b: slice(b * blk, (b + 1) * blk)

    def projection(b):
        state = {}

        def normalise():
            xn = _rms(x_ref[block(b), :], g_ref[...]).astype(BF16)
            if permuted:
                xn = _dot(perm_ref[...], xn).astype(BF16)
            state["xn"] = xn

        def columns(c0):
            def run():
                if c0 < 2 * D_MODEL:
                    zbuf[b % 2, :, c0:c0 + SSM_WIDTH] = _dot(state["xn"], w_ref[:, c0:c0 + SSM_WIDTH]).astype(BF16)
                else:
                    u_ref[block(b), :] = _dot(state["xn"], w_ref[:, c0:]).astype(BF16)
            return run

        return [normalise] + [columns(c0) for c0 in range(0, w_ref.shape[1], SSM_WIDTH)]

    for piece in projection(0):
        piece()
    for b in range(nblk):
        z = zbuf.at[b % 2]
        _lru_tile(z.at[:, 0:D_MODEL], z.at[:, D_MODEL:2 * D_MODEL], cw_ref, cb_ref, wa_ref, ba_ref,
                  wx_ref, bx_ref, sp_ref, conv0_ref, h0_ref, a_out_ref.at[block(b), :], conv_t_ref, h_t_ref,
                  xbuf, xcbuf, abuf, bbuf, hprev, hcar, tl=blk, permuted=permuted,
                  restart=(pl.program_id(1) == 0) if b == 0 else None,
                  between=projection(b + 1) if b + 1 < nblk else ())


def _block_perm():
    r_out = jnp.arange(PERM_BLOCK)
    chunks = PERM_BLOCK // CHUNK_STEPS
    src = CHUNK_STEPS * (r_out % chunks) + r_out // chunks
    return (jnp.arange(PERM_BLOCK)[None, :] == src[:, None]).astype(BF16)


def _in_proj_lru(x, gain, w, p, conv0, h0, nb, tl, permuted):
    rows = x.shape[0]
    nt = rows // (nb * tl)
    assert not permuted or tl % PERM_BLOCK == 0
    n_u = w.shape[1] - 2 * D_MODEL
    blk = PERM_BLOCK if permuted else tl
    cur = lambda c: pl.BlockSpec((tl, c), lambda b, t: (b * nt + t, 0))
    state = pl.BlockSpec((SUBLANES, D_MODEL), lambda b, t: (0, 0))
    return pl.pallas_call(
        functools.partial(_in_proj_lru_kernel, tl=tl, permuted=permuted),
        grid=(nb, nt),
        in_specs=[cur(D_MODEL), _resident((1, D_MODEL)), _resident(w.shape), _resident((PERM_BLOCK, PERM_BLOCK)),
                  _resident((4, D_MODEL)), _resident((1, D_MODEL)),
                  _resident((LRU_BLOCKS, LRU_BLOCK, LRU_BLOCK)), _resident((1, D_MODEL)),
                  _resident((LRU_BLOCKS, LRU_BLOCK, LRU_BLOCK)), _resident((1, D_MODEL)),
                  _resident((1, D_MODEL)), _resident((SUBLANES, D_MODEL)), _resident((SUBLANES, D_MODEL))],
        out_specs=[cur(D_MODEL), cur(n_u), state, state],
        out_shape=[jax.ShapeDtypeStruct((rows, D_MODEL), BF16),
                   jax.ShapeDtypeStruct((rows, n_u), BF16),
                   jax.ShapeDtypeStruct((SUBLANES, D_MODEL), F32),
                   jax.ShapeDtypeStruct((SUBLANES, D_MODEL), F32)],
        scratch_shapes=[pltpu.VMEM((2, blk, 2 * D_MODEL), BF16),
                        pltpu.VMEM(((0 if permuted else blk) + SUBLANES, D_MODEL), F32),
                        pltpu.VMEM((blk, D_MODEL), F32),
                        pltpu.VMEM((blk, D_MODEL), F32),
                        pltpu.VMEM((blk, D_MODEL), F32),
                        pltpu.VMEM((PERM_BLOCK // CHUNK_STEPS, D_MODEL), F32),
                        pltpu.VMEM((SUBLANES, D_MODEL), F32)],
        compiler_params=_cparams(("arbitrary", "arbitrary")),
        name="in_proj_lru",
    )(x, gain, w, _block_perm(), p["conv_w"], p["conv_b"], p["gate_a_w"], p["gate_a_b"], p["gate_x_w"],
      p["gate_x_b"], p["softplus"], conv0, h0)


SSM_CHUNK = 256


def _complex_scan(xs, pw_ref, hcar, n_rows, lane_pairs, prev=None):
    row = lax.broadcasted_iota(jnp.int32, (SUBLANES, SSM_CHUNK), 0)
    for re0, im0 in lane_pairs:
        re = slice(re0, re0 + SSM_CHUNK)
        im = slice(im0, im0 + SSM_CHUNK)
        tabs = [(pw_ref[k * SUBLANES:(k + 1) * SUBLANES, re], pw_ref[k * SUBLANES:(k + 1) * SUBLANES, im])
                for k in range(4)]

        def body(gidx, carry, re=re, im=im, tabs=tabs):
            cr, ci = carry
            r0 = pl.multiple_of(gidx * SUBLANES, SUBLANES)
            hr = xs[pl.ds(r0, SUBLANES), re]
            hi = xs[pl.ds(r0, SUBLANES), im]
            for k, s in enumerate((1, 2, 4)):
                pr, pi = tabs[k]
                sr = pltpu.roll(hr, s, 0)
                si = pltpu.roll(hi, s, 0)
                hr, hi = hr + (pr * sr - pi * si), hi + (pr * si + pi * sr)
            pr, pi = tabs[3]
            hr, hi = hr + (pr * cr - pi * ci), hi + (pr * ci + pi * cr)
            xs[pl.ds(r0, SUBLANES), re] = hr
            xs[pl.ds(r0, SUBLANES), im] = hi
            if prev is not None:
                prev[pl.ds(r0, SUBLANES), re] = jnp.where(row == 0, cr, pltpu.roll(hr, 1, 0))
                prev[pl.ds(r0, SUBLANES), im] = jnp.where(row == 0, ci, pltpu.roll(hi, 1, 0))
            last = slice(SUBLANES - 1, SUBLANES)
            return (jnp.broadcast_to(hr[last, :], (SUBLANES, SSM_CHUNK)),
                    jnp.broadcast_to(hi[last, :], (SUBLANES, SSM_CHUNK)))

        cr, ci = lax.fori_loop(0, n_rows // SUBLANES, body, (hcar[:, re], hcar[:, im]))
        hcar[:, re] = cr
        hcar[:, im] = ci


def _ssm_kernel(u_ref, bblk_ref, cblk_ref, d_ref, gw_ref, gb_ref, pw_ref, h0_ref,
                out_ref, h_t_ref, xs, hcar, *, tl):
    @pl.when(pl.program_id(1) == 0)
    def _():
        hcar[...] = h0_ref[...]

    u = u_ref[...]
    xs[...] = _dot(u, bblk_ref[...])
    pairs = [(c * SSM_CHUNK, SSM_LANES + c * SSM_CHUNK) for c in range(SSM_LANES // SSM_CHUNK)]
    _complex_scan(xs, pw_ref, hcar, tl, pairs)
    y = _dot(xs[...].astype(BF16), cblk_ref[...]) + d_ref[...] * u.astype(F32)
    g = _gelu_tanh(y)
    out_ref[...] = (g * jax.nn.sigmoid(_dot(g.astype(BF16), gw_ref[...]) + gb_ref[...])).astype(out_ref.dtype)
    h_t_ref[...] = hcar[...]


def _ssm(u, p, h0, nb, tl):
    rows = u.shape[0]
    nt = rows // (nb * tl)
    tile = pl.BlockSpec((tl, SSM_WIDTH), lambda b, t: (b * nt + t, 0))
    full2 = lambda shp: pl.BlockSpec(shp, lambda b, t: (0, 0))
    return pl.pallas_call(
        functools.partial(_ssm_kernel, tl=tl),
        grid=(nb, nt),
        in_specs=[tile,
                  _resident((SSM_WIDTH, 2 * SSM_LANES)), _resident((2 * SSM_LANES, SSM_WIDTH)),
                  full2((1, SSM_WIDTH)), full2((SSM_WIDTH, SSM_WIDTH)), full2((1, SSM_WIDTH)),
                  full2((4 * SUBLANES, 2 * SSM_LANES)), full2((SUBLANES, 2 * SSM_LANES))],
        out_specs=[tile, full2((SUBLANES, 2 * SSM_LANES))],
        out_shape=[jax.ShapeDtypeStruct((rows, SSM_WIDTH), BF16),
                   jax.ShapeDtypeStruct((SUBLANES, 2 * SSM_LANES), F32)],
        scratch_shapes=[pltpu.VMEM((tl, 2 * SSM_LANES), F32),
                        pltpu.VMEM((SUBLANES, 2 * SSM_LANES), F32)],
        compiler_params=_cparams(("arbitrary", "arbitrary")),
        name="s5_ssm",
    )(u, p["bblk"], p["cblk"], p["d"], p["glu_w"], p["glu_b"], p["powers"], h0)


SSM_Q = CHUNK_STEPS
SSM_PB = PERM_BLOCK
SSM_TILES = SSM_WIDTH // LANES
SSM_VW = SSM_Q * LANES


def _ssm_chunk_kernel(u_ref, w2_ref, tz_ref, w1_ref, pw_ref, d_ref, gw_ref, gb_ref,
                      h0_ref, out_ref, h_t_ref, a_scr, xs, hp, ypm, hcar, *, tl):
    nblk = tl // SSM_PB
    ncb = SSM_PB // SSM_Q
    nc = tl // SSM_Q

    @pl.when(pl.program_id(1) == 0)
    def _():
        hcar[...] = h0_ref[...]

    for b in range(nblk):
        for j in range(SSM_Q):
            a_scr[j, b * ncb:(b + 1) * ncb, :] = u_ref[b * SSM_PB + j * ncb:b * SSM_PB + (j + 1) * ncb, :]

    def u_tile(v):
        return jnp.concatenate([a_scr[j, :, v * LANES:(v + 1) * LANES] for j in range(SSM_Q)], axis=1)

    for v in range(SSM_TILES):
        xs[:, v * SSM_VW:(v + 1) * SSM_VW] = _dot(u_tile(v), w2_ref[v])

    half = SSM_VW // 2
    pairs = [(v * SSM_VW + c * SSM_CHUNK, v * SSM_VW + half + c * SSM_CHUNK)
             for v in range(SSM_TILES) for c in range(half // SSM_CHUNK)]
    _complex_scan(xs, pw_ref, hcar, nc, pairs, prev=hp)

    for v in range(SSM_TILES):
        yv = _dot(u_tile(v), tz_ref[v]) + _dot(hp[:, v * SSM_VW:(v + 1) * SSM_VW].astype(BF16), w1_ref[v])
        for b in range(nblk):
            for j in range(SSM_Q):
                ypm[b * SSM_PB + j * ncb:b * SSM_PB + (j + 1) * ncb, v * LANES:(v + 1) * LANES] = (
                    yv[b * ncb:(b + 1) * ncb, j * LANES:(j + 1) * LANES])

    for b in range(nblk):
        rows = slice(b * SSM_PB, (b + 1) * SSM_PB)
        y = ypm[rows, :] + d_ref[...] * u_ref[rows, :].astype(F32)
        g = _gelu_tanh(y)
        out_ref[rows, :] = (g * jax.nn.sigmoid(_dot(g.astype(BF16), gw_ref[...]) + gb_ref[...])).astype(out_ref.dtype)
    h_t_ref[...] = hcar[...]


def _ssm_chunked(u, p, h0, nb, tl):
    rows = u.shape[0]
    nt = rows // (nb * tl)
    assert tl % SSM_PB == 0
    tile = pl.BlockSpec((tl, SSM_WIDTH), lambda b, t: (b * nt + t, 0))
    wspec = _resident((SSM_TILES, SSM_VW, SSM_VW))
    state = (SUBLANES, 2 * SSM_LANES)
    return pl.pallas_call(
        functools.partial(_ssm_chunk_kernel, tl=tl),
        grid=(nb, nt),
        in_specs=[tile, wspec, wspec, wspec,
                  _resident((4 * SUBLANES, 2 * SSM_LANES)), _resident((1, SSM_WIDTH)),
                  _resident((SSM_WIDTH, SSM_WIDTH)), _resident((1, SSM_WIDTH)), _resident(state)],
        out_specs=[tile, pl.BlockSpec(state, lambda b, t: (0, 0))],
        out_shape=[jax.ShapeDtypeStruct((rows, SSM_WIDTH), BF16), jax.ShapeDtypeStruct(state, F32)],
        scratch_shapes=[pltpu.VMEM((SSM_Q, tl // SSM_Q, SSM_WIDTH), BF16),
                        pltpu.VMEM((tl // SSM_Q, 2 * SSM_LANES), F32),
                        pltpu.VMEM((tl // SSM_Q, 2 * SSM_LANES), F32),
                        pltpu.VMEM((tl, SSM_WIDTH), F32),
                        pltpu.VMEM(state, F32)],
        compiler_params=_cparams(("arbitrary", "arbitrary")),
        name="s5_ssm_chunked",
    )(u, p["w2"], p["tz"], p["w1"], p["chunk_powers"], p["d"], p["glu_w"], p["glu_b"], h0)


FFN_CHUNK = 512


def _resident(shape):
    return pl.BlockSpec(shape, lambda *_: (0,) * len(shape), pipeline_mode=pl.Buffered(1))


def _mix_ffn_kernel(a_ref, b_ref, permt_ref, wa_ref, wb_ref, h_ref, g_ref, wg_ref, wu_ref, wd_ref, o_ref,
                    *, permuted):
    if permuted:
        pt = permt_ref[...]
        blocks = [slice(r, r + PERM_BLOCK) for r in range(0, h_ref.shape[0], PERM_BLOCK)]
        a = jnp.concatenate([_dot(pt, a_ref[rows, :]).astype(BF16) for rows in blocks], axis=0)
        b = jnp.concatenate([_dot(pt, b_ref[rows, :]).astype(BF16) for rows in blocks], axis=0)
    else:
        a, b = a_ref[...], b_ref[...]
    h = h_ref[...] + _dot(a, wa_ref[...]) + _dot(b, wb_ref[...])
    o_ref[...] = h
    xn = _rms(h, g_ref[...]).astype(BF16)
    for c in range(wg_ref.shape[1] // FFN_CHUNK):
        sl = slice(c * FFN_CHUNK, (c + 1) * FFN_CHUNK)
        a = (_silu(_dot(xn, wg_ref[:, sl])) * _dot(xn, wu_ref[:, sl])).astype(BF16)
        o_ref[...] += _dot(a, wd_ref[sl, :])


def _mix_ffn(a, b, wa, wb, h, gain, wg, wu, wd, tm, permuted):
    rows = h.shape[0]
    assert not permuted or tm % PERM_BLOCK == 0
    row_spec = lambda c: pl.BlockSpec((tm, c), lambda i: (i, 0))
    return pl.pallas_call(
        functools.partial(_mix_ffn_kernel, permuted=permuted),
        grid=(rows // tm,),
        in_specs=[row_spec(a.shape[1]), row_spec(b.shape[1]), _resident((PERM_BLOCK, PERM_BLOCK)),
                  _resident(wa.shape), _resident(wb.shape),
                  row_spec(D_MODEL), _resident((1, D_MODEL)),
                  _resident(wg.shape), _resident(wu.shape), _resident(wd.shape)],
        out_specs=row_spec(D_MODEL),
        out_shape=jax.ShapeDtypeStruct((rows, D_MODEL), F32),
        compiler_params=_cparams(("parallel",)),
        name="mix_ffn",
    )(a, b, _block_perm().T, wa, wb, h, gain, wg, wu, wd)


def _gla_kernel(q_ref, k_ref, v_ref, og_ref, lg_ref, hn_ref, s0_ref, o_ref, s_t_ref, st, *, tl, ck):
    @pl.when(pl.program_id(1) == 0)
    def _():
        st[...] = s0_ref[...]

    ri = lax.broadcasted_iota(jnp.int32, (ck, ck), 0)
    ci = lax.broadcasted_iota(jnp.int32, (ck, ck), 1)
    causal = ri >= ci
    tri = causal.astype(F32).astype(BF16)
    scale = GLA_DK ** -0.5

    for c in range(tl // ck):
        rows = slice(c * ck, (c + 1) * ck)
        lg_hi, lg_lo = _split_bf16(lg_ref[rows, :])
        b_all = _dot(tri, lg_hi) + _dot(tri, lg_lo)
        for h in range(GLA_HEADS):
            ks = slice(h * GLA_DK, (h + 1) * GLA_DK)
            vs = slice(h * GLA_DV, (h + 1) * GLA_DV)
            b = b_all[:, ks]
            b_last = b[ck - 1:ck, :]
            q = q_ref[rows, ks].astype(F32)
            k = k_ref[rows, ks].astype(F32)
            v = v_ref[rows, vs]
            qd = (q * (scale * jnp.exp(b))).astype(BF16)
            kd = (k * jnp.exp(-b)).astype(BF16)
            kl = (k * jnp.exp(b_last - b)).astype(BF16)
            att = lax.dot_general(qd, kd, (((1,), (1,)), ((), ())), preferred_element_type=F32)
            att = jnp.where(causal, att, 0.0).astype(BF16)
            s_h = st[h]
            o = _dot(att, v) + lax.dot_general(qd, s_h.astype(BF16), (((1,), (1,)), ((), ())),
                                               preferred_element_type=F32)
            st[h] = s_h * jnp.exp(b_last) + lax.dot_general(v, kl, (((0,), (0,)), ((), ())),
                                                            preferred_element_type=F32)
            o = o * lax.rsqrt(jnp.mean(o * o, axis=-1, keepdims=True) + EPS) * hn_ref[:, vs]
            o_ref[rows, vs] = (o * _silu(og_ref[rows, vs].astype(F32))).astype(o_ref.dtype)
    s_t_ref[...] = st[...]


def _gla(q, k, v, og, lg, head_norm, s0, nb, tl, ck):
    rows = q.shape[0]
    nt = rows // (nb * tl)
    tile = lambda c: pl.BlockSpec((tl, c), lambda b, t: (b * nt + t, 0))
    state = pl.BlockSpec((GLA_HEADS, GLA_DV, GLA_DK), lambda b, t: (0, 0, 0))
    return pl.pallas_call(
        functools.partial(_gla_kernel, tl=tl, ck=ck),
        grid=(nb, nt),
        in_specs=[tile(GLA_KEY), tile(GLA_KEY), tile(GLA_VAL), tile(GLA_VAL), tile(GLA_KEY),
                  pl.BlockSpec((1, GLA_VAL), lambda b, t: (0, 0)), state],
        out_specs=[tile(GLA_VAL), state],
        out_shape=[jax.ShapeDtypeStruct((rows, GLA_VAL), BF16),
                   jax.ShapeDtypeStruct((GLA_HEADS, GLA_DV, GLA_DK), F32)],
        scratch_shapes=[pltpu.VMEM((GLA_HEADS, GLA_DV, GLA_DK), F32)],
        compiler_params=_cparams(("arbitrary", "arbitrary")),
        name="gla",
    )(q, k, v, og, lg, head_norm, s0)


TOK_TILE = 512
SEG_ALIGN = 16
LOCAL_ROWS = 1152
FFN_TILE = 512
MOE_CHUNK = 1792
SEG_BITS = (512, 256, 128, 64, 32, 16)
INFO_E1, INFO_E2, INFO_R1, INFO_R2, INFO_W1, INFO_W2 = range(6)
assert LOCAL_ROWS >= 2 * TOK_TILE + N_EXPERTS * (SEG_ALIGN - 1) and LOCAL_ROWS % LANES == 0


def _router_kernel(o_ref, wo_ref, h_ref, g_ref, wr_ref, tri_ref, hout_ref, xn_ref, info_ref, cnt_ref):
    tm = h_ref.shape[0]
    sub = tri_ref.shape[0]
    w_both = wr_ref[...]
    w_hi = wr_ref[:, 0:LANES]
    lane = lax.broadcasted_iota(jnp.int32, (sub, LANES), 1).astype(F32)
    neg = jnp.float32(-jnp.inf)
    count = jnp.zeros((1, LANES), F32)
    for r in range(tm // sub):
        rows = slice(r * sub, (r + 1) * sub)
        h = h_ref[rows, :] + _dot(o_ref[rows, :], wo_ref[...])
        hout_ref[rows, :] = h
        xn = _rms(h, g_ref[...])
        xn_ref[rows, :] = xn.astype(BF16)
        x_hi, x_lo = _split_bf16(xn)
        both = _dot(x_hi, w_both)
        logits = both[:, 0:LANES] + both[:, LANES:2 * LANES] + _dot(x_lo, w_hi)
        logits = jnp.where(lane < N_EXPERTS, logits, neg)
        v1 = jnp.max(logits, axis=-1, keepdims=True)
        i1 = jnp.min(jnp.where(logits == v1, lane, float(LANES)), axis=-1, keepdims=True)
        rest = jnp.where(lane == i1, neg, logits)
        v2 = jnp.max(rest, axis=-1, keepdims=True)
        i2 = jnp.min(jnp.where(rest == v2, lane, float(LANES)), axis=-1, keepdims=True)
        e2 = jnp.exp(v2 - v1)
        w1 = 1.0 / (1.0 + e2)
        w2 = e2 / (1.0 + e2)
        hit1 = lane == i1
        hit2 = lane == i2
        sel = jnp.where(hit1, 1.0, 0.0) + jnp.where(hit2, 1.0, 0.0)
        cum = _dot(tri_ref[...], sel.astype(BF16)) + count
        count = cum[sub - 1:sub, :]
        rank1 = jnp.sum(jnp.where(hit1, cum, 0.0), axis=-1, keepdims=True) - 1.0
        rank2 = jnp.sum(jnp.where(hit2, cum, 0.0), axis=-1, keepdims=True) - 1.0
        info = jnp.zeros_like(logits)
        for col, val in ((INFO_E1, i1), (INFO_E2, i2), (INFO_R1, rank1), (INFO_R2, rank2),
                         (INFO_W1, w1), (INFO_W2, w2)):
            info = jnp.where(lane == col, val, info)
        info_ref[rows, :] = info
    cnt_ref[...] = jnp.broadcast_to(count, (SUBLANES, LANES))


def _router(o, wo, h, gain, wr, tri):
    rows = h.shape[0]
    tm = TOK_TILE
    nt = rows // tm
    row_spec = lambda c: pl.BlockSpec((tm, c), lambda i: (i, 0))
    return pl.pallas_call(
        _router_kernel,
        grid=(nt,),
        in_specs=[row_spec(o.shape[1]), _resident(wo.shape), row_spec(D_MODEL), _resident((1, D_MODEL)),
                  _resident((D_MODEL, 2 * LANES)), _resident(tri.shape)],
        out_specs=[row_spec(D_MODEL), row_spec(D_MODEL), row_spec(LANES),
                   pl.BlockSpec((SUBLANES, LANES), lambda i: (i, 0))],
        out_shape=[jax.ShapeDtypeStruct((rows, D_MODEL), F32),
                   jax.ShapeDtypeStruct((rows, D_MODEL), BF16),
                   jax.ShapeDtypeStruct((rows, LANES), F32),
                   jax.ShapeDtypeStruct((nt * SUBLANES, LANES), F32)],
        compiler_params=_cparams(("parallel",)),
        name="router",
    )(o, wo, h, gain, wr, tri)


def _segment_copies(tile, loff_ref, slen_ref, goff_ref, make_copy, fn):
    for e in range(N_EXPERTS):
        idx = tile * N_EXPERTS + e
        lo = loff_ref[idx]
        n = slen_ref[idx]
        go = goff_ref[idx]
        for bit in SEG_BITS:
            pos = n & (-2 * bit)

            @pl.when((n & bit) != 0)
            def _(lo=lo, go=go, pos=pos, bit=bit):
                fn(make_copy(pl.multiple_of(lo + pos, SEG_ALIGN), pl.multiple_of(go + pos, SEG_ALIGN), bit))


def _local_rows(info_row, loff_ref, tile, which_e, which_r):
    e = info_row(which_e)
    loc = info_row(which_r)
    for ex in range(N_EXPERTS):
        loc = loc + jnp.where(e == float(ex), loff_ref[tile * N_EXPERTS + ex].astype(F32), 0.0)
    return loc


def _dispatch_kernel(loff_ref, slen_ref, goff_ref, xn_ref, info_ref, zero_ref, xs_ref, buf, sem):
    del zero_ref
    i = pl.program_id(0)
    n = pl.num_programs(0)
    slot = i % 2

    def copy_for(s):
        def make(lo, go, rows):
            return pltpu.make_async_copy(buf.at[s, pl.ds(lo, rows), :], xs_ref.at[pl.ds(go, rows), :],
                                         sem.at[s])
        return make

    wait = lambda c: c.wait()
    start = lambda c: c.start()

    @pl.when(i >= 2)
    def _():
        _segment_copies(i - 2, loff_ref, slen_ref, goff_ref, copy_for(slot), wait)

    info_t = info_ref[...].T
    row_of = lambda r: info_t[r:r + 1, :]
    loc1 = _local_rows(row_of, loff_ref, i, INFO_E1, INFO_R1)
    loc2 = _local_rows(row_of, loff_ref, i, INFO_E2, INFO_R2)
    ridx = lax.broadcasted_iota(jnp.int32, (LOCAL_ROWS, TOK_TILE), 0).astype(F32)
    select = (jnp.where(ridx == loc1, 1.0, 0.0) + jnp.where(ridx == loc2, 1.0, 0.0)).astype(BF16)
    buf[slot] = _dot(select, xn_ref[...]).astype(BF16)
    _segment_copies(i, loff_ref, slen_ref, goff_ref, copy_for(slot), start)

    @pl.when(i == n - 1)
    def _():
        _segment_copies(i, loff_ref, slen_ref, goff_ref, copy_for(slot), wait)

        @pl.when(i >= 1)
        def _():
            _segment_copies(i - 1, loff_ref, slen_ref, goff_ref, copy_for(1 - slot), wait)


def _dispatch(loff, slen, goff, xn, info, n_sort):
    rows = xn.shape[0]
    nt = rows // TOK_TILE
    zeros = jnp.zeros((n_sort, D_MODEL), BF16)
    return pl.pallas_call(
        _dispatch_kernel,
        grid_spec=pltpu.PrefetchScalarGridSpec(
            num_scalar_prefetch=3,
            grid=(nt,),
            in_specs=[pl.BlockSpec((TOK_TILE, D_MODEL), lambda i, *_: (i, 0)),
                      pl.BlockSpec((TOK_TILE, LANES), lambda i, *_: (i, 0)),
                      pl.BlockSpec(memory_space=pl.ANY)],
            out_specs=pl.BlockSpec(memory_space=pl.ANY),
            scratch_shapes=[pltpu.VMEM((2, LOCAL_ROWS, D_MODEL), BF16),
                            pltpu.SemaphoreType.DMA((2,))],
        ),
        out_shape=jax.ShapeDtypeStruct((n_sort, D_MODEL), BF16),
        input_output_aliases={5: 0},
        compiler_params=_cparams(("arbitrary",)),
        name="moe_dispatch",
    )(loff, slen, goff, xn, info, zeros)


def _moe_ffn_kernel(te_ref, tot_ref, xs_ref, wg_ref, wu_ref, wd_ref, y_ref, acc_ref):
    del te_ref
    j = pl.program_id(0)
    f = pl.program_id(1)

    @pl.when(j < tot_ref[0])
    def _():
        @pl.when(f == 0)
        def _():
            acc_ref[...] = jnp.zeros_like(acc_ref)

        x = xs_ref[...]
        a = _silu(_dot(x, wg_ref[0])) * _dot(x, wu_ref[0])
        acc_ref[...] += _dot(a.astype(BF16), wd_ref[0])

        @pl.when(f == pl.num_programs(1) - 1)
        def _():
            y_ref[...] = acc_ref[...].astype(y_ref.dtype)

    @pl.when((j >= tot_ref[0]) & (f == pl.num_programs(1) - 1))
    def _():
        y_ref[...] = jnp.zeros_like(y_ref)


def _moe_ffn(tile_e, total, xs, wg, wu, wd, fc):
    n_sort = xs.shape[0]
    nt = n_sort // FFN_TILE
    nf = wg.shape[2] // fc
    tile_of = lambda j, tot: jnp.minimum(j, tot[0] - 1)
    chunk_of = lambda j, f, tot: jnp.where(j < tot[0], f, nf - 1)
    return pl.pallas_call(
        _moe_ffn_kernel,
        grid_spec=pltpu.PrefetchScalarGridSpec(
            num_scalar_prefetch=2,
            grid=(nt, nf),
            in_specs=[pl.BlockSpec((FFN_TILE, D_MODEL), lambda j, f, te, tot: (tile_of(j, tot), 0)),
                      pl.BlockSpec((1, D_MODEL, fc), lambda j, f, te, tot: (te[j], 0, chunk_of(j, f, tot))),
                      pl.BlockSpec((1, D_MODEL, fc), lambda j, f, te, tot: (te[j], 0, chunk_of(j, f, tot))),
                      pl.BlockSpec((1, fc, D_MODEL), lambda j, f, te, tot: (te[j], chunk_of(j, f, tot), 0))],
            out_specs=pl.BlockSpec((FFN_TILE, D_MODEL), lambda j, f, te, tot: (j, 0)),
            scratch_shapes=[pltpu.VMEM((FFN_TILE, D_MODEL), F32)],
        ),
        out_shape=jax.ShapeDtypeStruct((n_sort, D_MODEL), BF16),
        compiler_params=_cparams(("arbitrary", "arbitrary")),
        name="moe_ffn",
    )(tile_e, total, xs, wg, wu, wd)


def _combine_kernel(loff_ref, slen_ref, goff_ref, info_ref, h_ref, fn_ref, y_ref, o_ref, buf, sem):
    i = pl.program_id(0)
    n = pl.num_programs(0)
    slot = i % 2

    def copy_for(s):
        def make(lo, go, rows):
            return pltpu.make_async_copy(y_ref.at[pl.ds(go, rows), :], buf.at[s, pl.ds(lo, rows), :],
                                         sem.at[s])
        return make

    wait = lambda c: c.wait()
    start = lambda c: c.start()

    @pl.when(i == 0)
    def _():
        buf[...] = jnp.zeros_like(buf)
        _segment_copies(i, loff_ref, slen_ref, goff_ref, copy_for(slot), start)

    @pl.when(i + 1 < n)
    def _():
        _segment_copies(i + 1, loff_ref, slen_ref, goff_ref, copy_for(1 - slot), start)

    _segment_copies(i, loff_ref, slen_ref, goff_ref, copy_for(slot), wait)

    ys = buf[slot]
    cidx = lax.broadcasted_iota(jnp.int32, (PROJ_SUB, LOCAL_ROWS), 1).astype(F32)
    for r in range(TOK_TILE // PROJ_SUB):
        rows = slice(r * PROJ_SUB, (r + 1) * PROJ_SUB)
        info = info_ref[rows, :]
        col_of = lambda c, info=info: info[:, c:c + 1]
        loc1 = _local_rows(col_of, loff_ref, i, INFO_E1, INFO_R1)
        loc2 = _local_rows(col_of, loff_ref, i, INFO_E2, INFO_R2)
        pt = (jnp.where(cidx == loc1, col_of(INFO_W1), 0.0)
              + jnp.where(cidx == loc2, col_of(INFO_W2), 0.0)).astype(BF16)
        o_ref[rows, :] = _rms(h_ref[rows, :] + _dot(pt, ys), fn_ref[...])


def _combine(loff, slen, goff, info, h, final_norm, y):
    rows = h.shape[0]
    nt = rows // TOK_TILE
    return pl.pallas_call(
        _combine_kernel,
        grid_spec=pltpu.PrefetchScalarGridSpec(
            num_scalar_prefetch=3,
            grid=(nt,),
            in_specs=[pl.BlockSpec((TOK_TILE, LANES), lambda i, *_: (i, 0)),
                      pl.BlockSpec((TOK_TILE, D_MODEL), lambda i, *_: (i, 0)),
                      pl.BlockSpec((1, D_MODEL), lambda i, *_: (0, 0)),
                      pl.BlockSpec(memory_space=pl.ANY)],
            out_specs=pl.BlockSpec((TOK_TILE, D_MODEL), lambda i, *_: (i, 0)),
            scratch_shapes=[pltpu.VMEM((2, LOCAL_ROWS, D_MODEL), BF16),
                            pltpu.SemaphoreType.DMA((2,))],
        ),
        out_shape=jax.ShapeDtypeStruct((rows, D_MODEL), F32),
        compiler_params=_cparams(("arbitrary",)),
        name="moe_combine",
    )(loff, slen, goff, info, h, final_norm, y)


def _routing_tables(cnt, n_tiles_max):
    slen = (cnt + SEG_ALIGN - 1) // SEG_ALIGN * SEG_ALIGN
    loff = jnp.cumsum(slen, axis=1) - slen
    rows_e = jnp.sum(slen, axis=0)
    rows_pad = (rows_e + FFN_TILE - 1) // FFN_TILE * FFN_TILE
    base = jnp.cumsum(rows_pad) - rows_pad
    goff = base[None, :] + jnp.cumsum(slen, axis=0) - slen
    tile_end = jnp.cumsum(rows_pad // FFN_TILE)
    total = tile_end[-1]
    j = jnp.arange(n_tiles_max, dtype=jnp.int32)
    tile_e = jnp.sum(jnp.minimum(j, total - 1)[:, None] >= tile_end[None, :], axis=1).astype(jnp.int32)
    flat = lambda a: a.reshape(-1).astype(jnp.int32)
    return flat(loff), flat(slen), flat(goff), tile_e, total.reshape(1).astype(jnp.int32)


def _moe(o, wo, h, gain, wr, wg, wu, wd, final_norm):
    rows = h.shape[0]
    assert rows % TOK_TILE == 0
    nt = rows // TOK_TILE
    max_rows = 2 * rows + nt * N_EXPERTS * (SEG_ALIGN - 1) + N_EXPERTS * (FFN_TILE - SEG_ALIGN)
    n_tiles_max = -(-max_rows // FFN_TILE)
    tri = jnp.tril(jnp.ones((PROJ_SUB, PROJ_SUB), BF16))
    h, xn, info, cnt = _router(o, wo, h, gain, wr, tri)
    cnt = cnt.reshape(nt, SUBLANES, LANES)[:, 0, :N_EXPERTS].astype(jnp.int32)
    loff, slen, goff, tile_e, total = _routing_tables(cnt, n_tiles_max)
    xs = _dispatch(loff, slen, goff, xn, info, n_tiles_max * FFN_TILE)
    y = _moe_ffn(tile_e, total, xs, wg, wu, wd, MOE_CHUNK)
    return _combine(loff, slen, goff, info, h, final_norm, y)


def _ssm_params(lam_re, lam_im, log_dt, b_re, b_im, c_re, c_im, d, glu_w, glu_b):
    g, n, hh = SSM_GROUPS, SSM_STATE, SSM_GROUP
    lr, li = lam_re.astype(F32), lam_im.astype(F32)
    dt = jnp.exp(log_dt.astype(F32))[:, None]
    mag = jnp.exp(dt * lr)
    ar, ai = mag * jnp.cos(dt * li), mag * jnp.sin(dt * li)
    den = lr * lr + li * li
    nr = ar - 1.0
    zr = (nr * lr + ai * li) / den
    zi = (ai * lr - nr * li) / den
    br = zr[..., None] * b_re - zi[..., None] * b_im
    bi = zr[..., None] * b_im + zi[..., None] * b_re
    eye = jnp.eye(g, dtype=F32)
    bblk = jnp.concatenate(
        [jnp.einsum("gnh,gk->ghkn", x, eye).reshape(SSM_WIDTH, SSM_LANES) for x in (br, bi)], axis=1)
    cblk = jnp.concatenate(
        [jnp.einsum("ghn,gk->gnkh", x, eye).reshape(SSM_LANES, SSM_WIDTH) for x in (c_re, -c_im)], axis=0)

    sub = jnp.arange(SUBLANES, dtype=jnp.int32)[:, None]

    def scan_tables(base_r, base_i, layout):
        def power(m):
            pr, pi = jnp.ones((SUBLANES, g, n), F32), jnp.zeros((SUBLANES, g, n), F32)
            for step in range(1, SUBLANES + 1):
                nr_, ni_ = pr * base_r - pi * base_i, pr * base_i + pi * base_r
                take = (m >= step)[:, :, None]
                pr, pi = jnp.where(take, nr_, pr), jnp.where(take, ni_, pi)
            return pr, pi

        tabs = []
        for s in (1, 2, 4):
            pr, pi = power(jnp.full((SUBLANES, 1), s, jnp.int32))
            keep = (sub >= s).astype(F32)[:, :, None]
            tabs.append(layout(pr * keep, pi * keep))
        tabs.append(layout(*power(sub + 1)))
        return jnp.concatenate(tabs, axis=0)

    flat = lambda pr, pi: jnp.concatenate([pr.reshape(SUBLANES, SSM_LANES), pi.reshape(SUBLANES, SSM_LANES)], axis=1)

    q, nv, gl = SSM_Q, SSM_TILES, SSM_GROUPS // SSM_TILES
    pw_r, pw_i = [jnp.ones_like(ar)], [jnp.zeros_like(ar)]
    for _ in range(q):
        pw_r, pw_i = pw_r + [pw_r[-1] * ar - pw_i[-1] * ai], pw_i + [pw_r[-1] * ai + pw_i[-1] * ar]
    pw_r, pw_i = jnp.stack(pw_r), jnp.stack(pw_i)
    cr, ci = c_re.astype(F32), c_im.astype(F32)

    def expand(compact, row_w, col_w):
        s_idx = jnp.arange(LANES)[:, None]
        c_idx = jnp.arange(SSM_VW)[None, :]
        onehot = (s_idx == (c_idx // (gl * col_w)) * col_w + c_idx % col_w).astype(BF16)
        full = jnp.einsum("vrs,sc->vrc", compact.astype(BF16), onehot, preferred_element_type=F32)
        r_idx = jnp.arange(SSM_VW)[:, None]
        same_group = (r_idx // row_w) % gl == (c_idx // col_w) % gl
        return jnp.where(same_group[None], full, 0.0).astype(BF16)

    tiled = lambda pr, pi: jnp.stack([pr, pi], axis=1).reshape(SUBLANES, 2, nv, gl, n).transpose(
        0, 2, 1, 3, 4).reshape(SUBLANES, 2 * SSM_LANES)
    crt, cit = cr.transpose(0, 2, 1)[:, :, :, None], ci.transpose(0, 2, 1)[:, :, :, None]
    m_r = crt * br[:, :, None, :] - cit * bi[:, :, None, :]
    m_i = crt * bi[:, :, None, :] + cit * br[:, :, None, :]
    kern = [jnp.sum(pw_r[dd][:, :, None, None] * m_r - pw_i[dd][:, :, None, None] * m_i, axis=1)
            for dd in range(q)]
    none = jnp.zeros_like(kern[0])
    kf = jnp.stack([jnp.stack([kern[o - i] if o >= i else none for o in range(q)]) for i in range(q)])
    kf = kf.transpose(2, 0, 1, 3, 4)
    kf = kf.reshape(nv, gl, q, q, hh, hh).transpose(0, 2, 1, 5, 3, 4)
    tz = expand(kf.reshape(nv, SSM_VW, LANES), hh, hh)
    rev_r = jnp.stack([pw_r[q - 1 - j] for j in range(q)])[:, :, :, None]
    rev_i = jnp.stack([pw_i[q - 1 - j] for j in range(q)])[:, :, :, None]
    wb = jnp.stack([rev_r * br[None] - rev_i * bi[None], rev_r * bi[None] + rev_i * br[None]])
    wb = wb.reshape(2, q, nv, gl, n, hh).transpose(2, 1, 3, 5, 0, 4)
    w2 = expand(wb.reshape(nv, SSM_VW, LANES), hh, n)
    nx_r, nx_i = pw_r[1:][:, :, None, :], pw_i[1:][:, :, None, :]
    wc = jnp.stack([cr[None] * nx_r - ci[None] * nx_i, -(cr[None] * nx_i + ci[None] * nx_r)])
    wc = wc.reshape(2, q, nv, gl, hh, n).transpose(2, 0, 3, 5, 1, 4)
    w1 = expand(wc.reshape(nv, SSM_VW, LANES), n, hh)
    return {
        "bblk": bblk.astype(BF16), "cblk": cblk.astype(BF16),
        "d": d.reshape(1, SSM_WIDTH).astype(F32),
        "glu_w": glu_w.astype(BF16), "glu_b": glu_b.reshape(1, SSM_WIDTH).astype(F32),
        "powers": scan_tables(ar, ai, flat),
        "chunk_powers": scan_tables(pw_r[q], pw_i[q], tiled),
        "tz": tz.astype(BF16), "w2": w2.astype(BF16), "w1": w1.astype(BF16),
    }


def _row(x):
    return x.reshape(1, -1).astype(F32)


def _pad_cols(w, n):
    return jnp.pad(w, ((0, 0), (0, n - w.shape[1])))


def kernel(x, meta_tokens, ev_norm_mix, ev_w_in, ev_conv_w, ev_conv_b, ev_gate_a_w, ev_gate_a_b, ev_gate_x_w, ev_gate_x_b, ev_lru_lambda, ev_ssm_lambda_re, ev_ssm_lambda_im, ev_ssm_log_dt, ev_ssm_b_re, ev_ssm_b_im, ev_ssm_c_re, ev_ssm_c_im, ev_ssm_d, ev_ssm_glu_w, ev_ssm_glu_b, ev_w_out, ev_norm_ffn, ev_ffn_w_gate, ev_ffn_w_up, ev_ffn_w_down, od_norm_mix, od_w_in, od_gla_gate_w2, od_gla_gate_b, od_gla_norm, od_w_out, od_norm_ffn, od_router_w, od_moe_w_gate, od_moe_w_up, od_moe_w_down, final_norm):
    nb, seq, _ = x.shape
    rows = nb * seq
    assert ev_w_in.shape[0] == 1 and od_w_in.shape[0] == 1, "two-layer trunk only"

    w_in0 = ev_w_in[0].astype(BF16)
    lru_p = {
        "conv_w": ev_conv_w[0].astype(F32), "conv_b": _row(ev_conv_b[0]),
        "gate_a_w": ev_gate_a_w[0].astype(BF16), "gate_a_b": _row(ev_gate_a_b[0]),
        "gate_x_w": ev_gate_x_w[0].astype(BF16), "gate_x_b": _row(ev_gate_x_b[0]),
        "softplus": _row(jax.nn.softplus(-ev_lru_lambda[0].astype(F32))),
    }
    ssm_p = _ssm_params(ev_ssm_lambda_re[0], ev_ssm_lambda_im[0], ev_ssm_log_dt[0], ev_ssm_b_re[0],
                        ev_ssm_b_im[0], ev_ssm_c_re[0], ev_ssm_c_im[0], ev_ssm_d[0],
                        ev_ssm_glu_w[0], ev_ssm_glu_b[0])
    w_out0 = ev_w_out[0].astype(BF16)
    w_out0_a, w_out0_b = w_out0[:D_MODEL], w_out0[D_MODEL:]
    ffn_g, ffn_u, ffn_d = (w[0].astype(BF16) for w in (ev_ffn_w_gate, ev_ffn_w_up, ev_ffn_w_down))
    odd_in = 2 * GLA_KEY + 2 * GLA_VAL
    w_in1 = _pad_cols(od_w_in[0], odd_in + LANES).astype(BF16)
    w2 = jnp.pad(od_gla_gate_w2[0].astype(F32), ((0, LANES - GLA_RANK), (0, 0)))
    w2_hi = w2.astype(BF16)
    w2_split = jnp.stack([w2_hi, (w2 - w2_hi.astype(F32)).astype(BF16)])
    wr = _pad_cols(od_router_w[0].astype(F32), LANES)
    wr_hi = wr.astype(BF16)
    wr_split = jnp.concatenate([wr_hi, (wr - wr_hi.astype(F32)).astype(BF16)], axis=1)
    w_out1 = od_w_out[0].astype(BF16)
    moe_g, moe_u, moe_d = (w[0].astype(BF16) for w in (od_moe_w_gate, od_moe_w_up, od_moe_w_down))

    tl = min(512, seq)
    tm = min(1024, rows)
    tm_ffn = min(512, rows)
    ck = min(128, tl)
    tl_ssm = min(2048, seq)
    tl_lru = min(2048, seq)

    def even_mixer(h, nbatch, tile_m, tile_f, tile_l, conv0, h0, s0):
        permuted = tile_l % PERM_BLOCK == 0
        a_out, u, conv_t, h_t = _in_proj_lru(h, _row(ev_norm_mix[0]), w_in0, lru_p, conv0, h0, nbatch,
                                             tl_lru if permuted else tile_l, permuted)
        if permuted:
            b_out, s_t = _ssm_chunked(u, ssm_p, s0, nbatch, tl_ssm)
        else:
            b_out, s_t = _ssm(u, ssm_p, s0, nbatch, tile_l)
            s_t = s_t.reshape(SUBLANES, 2, SSM_TILES, SSM_LANES // SSM_TILES).transpose(0, 2, 1, 3).reshape(
                SUBLANES, 2 * SSM_LANES)
        h = _mix_ffn(a_out, b_out, w_out0_a, w_out0_b, h, _row(ev_norm_ffn[0]), ffn_g, ffn_u, ffn_d, tile_f,
                     permuted)
        return h, (conv_t, h_t, s_t)

    def gla_inputs(h, tile_m):
        return _gla_proj(h, _row(od_norm_mix[0]), w_in1, w2_split, _row(od_gla_gate_b[0]), tile_m)

    zeros = lambda *s: jnp.zeros(s, F32)
    hm = meta_tokens.astype(F32)
    hm, (conv_m, h_m, s_m) = even_mixer(hm, 1, N_META, N_META, N_META, zeros(SUBLANES, D_MODEL),
                                        zeros(SUBLANES, D_MODEL), zeros(SUBLANES, 2 * SSM_LANES))
    qm, km, vm, ogm, lgm = gla_inputs(hm, N_META)
    _, gla_s = _gla(qm, km, vm, ogm, lgm, _row(od_gla_norm[0]),
                    zeros(GLA_HEADS, GLA_DV, GLA_DK), 1, N_META, N_META)

    h = x.reshape(rows, D_MODEL).astype(F32)
    h, _ = even_mixer(h, nb, tm, tm_ffn, tl, conv_m, h_m, s_m)
    q, k, v, og, lg = gla_inputs(h, tm)
    o, _ = _gla(q, k, v, og, lg, _row(od_gla_norm[0]), gla_s, nb, tl, ck)
    out = _moe(o, w_out1, h, _row(od_norm_ffn[0]), wr_split, moe_g, moe_u, moe_d, _row(final_norm))
    return out.reshape(nb, seq, D_MODEL)
```

```python
import functools
import math

import jax
import jax.numpy as jnp
from jax import lax
from jax.experimental import pallas as pl
from jax.experimental.pallas import tpu as pltpu

F32 = jnp.float32
BF16 = jnp.bfloat16

D_MODEL = 1024
N_META = 16
EPS = 1e-6
LRU_BLOCKS = 4
LRU_BLOCK = 256
LRU_C = 8.0
SSM_WIDTH = 512
SSM_GROUP = 16
SSM_GROUPS = 32
SSM_STATE = 64
SSM_LANES = SSM_GROUPS * SSM_STATE
D_FF = 3072
GLA_HEADS = 4
GLA_DK = 128
GLA_DV = 256
GLA_KEY = 512
GLA_VAL = 1024
GLA_RANK = 16
GLA_TAU = 16.0
N_EXPERTS = 8
D_FF_EXPERT = 3584

SUBLANES = 8
LANES = 128
CHUNK_STEPS = 8
PERM_BLOCK = 256
VMEM_LIMIT = 52 * 1024 * 1024


def _cparams(sem):
    return pltpu.CompilerParams(dimension_semantics=sem, vmem_limit_bytes=VMEM_LIMIT)


def _dot(a, b):
    return jnp.dot(a, b, preferred_element_type=F32)


def _rms(x, gain):
    ms = jnp.mean(x * x, axis=-1, keepdims=True)
    return x * lax.rsqrt(ms + EPS) * gain


def _gelu_tanh(x):
    c = math.sqrt(2.0 / math.pi)
    return 0.5 * x * (1.0 + jnp.tanh(c * (x + 0.044715 * (x * x * x))))


def _sigmoid(x):
    return 0.5 * jnp.tanh(0.5 * x) + 0.5


def _silu(x):
    return x * _sigmoid(x)


def _split_bf16(x):
    hi = x.astype(BF16)
    lo = (x - hi.astype(F32)).astype(BF16)
    return hi, lo


PROJ_SUB = 256


def _gla_proj_kernel(x_ref, g_ref, w_ref, w2_ref, b2_ref, q_ref, k_ref, v_ref, og_ref, lg_ref):
    tm = x_ref.shape[0]
    sub = min(PROJ_SUB, tm)
    o0 = 2 * GLA_KEY + GLA_VAL
    for r in range(tm // sub):
        rows = slice(r * sub, (r + 1) * sub)
        xn = _rms(x_ref[rows, :], g_ref[...]).astype(BF16)
        q_ref[rows, :] = _dot(xn, w_ref[:, 0:GLA_KEY]).astype(BF16)
        k_ref[rows, :] = _dot(xn, w_ref[:, GLA_KEY:2 * GLA_KEY]).astype(BF16)
        v_ref[rows, :] = _dot(xn, w_ref[:, 2 * GLA_KEY:o0]).astype(BF16)
        og_ref[rows, :] = _dot(xn, w_ref[:, o0:o0 + GLA_VAL]).astype(BF16)
        glr = _dot(xn, w_ref[:, o0 + GLA_VAL:o0 + GLA_VAL + LANES])
        glr_hi, glr_lo = _split_bf16(glr)
        w2_hi = w2_ref[0]
        w2_lo = w2_ref[1]
        pre = _dot(glr_hi, w2_hi) + _dot(glr_lo, w2_hi) + _dot(glr_hi, w2_lo) + b2_ref[...]
        lg_ref[rows, :] = (jnp.minimum(pre, 0.0) - jnp.log1p(jnp.exp(-jnp.abs(pre)))) * (1.0 / GLA_TAU)


def _gla_proj(x, gain, w, w2, b2, tm):
    rows = x.shape[0]
    n = w.shape[1]
    row_spec = lambda c: pl.BlockSpec((tm, c), lambda i: (i, 0))
    return pl.pallas_call(
        _gla_proj_kernel,
        grid=(rows // tm,),
        in_specs=[
            row_spec(D_MODEL),
            pl.BlockSpec((1, D_MODEL), lambda i: (0, 0)),
            _resident((D_MODEL, n)),
            pl.BlockSpec((2, LANES, GLA_KEY), lambda i: (0, 0, 0)),
            pl.BlockSpec((1, GLA_KEY), lambda i: (0, 0)),
        ],
        out_specs=[row_spec(GLA_KEY), row_spec(GLA_KEY), row_spec(GLA_VAL), row_spec(GLA_VAL),
                   row_spec(GLA_KEY)],
        out_shape=[jax.ShapeDtypeStruct((rows, GLA_KEY), BF16),
                   jax.ShapeDtypeStruct((rows, GLA_KEY), BF16),
                   jax.ShapeDtypeStruct((rows, GLA_VAL), BF16),
                   jax.ShapeDtypeStruct((rows, GLA_VAL), BF16),
                   jax.ShapeDtypeStruct((rows, GLA_KEY), F32)],
        compiler_params=_cparams(("parallel",)),
        name="gla_proj",
    )(x, gain, w, w2, b2)


def _affine_scan8(a, b, row):
    for s in (1, 2, 4):
        m = row >= s
        a_sh = pltpu.roll(a, s, 0)
        b_sh = pltpu.roll(b, s, 0)
        b = jnp.where(m, a * b_sh + b, b)
        a = jnp.where(m, a * a_sh, a)
    return a, b


def _lru_tile(gate_ref, rec_ref, cw_ref, cb_ref, wa_ref, ba_ref, wx_ref, bx_ref, sp_ref,
              conv0_ref, h0_ref, out_ref, conv_t_ref, h_t_ref, xbuf, xcbuf, abuf, bbuf, hprev, hcar,
              *, tl, permuted, restart, between=()):
    between = list(between)
    pause = lambda: between.pop(0)() if between else None
    if restart is not None:
        @pl.when(restart)
        def _():
            xbuf[0:SUBLANES, :] = conv0_ref[...]
            hcar[...] = h0_ref[...]

    row = lax.broadcasted_iota(jnp.int32, (SUBLANES, D_MODEL), 0)
    ns = PERM_BLOCK // CHUNK_STEPS
    slab = lambda b, j: slice(b * PERM_BLOCK + j * ns, b * PERM_BLOCK + (j + 1) * ns)

    if permuted:
        srow = lax.broadcasted_iota(jnp.int32, (ns, D_MODEL), 0)
        for b in range(tl // PERM_BLOCK):
            x = [rec_ref[slab(b, j), :].astype(F32) for j in range(CHUNK_STEPS)]
            back = {j: jnp.where(srow == 0, xbuf[j:j + 1, :], pltpu.roll(x[j], 1, 0))
                    for j in range(CHUNK_STEPS - 3, CHUNK_STEPS)}
            for j in range(CHUNK_STEPS):
                xc = cb_ref[...]
                for k in range(4):
                    src = x[j - k] if j >= k else back[j - k + CHUNK_STEPS]
                    xc = xc + cw_ref[3 - k:4 - k, :] * src
                xcbuf[slab(b, j), :] = xc
            last = [x[j][ns - 1:ns, :] for j in range(CHUNK_STEPS - 3, CHUNK_STEPS)]
            xbuf[0:SUBLANES, :] = jnp.where(row == 5, last[0], jnp.where(row == 6, last[1], last[2]))
    else:
        xbuf[SUBLANES:SUBLANES + tl, :] = rec_ref[...].astype(F32)
        xc = cb_ref[...]
        for tap in range(4):
            xc = xc + cw_ref[tap:tap + 1, :] * xbuf[SUBLANES - 3 + tap:SUBLANES - 3 + tap + tl, :]
        xcbuf[...] = xc
        xbuf[0:SUBLANES, :] = xbuf[tl:tl + SUBLANES, :]
    pause()
    for blk in range(LRU_BLOCKS):
        pause()
        sl = slice(blk * LRU_BLOCK, (blk + 1) * LRU_BLOCK)
        xs = xcbuf[:, sl]
        xb = xs.astype(BF16)
        r = _sigmoid(_dot(xb, wa_ref[blk]) + ba_ref[:, sl])
        i = _sigmoid(_dot(xb, wx_ref[blk]) + bx_ref[:, sl])
        a = jnp.exp(-LRU_C * r * sp_ref[:, sl])
        abuf[:, sl] = a
        v = 1.0 - a * a
        bbuf[:, sl] = jnp.where(v > 0.0, v * lax.rsqrt(v), 0.0) * (i * xs)

    last_row = lambda v: jnp.broadcast_to(v[SUBLANES - 1:SUBLANES, :], (SUBLANES, D_MODEL))
    if permuted:
        carry = hcar[...]
        for b in range(tl // PERM_BLOCK):
            for j in range(1, CHUNK_STEPS):
                a_j = abuf[slab(b, j), :]
                bbuf[slab(b, j), :] = a_j * bbuf[slab(b, j - 1), :] + bbuf[slab(b, j), :]
                abuf[slab(b, j), :] = a_j * abuf[slab(b, j - 1), :]
            top = b * PERM_BLOCK + (CHUNK_STEPS - 1) * ns
            for g in range(ns // SUBLANES):
                rg = slice(top + g * SUBLANES, top + (g + 1) * SUBLANES)
                a, bb = _affine_scan8(abuf[rg, :], bbuf[rg, :], row)
                h = a * carry + bb
                hprev[g * SUBLANES:(g + 1) * SUBLANES, :] = jnp.where(row == 0, carry, pltpu.roll(h, 1, 0))
                carry = last_row(h)
            for j in range(CHUNK_STEPS):
                bbuf[slab(b, j), :] = abuf[slab(b, j), :] * hprev[...] + bbuf[slab(b, j), :]
    else:
        def body(gidx, carry):
            r0 = pl.multiple_of(gidx * SUBLANES, SUBLANES)
            a, b = _affine_scan8(abuf[pl.ds(r0, SUBLANES), :], bbuf[pl.ds(r0, SUBLANES), :], row)
            h = a * carry + b
            bbuf[pl.ds(r0, SUBLANES), :] = h
            return last_row(h)

        carry = lax.fori_loop(0, tl // SUBLANES, body, hcar[...])
    hcar[...] = carry
    pause()
    out_ref[...] = (_gelu_tanh(gate_ref[...].astype(F32)) * bbuf[...]).astype(out_ref.dtype)
    while between:
        pause()
    conv_t_ref[...] = xbuf[0:SUBLANES, :]
    h_t_ref[...] = carry


def _in_proj_lru_kernel(x_ref, g_ref, w_ref, perm_ref, cw_ref, cb_ref, wa_ref, ba_ref, wx_ref, bx_ref, sp_ref,
                        conv0_ref, h0_ref, a_out_ref, u_ref, conv_t_ref, h_t_ref,
                        zbuf, xbuf, xcbuf, abuf, bbuf, hprev, hcar, *, tl, permuted):
    blk = zbuf.shape[1]
    nblk = tl // blk
    block = lambda b: slice(b * blk, (b + 1) * blk)

    def projection(b):
        state = {}

        def normalise():
            xn = _rms(x_ref[block(b), :], g_ref[...]).astype(BF16)
            if permuted:
                xn = _dot(perm_ref[...], xn).astype(BF16)
            state["xn"] = xn

        def columns(c0):
            def run():
                if c0 < 2 * D_MODEL:
                    zbuf[b % 2, :, c0:c0 + SSM_WIDTH] = _dot(state["xn"], w_ref[:, c0:c0 + SSM_WIDTH]).astype(BF16)
                else:
                    u_ref[block(b), :] = _dot(state["xn"], w_ref[:, c0:]).astype(BF16)
            return run

        return [normalise] + [columns(c0) for c0 in range(0, w_ref.shape[1], SSM_WIDTH)]

    for piece in projection(0):
        piece()
    for b in range(nblk):
        z = zbuf.at[b % 2]
        _lru_tile(z.at[:, 0:D_MODEL], z.at[:, D_MODEL:2 * D_MODEL], cw_ref, cb_ref, wa_ref, ba_ref,
                  wx_ref, bx_ref, sp_ref, conv0_ref, h0_ref, a_out_ref.at[block(b), :], conv_t_ref, h_t_ref,
                  xbuf, xcbuf, abuf, bbuf, hprev, hcar, tl=blk, permuted=permuted,
                  restart=(pl.program_id(1) == 0) if b == 0 else None,
                  between=projection(b + 1) if b + 1 < nblk else ())


def _block_perm():
    r_out = jnp.arange(PERM_BLOCK)
    chunks = PERM_BLOCK // CHUNK_STEPS
    src = CHUNK_STEPS * (r_out % chunks) + r_out // chunks
    return (jnp.arange(PERM_BLOCK)[None, :] == src[:, None]).astype(BF16)


def _in_proj_lru(x, gain, w, p, conv0, h0, nb, tl, permuted):
    rows = x.shape[0]
    nt = rows // (nb * tl)
    assert not permuted or tl % PERM_BLOCK == 0
    n_u = w.shape[1] - 2 * D_MODEL
    blk = PERM_BLOCK if permuted else tl
    cur = lambda c: pl.BlockSpec((tl, c), lambda b, t: (b * nt + t, 0))
    state = pl.BlockSpec((SUBLANES, D_MODEL), lambda b, t: (0, 0))
    return pl.pallas_call(
        functools.partial(_in_proj_lru_kernel, tl=tl, permuted=permuted),
        grid=(nb, nt),
        in_specs=[cur(D_MODEL), _resident((1, D_MODEL)), _resident(w.shape), _resident((PERM_BLOCK, PERM_BLOCK)),
                  _resident((4, D_MODEL)), _resident((1, D_MODEL)),
                  _resident((LRU_BLOCKS, LRU_BLOCK, LRU_BLOCK)), _resident((1, D_MODEL)),
                  _resident((LRU_BLOCKS, LRU_BLOCK, LRU_BLOCK)), _resident((1, D_MODEL)),
                  _resident((1, D_MODEL)), _resident((SUBLANES, D_MODEL)), _resident((SUBLANES, D_MODEL))],
        out_specs=[cur(D_MODEL), cur(n_u), state, state],
        out_shape=[jax.ShapeDtypeStruct((rows, D_MODEL), BF16),
                   jax.ShapeDtypeStruct((rows, n_u), BF16),
                   jax.ShapeDtypeStruct((SUBLANES, D_MODEL), F32),
                   jax.ShapeDtypeStruct((SUBLANES, D_MODEL), F32)],
        scratch_shapes=[pltpu.VMEM((2, blk, 2 * D_MODEL), BF16),
                        pltpu.VMEM(((0 if permuted else blk) + SUBLANES, D_MODEL), F32),
                        pltpu.VMEM((blk, D_MODEL), F32),
                        pltpu.VMEM((blk, D_MODEL), F32),
                        pltpu.VMEM((blk, D_MODEL), F32),
                        pltpu.VMEM((PERM_BLOCK // CHUNK_STEPS, D_MODEL), F32),
                        pltpu.VMEM((SUBLANES, D_MODEL), F32)],
        compiler_params=_cparams(("arbitrary", "arbitrary")),
        name="in_proj_lru",
    )(x, gain, w, _block_perm(), p["conv_w"], p["conv_b"], p["gate_a_w"], p["gate_a_b"], p["gate_x_w"],
      p["gate_x_b"], p["softplus"], conv0, h0)


SSM_CHUNK = 256


def _complex_scan(xs, pw_ref, hcar, n_rows, lane_pairs, prev=None):
    row = lax.broadcasted_iota(jnp.int32, (SUBLANES, SSM_CHUNK), 0)
    for re0, im0 in lane_pairs:
        re = slice(re0, re0 + SSM_CHUNK)
        im = slice(im0, im0 + SSM_CHUNK)
        tabs = [(pw_ref[k * SUBLANES:(k + 1) * SUBLANES, re], pw_ref[k * SUBLANES:(k + 1) * SUBLANES, im])
                for k in range(4)]

        def body(gidx, carry, re=re, im=im, tabs=tabs):
            cr, ci = carry
            r0 = pl.multiple_of(gidx * SUBLANES, SUBLANES)
            hr = xs[pl.ds(r0, SUBLANES), re]
            hi = xs[pl.ds(r0, SUBLANES), im]
            for k, s in enumerate((1, 2, 4)):
                pr, pi = tabs[k]
                sr = pltpu.roll(hr, s, 0)
                si = pltpu.roll(hi, s, 0)
                hr, hi = hr + (pr * sr - pi * si), hi + (pr * si + pi * sr)
            pr, pi = tabs[3]
            hr, hi = hr + (pr * cr - pi * ci), hi + (pr * ci + pi * cr)
            xs[pl.ds(r0, SUBLANES), re] = hr
            xs[pl.ds(r0, SUBLANES), im] = hi
            if prev is not None:
                prev[pl.ds(r0, SUBLANES), re] = jnp.where(row == 0, cr, pltpu.roll(hr, 1, 0))
                prev[pl.ds(r0, SUBLANES), im] = jnp.where(row == 0, ci, pltpu.roll(hi, 1, 0))
            last = slice(SUBLANES - 1, SUBLANES)
            return (jnp.broadcast_to(hr[last, :], (SUBLANES, SSM_CHUNK)),
                    jnp.broadcast_to(hi[last, :], (SUBLANES, SSM_CHUNK)))

        cr, ci = lax.fori_loop(0, n_rows // SUBLANES, body, (hcar[:, re], hcar[:, im]))
        hcar[:, re] = cr
        hcar[:, im] = ci


def _ssm_kernel(u_ref, bblk_ref, cblk_ref, d_ref, gw_ref, gb_ref, pw_ref, h0_ref,
                out_ref, h_t_ref, xs, hcar, *, tl):
    @pl.when(pl.program_id(1) == 0)
    def _():
        hcar[...] = h0_ref[...]

    u = u_ref[...]
    xs[...] = _dot(u, bblk_ref[...])
    pairs = [(c * SSM_CHUNK, SSM_LANES + c * SSM_CHUNK) for c in range(SSM_LANES // SSM_CHUNK)]
    _complex_scan(xs, pw_ref, hcar, tl, pairs)
    y = _dot(xs[...].astype(BF16), cblk_ref[...]) + d_ref[...] * u.astype(F32)
    g = _gelu_tanh(y)
    out_ref[...] = (g * jax.nn.sigmoid(_dot(g.astype(BF16), gw_ref[...]) + gb_ref[...])).astype(out_ref.dtype)
    h_t_ref[...] = hcar[...]


def _ssm(u, p, h0, nb, tl):
    rows = u.shape[0]
    nt = rows // (nb * tl)
    tile = pl.BlockSpec((tl, SSM_WIDTH), lambda b, t: (b * nt + t, 0))
    full2 = lambda shp: pl.BlockSpec(shp, lambda b, t: (0, 0))
    return pl.pallas_call(
        functools.partial(_ssm_kernel, tl=tl),
        grid=(nb, nt),
        in_specs=[tile,
                  _resident((SSM_WIDTH, 2 * SSM_LANES)), _resident((2 * SSM_LANES, SSM_WIDTH)),
                  full2((1, SSM_WIDTH)), full2((SSM_WIDTH, SSM_WIDTH)), full2((1, SSM_WIDTH)),
                  full2((4 * SUBLANES, 2 * SSM_LANES)), full2((SUBLANES, 2 * SSM_LANES))],
        out_specs=[tile, full2((SUBLANES, 2 * SSM_LANES))],
        out_shape=[jax.ShapeDtypeStruct((rows, SSM_WIDTH), BF16),
                   jax.ShapeDtypeStruct((SUBLANES, 2 * SSM_LANES), F32)],
        scratch_shapes=[pltpu.VMEM((tl, 2 * SSM_LANES), F32),
                        pltpu.VMEM((SUBLANES, 2 * SSM_LANES), F32)],
        compiler_params=_cparams(("arbitrary", "arbitrary")),
        name="s5_ssm",
    )(u, p["bblk"], p["cblk"], p["d"], p["glu_w"], p["glu_b"], p["powers"], h0)


SSM_Q = CHUNK_STEPS
SSM_PB = PERM_BLOCK
SSM_TILES = SSM_WIDTH // LANES
SSM_VW = SSM_Q * LANES


def _ssm_chunk_kernel(u_ref, w2_ref, tz_ref, w1_ref, pw_ref, d_ref, gw_ref, gb_ref,
                      h0_ref, out_ref, h_t_ref, a_scr, xs, hp, ypm, hcar, *, tl):
    nblk = tl // SSM_PB
    ncb = SSM_PB // SSM_Q
    nc = tl // SSM_Q

    @pl.when(pl.program_id(1) == 0)
    def _():
        hcar[...] = h0_ref[...]

    for b in range(nblk):
        for j in range(SSM_Q):
            a_scr[j, b * ncb:(b + 1) * ncb, :] = u_ref[b * SSM_PB + j * ncb:b * SSM_PB + (j + 1) * ncb, :]

    def u_tile(v):
        return jnp.concatenate([a_scr[j, :, v * LANES:(v + 1) * LANES] for j in range(SSM_Q)], axis=1)

    for v in range(SSM_TILES):
        xs[:, v * SSM_VW:(v + 1) * SSM_VW] = _dot(u_tile(v), w2_ref[v])

    half = SSM_VW // 2
    pairs = [(v * SSM_VW + c * SSM_CHUNK, v * SSM_VW + half + c * SSM_CHUNK)
             for v in range(SSM_TILES) for c in range(half // SSM_CHUNK)]
    _complex_scan(xs, pw_ref, hcar, nc, pairs, prev=hp)

    for v in range(SSM_TILES):
        yv = _dot(u_tile(v), tz_ref[v]) + _dot(hp[:, v * SSM_VW:(v + 1) * SSM_VW].astype(BF16), w1_ref[v])
        for b in range(nblk):
            for j in range(SSM_Q):
                ypm[b * SSM_PB + j * ncb:b * SSM_PB + (j + 1) * ncb, v * LANES:(v + 1) * LANES] = (
                    yv[b * ncb:(b + 1) * ncb, j * LANES:(j + 1) * LANES])

    for b in range(nblk):
        rows = slice(b * SSM_PB, (b + 1) * SSM_PB)
        y = ypm[rows, :] + d_ref[...] * u_ref[rows, :].astype(F32)
        g = _gelu_tanh(y)
        out_ref[rows, :] = (g * jax.nn.sigmoid(_dot(g.astype(BF16), gw_ref[...]) + gb_ref[...])).astype(out_ref.dtype)
    h_t_ref[...] = hcar[...]


def _ssm_chunked(u, p, h0, nb, tl):
    rows = u.shape[0]
    nt = rows // (nb * tl)
    assert tl % SSM_PB == 0
    tile = pl.BlockSpec((tl, SSM_WIDTH), lambda b, t: (b * nt + t, 0))
    wspec = _resident((SSM_TILES, SSM_VW, SSM_VW))
    state = (SUBLANES, 2 * SSM_LANES)
    return pl.pallas_call(
        functools.partial(_ssm_chunk_kernel, tl=tl),
        grid=(nb, nt),
        in_specs=[tile, wspec, wspec, wspec,
                  _resident((4 * SUBLANES, 2 * SSM_LANES)), _resident((1, SSM_WIDTH)),
                  _resident((SSM_WIDTH, SSM_WIDTH)), _resident((1, SSM_WIDTH)), _resident(state)],
        out_specs=[tile, pl.BlockSpec(state, lambda b, t: (0, 0))],
        out_shape=[jax.ShapeDtypeStruct((rows, SSM_WIDTH), BF16), jax.ShapeDtypeStruct(state, F32)],
        scratch_shapes=[pltpu.VMEM((SSM_Q, tl // SSM_Q, SSM_WIDTH), BF16),
                        pltpu.VMEM((tl // SSM_Q, 2 * SSM_LANES), F32),
                        pltpu.VMEM((tl // SSM_Q, 2 * SSM_LANES), F32),
                        pltpu.VMEM((tl, SSM_WIDTH), F32),
                        pltpu.VMEM(state, F32)],
        compiler_params=_cparams(("arbitrary", "arbitrary")),
        name="s5_ssm_chunked",
    )(u, p["w2"], p["tz"], p["w1"], p["chunk_powers"], p["d"], p["glu_w"], p["glu_b"], h0)


FFN_CHUNK = 512


def _resident(shape):
    return pl.BlockSpec(shape, lambda *_: (0,) * len(shape), pipeline_mode=pl.Buffered(1))


def _mix_ffn_kernel(a_ref, b_ref, permt_ref, wa_ref, wb_ref, h_ref, g_ref, wg_ref, wu_ref, wd_ref, o_ref,
                    *, permuted):
    if permuted:
        pt = permt_ref[...]
        blocks = [slice(r, r + PERM_BLOCK) for r in range(0, h_ref.shape[0], PERM_BLOCK)]
        a = jnp.concatenate([_dot(pt, a_ref[rows, :]).astype(BF16) for rows in blocks], axis=0)
        b = jnp.concatenate([_dot(pt, b_ref[rows, :]).astype(BF16) for rows in blocks], axis=0)
    else:
        a, b = a_ref[...], b_ref[...]
    h = h_ref[...] + _dot(a, wa_ref[...]) + _dot(b, wb_ref[...])
    o_ref[...] = h
    xn = _rms(h, g_ref[...]).astype(BF16)
    for c in range(wg_ref.shape[1] // FFN_CHUNK):
        sl = slice(c * FFN_CHUNK, (c + 1) * FFN_CHUNK)
        a = (_silu(_dot(xn, wg_ref[:, sl])) * _dot(xn, wu_ref[:, sl])).astype(BF16)
        o_ref[...] += _dot(a, wd_ref[sl, :])


def _mix_ffn(a, b, wa, wb, h, gain, wg, wu, wd, tm, permuted):
    rows = h.shape[0]
    assert not permuted or tm % PERM_BLOCK == 0
    row_spec = lambda c: pl.BlockSpec((tm, c), lambda i: (i, 0))
    return pl.pallas_call(
        functools.partial(_mix_ffn_kernel, permuted=permuted),
        grid=(rows // tm,),
        in_specs=[row_spec(a.shape[1]), row_spec(b.shape[1]), _resident((PERM_BLOCK, PERM_BLOCK)),
                  _resident(wa.shape), _resident(wb.shape),
                  row_spec(D_MODEL), _resident((1, D_MODEL)),
                  _resident(wg.shape), _resident(wu.shape), _resident(wd.shape)],
        out_specs=row_spec(D_MODEL),
        out_shape=jax.ShapeDtypeStruct((rows, D_MODEL), F32),
        compiler_params=_cparams(("parallel",)),
        name="mix_ffn",
    )(a, b, _block_perm().T, wa, wb, h, gain, wg, wu, wd)


def _gla_kernel(q_ref, k_ref, v_ref, og_ref, lg_ref, hn_ref, s0_ref, o_ref, s_t_ref, st, *, tl, ck):
    @pl.when(pl.program_id(1) == 0)
    def _():
        st[...] = s0_ref[...]

    ri = lax.broadcasted_iota(jnp.int32, (ck, ck), 0)
    ci = lax.broadcasted_iota(jnp.int32, (ck, ck), 1)
    causal = ri >= ci
    tri = causal.astype(F32).astype(BF16)
    scale = GLA_DK ** -0.5

    for c in range(tl // ck):
        rows = slice(c * ck, (c + 1) * ck)
        lg_hi, lg_lo = _split_bf16(lg_ref[rows, :])
        b_all = _dot(tri, lg_hi) + _dot(tri, lg_lo)
        for h in range(GLA_HEADS):
            ks = slice(h * GLA_DK, (h + 1) * GLA_DK)
            vs = slice(h * GLA_DV, (h + 1) * GLA_DV)
            b = b_all[:, ks]
            b_last = b[ck - 1:ck, :]
            q = q_ref[rows, ks].astype(F32)
            k = k_ref[rows, ks].astype(F32)
            v = v_ref[rows, vs]
            qd = (q * (scale * jnp.exp(b))).astype(BF16)
            kd = (k * jnp.exp(-b)).astype(BF16)
            kl = (k * jnp.exp(b_last - b)).astype(BF16)
            att = lax.dot_general(qd, kd, (((1,), (1,)), ((), ())), preferred_element_type=F32)
            att = jnp.where(causal, att, 0.0).astype(BF16)
            s_h = st[h]
            o = _dot(att, v) + lax.dot_general(qd, s_h.astype(BF16), (((1,), (1,)), ((), ())),
                                               preferred_element_type=F32)
            st[h] = s_h * jnp.exp(b_last) + lax.dot_general(v, kl, (((0,), (0,)), ((), ())),
                                                            preferred_element_type=F32)
            o = o * lax.rsqrt(jnp.mean(o * o, axis=-1, keepdims=True) + EPS) * hn_ref[:, vs]
            o_ref[rows, vs] = (o * _silu(og_ref[rows, vs].astype(F32))).astype(o_ref.dtype)
    s_t_ref[...] = st[...]


def _gla(q, k, v, og, lg, head_norm, s0, nb, tl, ck):
    rows = q.shape[0]
    nt = rows // (nb * tl)
    tile = lambda c: pl.BlockSpec((tl, c), lambda b, t: (b * nt + t, 0))
    state = pl.BlockSpec((GLA_HEADS, GLA_DV, GLA_DK), lambda b, t: (0, 0, 0))
    return pl.pallas_call(
        functools.partial(_gla_kernel, tl=tl, ck=ck),
        grid=(nb, nt),
        in_specs=[tile(GLA_KEY), tile(GLA_KEY), tile(GLA_VAL), tile(GLA_VAL), tile(GLA_KEY),
                  pl.BlockSpec((1, GLA_VAL), lambda b, t: (0, 0)), state],
        out_specs=[tile(GLA_VAL), state],
        out_shape=[jax.ShapeDtypeStruct((rows, GLA_VAL), BF16),
                   jax.ShapeDtypeStruct((GLA_HEADS, GLA_DV, GLA_DK), F32)],
        scratch_shapes=[pltpu.VMEM((GLA_HEADS, GLA_DV, GLA_DK), F32)],
        compiler_params=_cparams(("arbitrary", "arbitrary")),
        name="gla",
    )(q, k, v, og, lg, head_norm, s0)


TOK_TILE = 512
SEG_ALIGN = 16
LOCAL_ROWS = 1152
FFN_TILE = 512
MOE_CHUNK = 1792
SEG_BITS = (512, 256, 128, 64, 32, 16)
INFO_E1, INFO_E2, INFO_R1, INFO_R2, INFO_W1, INFO_W2 = range(6)
assert LOCAL_ROWS >= 2 * TOK_TILE + N_EXPERTS * (SEG_ALIGN - 1) and LOCAL_ROWS % LANES == 0


def _router_kernel(o_ref, wo_ref, h_ref, g_ref, wr_ref, tri_ref, hout_ref, xn_ref, info_ref, cnt_ref):
    tm = h_ref.shape[0]
    sub = tri_ref.shape[0]
    w_both = wr_ref[...]
    w_hi = wr_ref[:, 0:LANES]
    lane = lax.broadcasted_iota(jnp.int32, (sub, LANES), 1).astype(F32)
    neg = jnp.float32(-jnp.inf)
    count = jnp.zeros((1, LANES), F32)
    for r in range(tm // sub):
        rows = slice(r * sub, (r + 1) * sub)
        h = h_ref[rows, :] + _dot(o_ref[rows, :], wo_ref[...])
        hout_ref[rows, :] = h
        xn = _rms(h, g_ref[...])
        xn_ref[rows, :] = xn.astype(BF16)
        x_hi, x_lo = _split_bf16(xn)
        both = _dot(x_hi, w_both)
        logits = both[:, 0:LANES] + both[:, LANES:2 * LANES] + _dot(x_lo, w_hi)
        logits = jnp.where(lane < N_EXPERTS, logits, neg)
        v1 = jnp.max(logits, axis=-1, keepdims=True)
        i1 = jnp.min(jnp.where(logits == v1, lane, float(LANES)), axis=-1, keepdims=True)
        rest = jnp.where(lane == i1, neg, logits)
        v2 = jnp.max(rest, axis=-1, keepdims=True)
        i2 = jnp.min(jnp.where(rest == v2, lane, float(LANES)), axis=-1, keepdims=True)
        e2 = jnp.exp(v2 - v1)
        w1 = 1.0 / (1.0 + e2)
        w2 = e2 / (1.0 + e2)
        hit1 = lane == i1
        hit2 = lane == i2
        sel = jnp.where(hit1, 1.0, 0.0) + jnp.where(hit2, 1.0, 0.0)
        cum = _dot(tri_ref[...], sel.astype(BF16)) + count
        count = cum[sub - 1:sub, :]
        rank1 = jnp.sum(jnp.where(hit1, cum, 0.0), axis=-1, keepdims=True) - 1.0
        rank2 = jnp.sum(jnp.where(hit2, cum, 0.0), axis=-1, keepdims=True) - 1.0
        info = jnp.zeros_like(logits)
        for col, val in ((INFO_E1, i1), (INFO_E2, i2), (INFO_R1, rank1), (INFO_R2, rank2),
                         (INFO_W1, w1), (INFO_W2, w2)):
            info = jnp.where(lane == col, val, info)
        info_ref[rows, :] = info
    cnt_ref[...] = jnp.broadcast_to(count, (SUBLANES, LANES))


def _router(o, wo, h, gain, wr, tri):
    rows = h.shape[0]
    tm = TOK_TILE
    nt = rows // tm
    row_spec = lambda c: pl.BlockSpec((tm, c), lambda i: (i, 0))
    return pl.pallas_call(
        _router_kernel,
        grid=(nt,),
        in_specs=[row_spec(o.shape[1]), _resident(wo.shape), row_spec(D_MODEL), _resident((1, D_MODEL)),
                  _resident((D_MODEL, 2 * LANES)), _resident(tri.shape)],
        out_specs=[row_spec(D_MODEL), row_spec(D_MODEL), row_spec(LANES),
                   pl.BlockSpec((SUBLANES, LANES), lambda i: (i, 0))],
        out_shape=[jax.ShapeDtypeStruct((rows, D_MODEL), F32),
                   jax.ShapeDtypeStruct((rows, D_MODEL), BF16),
                   jax.ShapeDtypeStruct((rows, LANES), F32),
                   jax.ShapeDtypeStruct((nt * SUBLANES, LANES), F32)],
        compiler_params=_cparams(("parallel",)),
        name="router",
    )(o, wo, h, gain, wr, tri)


def _segment_copies(tile, loff_ref, slen_ref, goff_ref, make_copy, fn):
    for e in range(N_EXPERTS):
        idx = tile * N_EXPERTS + e
        lo = loff_ref[idx]
        n = slen_ref[idx]
        go = goff_ref[idx]
        for bit in SEG_BITS:
            pos = n & (-2 * bit)

            @pl.when((n & bit) != 0)
            def _(lo=lo, go=go, pos=pos, bit=bit):
                fn(make_copy(pl.multiple_of(lo + pos, SEG_ALIGN), pl.multiple_of(go + pos, SEG_ALIGN), bit))


def _local_rows(info_row, loff_ref, tile, which_e, which_r):
    e = info_row(which_e)
    loc = info_row(which_r)
    for ex in range(N_EXPERTS):
        loc = loc + jnp.where(e == float(ex), loff_ref[tile * N_EXPERTS + ex].astype(F32), 0.0)
    return loc


def _dispatch_kernel(loff_ref, slen_ref, goff_ref, xn_ref, info_ref, zero_ref, xs_ref, buf, sem):
    del zero_ref
    i = pl.program_id(0)
    n = pl.num_programs(0)
    slot = i % 2

    def copy_for(s):
        def make(lo, go, rows):
            return pltpu.make_async_copy(buf.at[s, pl.ds(lo, rows), :], xs_ref.at[pl.ds(go, rows), :],
                                         sem.at[s])
        return make

    wait = lambda c: c.wait()
    start = lambda c: c.start()

    @pl.when(i >= 2)
    def _():
        _segment_copies(i - 2, loff_ref, slen_ref, goff_ref, copy_for(slot), wait)

    info_t = info_ref[...].T
    row_of = lambda r: info_t[r:r + 1, :]
    loc1 = _local_rows(row_of, loff_ref, i, INFO_E1, INFO_R1)
    loc2 = _local_rows(row_of, loff_ref, i, INFO_E2, INFO_R2)
    ridx = lax.broadcasted_iota(jnp.int32, (LOCAL_ROWS, TOK_TILE), 0).astype(F32)
    select = (jnp.where(ridx == loc1, 1.0, 0.0) + jnp.where(ridx == loc2, 1.0, 0.0)).astype(BF16)
    buf[slot] = _dot(select, xn_ref[...]).astype(BF16)
    _segment_copies(i, loff_ref, slen_ref, goff_ref, copy_for(slot), start)

    @pl.when(i == n - 1)
    def _():
        _segment_copies(i, loff_ref, slen_ref, goff_ref, copy_for(slot), wait)

        @pl.when(i >= 1)
        def _():
            _segment_copies(i - 1, loff_ref, slen_ref, goff_ref, copy_for(1 - slot), wait)


def _dispatch(loff, slen, goff, xn, info, n_sort):
    rows = xn.shape[0]
    nt = rows // TOK_TILE
    zeros = jnp.zeros((n_sort, D_MODEL), BF16)
    return pl.pallas_call(
        _dispatch_kernel,
        grid_spec=pltpu.PrefetchScalarGridSpec(
            num_scalar_prefetch=3,
            grid=(nt,),
            in_specs=[pl.BlockSpec((TOK_TILE, D_MODEL), lambda i, *_: (i, 0)),
                      pl.BlockSpec((TOK_TILE, LANES), lambda i, *_: (i, 0)),
                      pl.BlockSpec(memory_space=pl.ANY)],
            out_specs=pl.BlockSpec(memory_space=pl.ANY),
            scratch_shapes=[pltpu.VMEM((2, LOCAL_ROWS, D_MODEL), BF16),
                            pltpu.SemaphoreType.DMA((2,))],
        ),
        out_shape=jax.ShapeDtypeStruct((n_sort, D_MODEL), BF16),
        input_output_aliases={5: 0},
        compiler_params=_cparams(("arbitrary",)),
        name="moe_dispatch",
    )(loff, slen, goff, xn, info, zeros)


def _moe_ffn_kernel(te_ref, tot_ref, xs_ref, wg_ref, wu_ref, wd_ref, y_ref, acc_ref):
    del te_ref
    j = pl.program_id(0)
    f = pl.program_id(1)

    @pl.when(j < tot_ref[0])
    def _():
        @pl.when(f == 0)
        def _():
            acc_ref[...] = jnp.zeros_like(acc_ref)

        x = xs_ref[...]
        a = _silu(_dot(x, wg_ref[0])) * _dot(x, wu_ref[0])
        acc_ref[...] += _dot(a.astype(BF16), wd_ref[0])

        @pl.when(f == pl.num_programs(1) - 1)
        def _():
            y_ref[...] = acc_ref[...].astype(y_ref.dtype)

    @pl.when((j >= tot_ref[0]) & (f == pl.num_programs(1) - 1))
    def _():
        y_ref[...] = jnp.zeros_like(y_ref)


def _moe_ffn(tile_e, total, xs, wg, wu, wd, fc):
    n_sort = xs.shape[0]
    nt = n_sort // FFN_TILE
    nf = wg.shape[2] // fc
    tile_of = lambda j, tot: jnp.minimum(j, tot[0] - 1)
    chunk_of = lambda j, f, tot: jnp.where(j < tot[0], f, nf - 1)
    return pl.pallas_call(
        _moe_ffn_kernel,
        grid_spec=pltpu.PrefetchScalarGridSpec(
            num_scalar_prefetch=2,
            grid=(nt, nf),
            in_specs=[pl.BlockSpec((FFN_TILE, D_MODEL), lambda j, f, te, tot: (tile_of(j, tot), 0)),
                      pl.BlockSpec((1, D_MODEL, fc), lambda j, f, te, tot: (te[j], 0, chunk_of(j, f, tot))),
                      pl.BlockSpec((1, D_MODEL, fc), lambda j, f, te, tot: (te[j], 0, chunk_of(j, f, tot))),
                      pl.BlockSpec((1, fc, D_MODEL), lambda j, f, te, tot: (te[j], chunk_of(j, f, tot), 0))],
            out_specs=pl.BlockSpec((FFN_TILE, D_MODEL), lambda j, f, te, tot: (j, 0)),
            scratch_shapes=[pltpu.VMEM((FFN_TILE, D_MODEL), F32)],
        ),
        out_shape=jax.ShapeDtypeStruct((n_sort, D_MODEL), BF16),
        compiler_params=_cparams(("arbitrary", "arbitrary")),
        name="moe_ffn",
    )(tile_e, total, xs, wg, wu, wd)


def _combine_kernel(loff_ref, slen_ref, goff_ref, info_ref, h_ref, fn_ref, y_ref, o_ref, buf, sem):
    i = pl.program_id(0)
    n = pl.num_programs(0)
    slot = i % 2

    def copy_for(s):
        def make(lo, go, rows):
            return pltpu.make_async_copy(y_ref.at[pl.ds(go, rows), :], buf.at[s, pl.ds(lo, rows), :],
                                         sem.at[s])
        return make

    wait = lambda c: c.wait()
    start = lambda c: c.start()

    @pl.when(i == 0)
    def _():
        buf[...] = jnp.zeros_like(buf)
        _segment_copies(i, loff_ref, slen_ref, goff_ref, copy_for(slot), start)

    @pl.when(i + 1 < n)
    def _():
        _segment_copies(i + 1, loff_ref, slen_ref, goff_ref, copy_for(1 - slot), start)

    _segment_copies(i, loff_ref, slen_ref, goff_ref, copy_for(slot), wait)

    ys = buf[slot]
    cidx = lax.broadcasted_iota(jnp.int32, (PROJ_SUB, LOCAL_ROWS), 1).astype(F32)
    for r in range(TOK_TILE // PROJ_SUB):
        rows = slice(r * PROJ_SUB, (r + 1) * PROJ_SUB)
        info = info_ref[rows, :]
        col_of = lambda c, info=info: info[:, c:c + 1]
        loc1 = _local_rows(col_of, loff_ref, i, INFO_E1, INFO_R1)
        loc2 = _local_rows(col_of, loff_ref, i, INFO_E2, INFO_R2)
        pt = (jnp.where(cidx == loc1, col_of(INFO_W1), 0.0)
              + jnp.where(cidx == loc2, col_of(INFO_W2), 0.0)).astype(BF16)
        o_ref[rows, :] = _rms(h_ref[rows, :] + _dot(pt, ys), fn_ref[...])


def _combine(loff, slen, goff, info, h, final_norm, y):
    rows = h.shape[0]
    nt = rows // TOK_TILE
    return pl.pallas_call(
        _combine_kernel,
        grid_spec=pltpu.PrefetchScalarGridSpec(
            num_scalar_prefetch=3,
            grid=(nt,),
            in_specs=[pl.BlockSpec((TOK_TILE, LANES), lambda i, *_: (i, 0)),
                      pl.BlockSpec((TOK_TILE, D_MODEL), lambda i, *_: (i, 0)),
                      pl.BlockSpec((1, D_MODEL), lambda i, *_: (0, 0)),
                      pl.BlockSpec(memory_space=pl.ANY)],
            out_specs=pl.BlockSpec((TOK_TILE, D_MODEL), lambda i, *_: (i, 0)),
            scratch_shapes=[pltpu.VMEM((2, LOCAL_ROWS, D_MODEL), BF16),
                            pltpu.SemaphoreType.DMA((2,))],
        ),
        out_shape=jax.ShapeDtypeStruct((rows, D_MODEL), F32),
        compiler_params=_cparams(("arbitrary",)),
        name="moe_combine",
    )(loff, slen, goff, info, h, final_norm, y)


def _routing_tables(cnt, n_tiles_max):
    slen = (cnt + SEG_ALIGN - 1) // SEG_ALIGN * SEG_ALIGN
    loff = jnp.cumsum(slen, axis=1) - slen
    rows_e = jnp.sum(slen, axis=0)
    rows_pad = (rows_e + FFN_TILE - 1) // FFN_TILE * FFN_TILE
    base = jnp.cumsum(rows_pad) - rows_pad
    goff = base[None, :] + jnp.cumsum(slen, axis=0) - slen
    tile_end = jnp.cumsum(rows_pad // FFN_TILE)
    total = tile_end[-1]
    j = jnp.arange(n_tiles_max, dtype=jnp.int32)
    tile_e = jnp.sum(jnp.minimum(j, total - 1)[:, None] >= tile_end[None, :], axis=1).astype(jnp.int32)
    flat = lambda a: a.reshape(-1).astype(jnp.int32)
    return flat(loff), flat(slen), flat(goff), tile_e, total.reshape(1).astype(jnp.int32)


def _moe(o, wo, h, gain, wr, wg, wu, wd, final_norm):
    rows = h.shape[0]
    assert rows % TOK_TILE == 0
    nt = rows // TOK_TILE
    max_rows = 2 * rows + nt * N_EXPERTS * (SEG_ALIGN - 1) + N_EXPERTS * (FFN_TILE - SEG_ALIGN)
    n_tiles_max = -(-max_rows // FFN_TILE)
    tri = jnp.tril(jnp.ones((PROJ_SUB, PROJ_SUB), BF16))
    h, xn, info, cnt = _router(o, wo, h, gain, wr, tri)
    cnt = cnt.reshape(nt, SUBLANES, LANES)[:, 0, :N_EXPERTS].astype(jnp.int32)
    loff, slen, goff, tile_e, total = _routing_tables(cnt, n_tiles_max)
    xs = _dispatch(loff, slen, goff, xn, info, n_tiles_max * FFN_TILE)
    y = _moe_ffn(tile_e, total, xs, wg, wu, wd, MOE_CHUNK)
    return _combine(loff, slen, goff, info, h, final_norm, y)


def _ssm_params(lam_re, lam_im, log_dt, b_re, b_im, c_re, c_im, d, glu_w, glu_b):
    g, n, hh = SSM_GROUPS, SSM_STATE, SSM_GROUP
    lr, li = lam_re.astype(F32), lam_im.astype(F32)
    dt = jnp.exp(log_dt.astype(F32))[:, None]
    mag = jnp.exp(dt * lr)
    ar, ai = mag * jnp.cos(dt * li), mag * jnp.sin(dt * li)
    den = lr * lr + li * li
    nr = ar - 1.0
    zr = (nr * lr + ai * li) / den
    zi = (ai * lr - nr * li) / den
    br = zr[..., None] * b_re - zi[..., None] * b_im
    bi = zr[..., None] * b_im + zi[..., None] * b_re
    eye = jnp.eye(g, dtype=F32)
    bblk = jnp.concatenate(
        [jnp.einsum("gnh,gk->ghkn", x, eye).reshape(SSM_WIDTH, SSM_LANES) for x in (br, bi)], axis=1)
    cblk = jnp.concatenate(
        [jnp.einsum("ghn,gk->gnkh", x, eye).reshape(SSM_LANES, SSM_WIDTH) for x in (c_re, -c_im)], axis=0)

    sub = jnp.arange(SUBLANES, dtype=jnp.int32)[:, None]

    def scan_tables(base_r, base_i, layout):
        def power(m):
            pr, pi = jnp.ones((SUBLANES, g, n), F32), jnp.zeros((SUBLANES, g, n), F32)
            for step in range(1, SUBLANES + 1):
                nr_, ni_ = pr * base_r - pi * base_i, pr * base_i + pi * base_r
                take = (m >= step)[:, :, None]
                pr, pi = jnp.where(take, nr_, pr), jnp.where(take, ni_, pi)
            return pr, pi

        tabs = []
        for s in (1, 2, 4):
            pr, pi = power(jnp.full((SUBLANES, 1), s, jnp.int32))
            keep = (sub >= s).astype(F32)[:, :, None]
            tabs.append(layout(pr * keep, pi * keep))
        tabs.append(layout(*power(sub + 1)))
        return jnp.concatenate(tabs, axis=0)

    flat = lambda pr, pi: jnp.concatenate([pr.reshape(SUBLANES, SSM_LANES), pi.reshape(SUBLANES, SSM_LANES)], axis=1)

    q, nv, gl = SSM_Q, SSM_TILES, SSM_GROUPS // SSM_TILES
    pw_r, pw_i = [jnp.ones_like(ar)], [jnp.zeros_like(ar)]
    for _ in range(q):
        pw_r, pw_i = pw_r + [pw_r[-1] * ar - pw_i[-1] * ai], pw_i + [pw_r[-1] * ai + pw_i[-1] * ar]
    pw_r, pw_i = jnp.stack(pw_r), jnp.stack(pw_i)
    cr, ci = c_re.astype(F32), c_im.astype(F32)

    def expand(compact, row_w, col_w):
        s_idx = jnp.arange(LANES)[:, None]
        c_idx = jnp.arange(SSM_VW)[None, :]
        onehot = (s_idx == (c_idx // (gl * col_w)) * col_w + c_idx % col_w).astype(BF16)
        full = jnp.einsum("vrs,sc->vrc", compact.astype(BF16), onehot, preferred_element_type=F32)
        r_idx = jnp.arange(SSM_VW)[:, None]
        same_group = (r_idx // row_w) % gl == (c_idx // col_w) % gl
        return jnp.where(same_group[None], full, 0.0).astype(BF16)

    tiled = lambda pr, pi: jnp.stack([pr, pi], axis=1).reshape(SUBLANES, 2, nv, gl, n).transpose(
        0, 2, 1, 3, 4).reshape(SUBLANES, 2 * SSM_LANES)
    crt, cit = cr.transpose(0, 2, 1)[:, :, :, None], ci.transpose(0, 2, 1)[:, :, :, None]
    m_r = crt * br[:, :, None, :] - cit * bi[:, :, None, :]
    m_i = crt * bi[:, :, None, :] + cit * br[:, :, None, :]
    kern = [jnp.sum(pw_r[dd][:, :, None, None] * m_r - pw_i[dd][:, :, None, None] * m_i, axis=1)
            for dd in range(q)]
    none = jnp.zeros_like(kern[0])
    kf = jnp.stack([jnp.stack([kern[o - i] if o >= i else none for o in range(q)]) for i in range(q)])
    kf = kf.transpose(2, 0, 1, 3, 4)
    kf = kf.reshape(nv, gl, q, q, hh, hh).transpose(0, 2, 1, 5, 3, 4)
    tz = expand(kf.reshape(nv, SSM_VW, LANES), hh, hh)
    rev_r = jnp.stack([pw_r[q - 1 - j] for j in range(q)])[:, :, :, None]
    rev_i = jnp.stack([pw_i[q - 1 - j] for j in range(q)])[:, :, :, None]
    wb = jnp.stack([rev_r * br[None] - rev_i * bi[None], rev_r * bi[None] + rev_i * br[None]])
    wb = wb.reshape(2, q, nv, gl, n, hh).transpose(2, 1, 3, 5, 0, 4)
    w2 = expand(wb.reshape(nv, SSM_VW, LANES), hh, n)
    nx_r, nx_i = pw_r[1:][:, :, None, :], pw_i[1:][:, :, None, :]
    wc = jnp.stack([cr[None] * nx_r - ci[None] * nx_i, -(cr[None] * nx_i + ci[None] * nx_r)])
    wc = wc.reshape(2, q, nv, gl, hh, n).transpose(2, 0, 3, 5, 1, 4)
    w1 = expand(wc.reshape(nv, SSM_VW, LANES), n, hh)
    return {
        "bblk": bblk.astype(BF16), "cblk": cblk.astype(BF16),
        "d": d.reshape(1, SSM_WIDTH).astype(F32),
        "glu_w": glu_w.astype(BF16), "glu_b": glu_b.reshape(1, SSM_WIDTH).astype(F32),
        "powers": scan_tables(ar, ai, flat),
        "chunk_powers": scan_tables(pw_r[q], pw_i[q], tiled),
        "tz": tz.astype(BF16), "w2": w2.astype(BF16), "w1": w1.astype(BF16),
    }


def _row(x):
    return x.reshape(1, -1).astype(F32)


def _pad_cols(w, n):
    return jnp.pad(w, ((0, 0), (0, n - w.shape[1])))


def kernel(x, meta_tokens, ev_norm_mix, ev_w_in, ev_conv_w, ev_conv_b, ev_gate_a_w, ev_gate_a_b, ev_gate_x_w, ev_gate_x_b, ev_lru_lambda, ev_ssm_lambda_re, ev_ssm_lambda_im, ev_ssm_log_dt, ev_ssm_b_re, ev_ssm_b_im, ev_ssm_c_re, ev_ssm_c_im, ev_ssm_d, ev_ssm_glu_w, ev_ssm_glu_b, ev_w_out, ev_norm_ffn, ev_ffn_w_gate, ev_ffn_w_up, ev_ffn_w_down, od_norm_mix, od_w_in, od_gla_gate_w2, od_gla_gate_b, od_gla_norm, od_w_out, od_norm_ffn, od_router_w, od_moe_w_gate, od_moe_w_up, od_moe_w_down, final_norm):
    nb, seq, _ = x.shape
    rows = nb * seq
    assert ev_w_in.shape[0] == 1 and od_w_in.shape[0] == 1, "two-layer trunk only"

    w_in0 = ev_w_in[0].astype(BF16)
    lru_p = {
        "conv_w": ev_conv_w[0].astype(F32), "conv_b": _row(ev_conv_b[0]),
        "gate_a_w": ev_gate_a_w[0].astype(BF16), "gate_a_b": _row(ev_gate_a_b[0]),
        "gate_x_w": ev_gate_x_w[0].astype(BF16), "gate_x_b": _row(ev_gate_x_b[0]),
        "softplus": _row(jax.nn.softplus(-ev_lru_lambda[0].astype(F32))),
    }
    ssm_p = _ssm_params(ev_ssm_lambda_re[0], ev_ssm_lambda_im[0], ev_ssm_log_dt[0], ev_ssm_b_re[0],
                        ev_ssm_b_im[0], ev_ssm_c_re[0], ev_ssm_c_im[0], ev_ssm_d[0],
                        ev_ssm_glu_w[0], ev_ssm_glu_b[0])
    w_out0 = ev_w_out[0].astype(BF16)
    w_out0_a, w_out0_b = w_out0[:D_MODEL], w_out0[D_MODEL:]
    ffn_g, ffn_u, ffn_d = (w[0].astype(BF16) for w in (ev_ffn_w_gate, ev_ffn_w_up, ev_ffn_w_down))
    odd_in = 2 * GLA_KEY + 2 * GLA_VAL
    w_in1 = _pad_cols(od_w_in[0], odd_in + LANES).astype(BF16)
    w2 = jnp.pad(od_gla_gate_w2[0].astype(F32), ((0, LANES - GLA_RANK), (0, 0)))
    w2_hi = w2.astype(BF16)
    w2_split = jnp.stack([w2_hi, (w2 - w2_hi.astype(F32)).astype(BF16)])
    wr = _pad_cols(od_router_w[0].astype(F32), LANES)
    wr_hi = wr.astype(BF16)
    wr_split = jnp.concatenate([wr_hi, (wr - wr_hi.astype(F32)).astype(BF16)], axis=1)
    w_out1 = od_w_out[0].astype(BF16)
    moe_g, moe_u, moe_d = (w[0].astype(BF16) for w in (od_moe_w_gate, od_moe_w_up, od_moe_w_down))

    tl = min(512, seq)
    tm = min(1024, rows)
    tm_ffn = min(512, rows)
    ck = min(128, tl)
    tl_ssm = min(2048, seq)
    tl_lru = min(1024, seq)

    def even_mixer(h, nbatch, tile_m, tile_f, tile_l, conv0, h0, s0):
        permuted = tile_l % PERM_BLOCK == 0
        a_out, u, conv_t, h_t = _in_proj_lru(h, _row(ev_norm_mix[0]), w_in0, lru_p, conv0, h0, nbatch,
                                             tl_lru if permuted else tile_l, permuted)
        if permuted:
            b_out, s_t = _ssm_chunked(u, ssm_p, s0, nbatch, tl_ssm)
        else:
            b_out, s_t = _ssm(u, ssm_p, s0, nbatch, tile_l)
            s_t = s_t.reshape(SUBLANES, 2, SSM_TILES, SSM_LANES // SSM_TILES).transpose(0, 2, 1, 3).reshape(
                SUBLANES, 2 * SSM_LANES)
        h = _mix_ffn(a_out, b_out, w_out0_a, w_out0_b, h, _row(ev_norm_ffn[0]), ffn_g, ffn_u, ffn_d, tile_f,
                     permuted)
        return h, (conv_t, h_t, s_t)

    def gla_inputs(h, tile_m):
        return _gla_proj(h, _row(od_norm_mix[0]), w_in1, w2_split, _row(od_gla_gate_b[0]), tile_m)

    zeros = lambda *s: jnp.zeros(s, F32)
    hm = meta_tokens.astype(F32)
    hm, (conv_m, h_m, s_m) = even_mixer(hm, 1, N_META, N_META, N_META, zeros(SUBLANES, D_MODEL),
                                        zeros(SUBLANES, D_MODEL), zeros(SUBLANES, 2 * SSM_LANES))
    qm, km, vm, ogm, lgm = gla_inputs(hm, N_META)
    _, gla_s = _gla(qm, km, vm, ogm, lgm, _row(od_gla_norm[0]),
                    zeros(GLA_HEADS, GLA_DV, GLA_DK), 1, N_META, N_META)

    h = x.reshape(rows, D_MODEL).astype(F32)
    h, _ = even_mixer(h, nb, tm, tm_ffn, tl, conv_m, h_m, s_m)
    q, k, v, og, lg = gla_inputs(h, tm)
    o, _ = _gla(q, k, v, og, lg, _row(od_gla_norm[0]), gla_s, nb, tl, ck)
    out = _moe(o, w_out1, h, _row(od_norm_ffn[0]), wr_split, moe_g, moe_u, moe_d, _row(final_norm))
    return out.reshape(nb, seq, D_MODEL)
```
